```python
import jax, jax.numpy as jnp
from jax import lax
import numpy as np

D_MODEL = 2048
BATCH = 4
SEQ = 4096
DEPTH = 1

N_META = 16
D_ATT = D_MODEL // 2
D_MLSTM = D_MODEL - D_ATT
ATT_HEADS = 8
QK_NOPE = 128
QK_ROPE = 64
V_HEAD = D_ATT // ATT_HEADS
Q_LORA = D_MODEL // 4
KV_LORA = D_MODEL // 8
MLSTM_HEADS = 4
MLSTM_HEAD = D_MLSTM // MLSTM_HEADS
CONV_K = 5
CHUNK = 64
META_PAD = (-N_META) % CHUNK
D_FF = 256 * ((8 * D_MODEL // 3 + 255) // 256)
D_IN = Q_LORA + KV_LORA + QK_ROPE + 2 * D_MLSTM
Q_BLOCK = 128
ROPE_BASE = 10000.0
LN_EPS = 1e-5
RMS_EPS = 1e-6
NEG = -1e30
ALPHA = (2 * DEPTH) ** 0.25
BETA = (8 * DEPTH) ** -0.25

kernel_name = 'hybrid_mla_mlstm_macaron_deepnorm_layer'


def _layer_norm(x, g, b):
    xf = x.astype(jnp.float32)
    mu = xf.mean(-1, keepdims=True)
    var = jnp.mean(jnp.square(xf - mu), -1, keepdims=True)
    return ((xf - mu) * lax.rsqrt(var + LN_EPS) * g + b).astype(x.dtype)


def _rms_norm(x, g):
    xf = x.astype(jnp.float32)
    return (xf * lax.rsqrt(jnp.mean(jnp.square(xf), -1, keepdims=True) + RMS_EPS) * g).astype(x.dtype)


def _swiglu(x, w_gate, w_up, w_down):
    return (jax.nn.silu(x @ w_gate) * (x @ w_up)) @ w_down


def _rope_tables(L):
    pos = jnp.arange(L, dtype=jnp.float32)
    inv = ROPE_BASE ** (-jnp.arange(0, QK_ROPE, 2, dtype=jnp.float32) / QK_ROPE)
    ang = pos[:, None] * inv[None, :]
    return jnp.cos(ang), jnp.sin(ang)


def _apply_rope(x, cos, sin):
    xf = x.astype(jnp.float32)
    x1, x2 = jnp.split(xf, 2, axis=-1)
    c, s = cos[:, None, :], sin[:, None, :]
    return jnp.concatenate([x1 * c - x2 * s, x2 * c + x1 * s], -1).astype(x.dtype)


def _mla(u_q, u_kv, u_kr, q_norm_g, w_uq, kv_norm_g, w_ukv, out_g, cos, sin):
    B, L, _ = u_q.shape
    H = ATT_HEADS
    q = (_rms_norm(u_q, q_norm_g) @ w_uq).reshape(B, L, H, QK_NOPE + QK_ROPE)
    q_nope, q_rope = q[..., :QK_NOPE], _apply_rope(q[..., QK_NOPE:], cos, sin)
    kv = (_rms_norm(u_kv, kv_norm_g) @ w_ukv).reshape(B, L, H, QK_NOPE + V_HEAD)
    k_nope, v = kv[..., :QK_NOPE], kv[..., QK_NOPE:]
    k_rope = _apply_rope(u_kr[:, :, None, :], cos, sin)[:, :, 0]
    scale = (QK_NOPE + QK_ROPE) ** -0.5
    n_blk = -(-L // Q_BLOCK)
    pad = n_blk * Q_BLOCK - L

    def to_blocks(a):
        a = jnp.pad(a, ((0, 0), (0, pad), (0, 0), (0, 0)))
        return a.reshape(B, n_blk, Q_BLOCK, H, a.shape[-1]).transpose(1, 0, 2, 3, 4)

    def block(args):
        qn_b, qr_b = args
        s = (jnp.einsum('bqhd,bkhd->bhqk', qn_b, k_nope)
             + jnp.einsum('bqhd,bkd->bhqk', qr_b, k_rope))
        p = jax.nn.softmax(s.astype(jnp.float32) * scale, axis=-1).astype(v.dtype)
        return jnp.einsum('bhqk,bkhd->bqhd', p, v)

    o = lax.map(block, (to_blocks(q_nope), to_blocks(q_rope)))
    o = o.transpose(1, 0, 2, 3, 4).reshape(B, n_blk * Q_BLOCK, H * V_HEAD)[:, :L]
    return _rms_norm(o, out_g)


def _mlstm_chunkwise(q, k, v, li, lf):
    B, H, T, dk = q.shape
    dv = v.shape[-1]
    nc = T // CHUNK
    mask = jnp.tril(jnp.ones((CHUNK, CHUNK), bool))

    def to_chunks(a):
        return jnp.moveaxis(a.reshape(B, H, nc, CHUNK, *a.shape[3:]), 2, 0)

    def step(carry, xs):
        C, n, m = carry
        qc, kc, vc, lic, lfc = xs
        b = jnp.cumsum(lfc, -1)
        D = jnp.where(mask, b[..., :, None] - b[..., None, :] + lic[..., None, :], NEG)
        m_inter = b + m[..., None]
        m_t = jnp.maximum(m_inter, D.max(-1))
        A = jnp.exp(D - m_t[..., None]) * jnp.einsum('bhtd,bhsd->bhts', qc, kc)
        w_inter = jnp.exp(m_inter - m_t)
        num = (jnp.einsum('bhts,bhsv->bhtv', A, vc)
               + w_inter[..., None] * jnp.einsum('bhtd,bhdv->bhtv', qc, C))
        den = A.sum(-1) + w_inter * jnp.einsum('bhtd,bhd->bht', qc, n)
        h = num / jnp.maximum(jnp.abs(den), jnp.exp(-m_t))[..., None]
        bL = b[..., -1]
        g = bL[..., None] - b + lic
        m_new = jnp.maximum(bL + m, g.max(-1))
        decay = jnp.exp(bL + m - m_new)
        wc = jnp.exp(g - m_new[..., None])
        C = decay[..., None, None] * C + jnp.einsum('bhs,bhsd,bhsv->bhdv', wc, kc, vc)
        n = decay[..., None] * n + jnp.einsum('bhs,bhsd->bhd', wc, kc)
        return (C, n, m_new), h

    init = (jnp.zeros((B, H, dk, dv), jnp.float32), jnp.zeros((B, H, dk), jnp.float32),
            jnp.zeros((B, H), jnp.float32))
    _, h = lax.scan(step, init, tuple(to_chunks(a) for a in (q, k, v, li, lf)))
    return jnp.moveaxis(h, 0, 2).reshape(B, H, T, dv)


def _centred_dwconv(x, w, b):
    K = w.shape[0]
    y = lax.conv_general_dilated(x, w[:, None, :].astype(x.dtype), window_strides=(1,),
                                 padding=[(K // 2, K // 2)], dimension_numbers=('NWC', 'WIO', 'NWC'),
                                 feature_group_count=x.shape[-1])
    return y + b


def _mlstm_mixer(x_m, z, conv_w, conv_b, w_q, w_k, w_v, w_gates, b_gates, gn_g, skip):
    B, L, _ = x_m.shape
    H, dh = MLSTM_HEADS, MLSTM_HEAD
    f32 = jnp.float32
    x_c = jax.nn.silu(_centred_dwconv(x_m, conv_w, conv_b))
    xc_h = x_c.reshape(B, L, H, dh)
    xm_h = x_m.reshape(B, L, H, dh)
    q = jnp.einsum('blhd,hde->bhle', xc_h, w_q).astype(f32)
    k = (jnp.einsum('blhd,hde->bhle', xc_h, w_k) * dh ** -0.5).astype(f32)
    v = jnp.einsum('blhd,hde->bhle', xm_h, w_v).astype(f32)
    g = (jnp.concatenate([x_c, x_m], -1) @ w_gates + b_gates).astype(f32)
    g = g.reshape(B, L, 4, H).transpose(2, 0, 3, 1)
    tpad = ((0, 0), (0, 0), (META_PAD, 0))
    pad4 = tpad + ((0, 0),)
    q, k, v = [jnp.pad(a, pad4) for a in (q, k, v)]
    li_f = jnp.pad(g[0], tpad, constant_values=NEG)
    lf_f = jnp.pad(jax.nn.log_sigmoid(g[1]), tpad)
    li_b = jnp.pad(g[2], tpad, constant_values=NEG)
    lf_b = jnp.pad(jax.nn.log_sigmoid(g[3]), tpad)
    h_f = _mlstm_chunkwise(q, k, v, li_f, lf_f)
    flip = lambda a: jnp.flip(a, axis=2)
    h_b = flip(_mlstm_chunkwise(flip(q), flip(k), flip(v), flip(li_b), flip(lf_b)))
    h = (h_f + h_b)[:, :, META_PAD:].transpose(0, 2, 1, 3)
    h = jax.nn.sigmoid(z.astype(f32)).reshape(B, L, H, dh) * h
    mu = h.mean(-1, keepdims=True)
    var = jnp.mean(jnp.square(h - mu), -1, keepdims=True)
    h = ((h - mu) * lax.rsqrt(var + LN_EPS)).reshape(B, L, D_MLSTM)
    return (h * gn_g + skip * x_c).astype(x_m.dtype)


def _normal(key, shape, scale):
    return scale * jax.random.normal(key, shape, jnp.float32)


def setup_inputs(seed: int = 0) -> dict:
    key = jax.random.key(seed)
    ks = iter(jax.random.split(key, 40))

    def w(shape, fan_in, mult=1.0):
        return _normal(next(ks), (DEPTH,) + shape, mult * fan_in ** -0.5)

    def gain(n):
        return 1.0 + _normal(next(ks), (DEPTH, n), 0.02)

    def bias(n):
        return _normal(next(ks), (DEPTH, n), 0.02)

    H = MLSTM_HEADS
    x = _normal(next(ks), (BATCH, SEQ, D_MODEL), 1.0)
    meta_tokens = _normal(next(ks), (N_META, D_MODEL), 1.0)
    ffn1_w_gate = w((D_MODEL, D_FF), D_MODEL)
    ffn1_w_up = w((D_MODEL, D_FF), D_MODEL)
    ffn1_w_down = w((D_FF, D_MODEL), D_FF, BETA)
    ln1_g, ln1_b = gain(D_MODEL), bias(D_MODEL)
    w_in = w((D_MODEL, D_IN), D_MODEL)
    mla_q_norm_g = gain(Q_LORA)
    mla_w_uq = w((Q_LORA, ATT_HEADS * (QK_NOPE + QK_ROPE)), Q_LORA)
    mla_kv_norm_g = gain(KV_LORA)
    mla_w_ukv = w((KV_LORA, ATT_HEADS * (QK_NOPE + V_HEAD)), KV_LORA)
    attn_out_g = gain(D_ATT)
    mlstm_conv_w = w((CONV_K, D_MLSTM), CONV_K)
    mlstm_conv_b = bias(D_MLSTM)
    mlstm_w_q = w((H, MLSTM_HEAD, MLSTM_HEAD), MLSTM_HEAD)
    mlstm_w_k = w((H, MLSTM_HEAD, MLSTM_HEAD), MLSTM_HEAD)
    mlstm_w_v = w((H, MLSTM_HEAD, MLSTM_HEAD), MLSTM_HEAD)
    mlstm_w_gates = w((2 * D_MLSTM, 4 * H), 2 * D_MLSTM)
    i_bias = _normal(next(ks), (DEPTH, 2, H), 0.1)
    f_bias = jnp.linspace(3.0, 6.0, H, dtype=jnp.float32)[None, None, :] + _normal(next(ks), (DEPTH, 2, H), 0.1)
    mlstm_b_gates = jnp.stack([i_bias, f_bias], axis=2).reshape(DEPTH, 4 * H)
    mlstm_gn_g = gain(D_MLSTM)
    mlstm_skip = gain(D_MLSTM)
    w_out = w((D_MODEL, D_MODEL), D_MODEL, BETA)
    ln2_g, ln2_b = gain(D_MODEL), bias(D_MODEL)
    ffn2_w_gate = w((D_MODEL, D_FF), D_MODEL)
    ffn2_w_up = w((D_MODEL, D_FF), D_MODEL)
    ffn2_w_down = w((D_FF, D_MODEL), D_FF, BETA)
    ln3_g, ln3_b = gain(D_MODEL), bias(D_MODEL)
    return {'x': x, 'meta_tokens': meta_tokens,
            'ffn1_w_gate': ffn1_w_gate, 'ffn1_w_up': ffn1_w_up, 'ffn1_w_down': ffn1_w_down,
            'ln1_g': ln1_g, 'ln1_b': ln1_b, 'w_in': w_in,
            'mla_q_norm_g': mla_q_norm_g, 'mla_w_uq': mla_w_uq, 'mla_kv_norm_g': mla_kv_norm_g,
            'mla_w_ukv': mla_w_ukv, 'attn_out_g': attn_out_g,
            'mlstm_conv_w': mlstm_conv_w, 'mlstm_conv_b': mlstm_conv_b, 'mlstm_w_q': mlstm_w_q,
            'mlstm_w_k': mlstm_w_k, 'mlstm_w_v': mlstm_w_v, 'mlstm_w_gates': mlstm_w_gates,
            'mlstm_b_gates': mlstm_b_gates, 'mlstm_gn_g': mlstm_gn_g, 'mlstm_skip': mlstm_skip,
            'w_out': w_out, 'ln2_g': ln2_g, 'ln2_b': ln2_b,
            'ffn2_w_gate': ffn2_w_gate, 'ffn2_w_up': ffn2_w_up, 'ffn2_w_down': ffn2_w_down,
            'ln3_g': ln3_g, 'ln3_b': ln3_b}


def reference(x, meta_tokens, ffn1_w_gate, ffn1_w_up, ffn1_w_down, ln1_g, ln1_b, w_in,
              mla_q_norm_g, mla_w_uq, mla_kv_norm_g, mla_w_ukv, attn_out_g,
              mlstm_conv_w, mlstm_conv_b, mlstm_w_q, mlstm_w_k, mlstm_w_v, mlstm_w_gates,
              mlstm_b_gates, mlstm_gn_g, mlstm_skip, w_out, ln2_g, ln2_b,
              ffn2_w_gate, ffn2_w_up, ffn2_w_down, ln3_g, ln3_b):
    B = x.shape[0]
    meta = jnp.broadcast_to(meta_tokens[None].astype(x.dtype), (B, N_META, x.shape[-1]))
    h = jnp.concatenate([meta, x], axis=1)
    L = h.shape[1]
    cos, sin = _rope_tables(L)
    o1 = Q_LORA
    o2 = o1 + KV_LORA
    o3 = o2 + QK_ROPE
    o4 = o3 + D_MLSTM
    for d in range(DEPTH):
        h = _layer_norm(ALPHA * h + 0.5 * _swiglu(h, ffn1_w_gate[d], ffn1_w_up[d], ffn1_w_down[d]), ln1_g[d], ln1_b[d])
        u = h @ w_in[d]
        y_att = _mla(u[..., :o1], u[..., o1:o2], u[..., o2:o3], mla_q_norm_g[d], mla_w_uq[d],
                     mla_kv_norm_g[d], mla_w_ukv[d], attn_out_g[d], cos, sin)
        y_mlstm = _mlstm_mixer(u[..., o3:o4], u[..., o4:], mlstm_conv_w[d], mlstm_conv_b[d],
                               mlstm_w_q[d], mlstm_w_k[d], mlstm_w_v[d], mlstm_w_gates[d],
                               mlstm_b_gates[d], mlstm_gn_g[d], mlstm_skip[d])
        y = jnp.concatenate([y_att, y_mlstm], axis=-1) @ w_out[d]
        h = _layer_norm(ALPHA * h + y, ln2_g[d], ln2_b[d])
        h = _layer_norm(ALPHA * h + 0.5 * _swiglu(h, ffn2_w_gate[d], ffn2_w_up[d], ffn2_w_down[d]), ln3_g[d], ln3_b[d])
    return h[:, N_META:]
```

```python
import functools

import jax
import jax.numpy as jnp
from jax import lax
from jax.experimental import pallas as pl
from jax.experimental.pallas import tpu as pltpu

F32 = jnp.float32
BF16 = jnp.bfloat16

N_META = 16
ATT_HEADS = 8
QK_NOPE = 128
QK_ROPE = 64
MLSTM_HEADS = 4
CONV_K = 5
ROPE_BASE = 10000.0
LN_EPS = 1e-5
RMS_EPS = 1e-6
DEPTH = 1
ALPHA = (2 * DEPTH) ** 0.25

LANES = 128
SUBLANES = 8
VMEM_LIMIT = 56 * 1024 * 1024

MLSTM_CHUNK = 256
GATE_GROUP = 8
LOWEST = float(jnp.finfo(jnp.float32).min)
NEG = -1e30


def _dot(a, b):
    return jnp.dot(a, b, preferred_element_type=F32)


def _dot_nt(a, b):
    return lax.dot_general(a, b, (((1,), (1,)), ((), ())), preferred_element_type=F32)


def _dot_tn(a, b):
    return lax.dot_general(a, b, (((0,), (0,)), ((), ())), preferred_element_type=F32)


def _layer_norm(y, g, b):
    mu = jnp.mean(y, axis=-1, keepdims=True)
    yc = y - mu
    var = jnp.mean(yc * yc, axis=-1, keepdims=True)
    return yc * lax.rsqrt(var + LN_EPS) * g + b


def _rms_norm(y, g):
    return y * lax.rsqrt(jnp.mean(y * y, axis=-1, keepdims=True) + RMS_EPS) * g


def _params(*sem):
    return pltpu.CompilerParams(dimension_semantics=sem, vmem_limit_bytes=VMEM_LIMIT)


def _ffn_ln_body(x_ref, wg_ref, wu_ref, wd_ref, g_ref, b_ref, o_ref, acc_ref, xb_ref, *, nf):
    f = pl.program_id(1)

    @pl.when(f == 0)
    def _():
        xb_ref[...] = x_ref[...].astype(BF16)

    xb = xb_ref[...]
    gate = _dot(xb, wg_ref[...])
    up = _dot(xb, wu_ref[...])
    act = (gate * jax.nn.sigmoid(gate) * up).astype(BF16)
    part = _dot(act, wd_ref[...])

    @pl.when(f == 0)
    def _():
        acc_ref[...] = part

    @pl.when(f > 0)
    def _():
        acc_ref[...] += part

    @pl.when(f == nf - 1)
    def _():
        y = ALPHA * x_ref[...] + 0.5 * acc_ref[...]
        o_ref[...] = _layer_norm(y, g_ref[...], b_ref[...])


def _ffn_ln(x, wg, wu, wd, g, b, *, tm, tf):
    n, d = x.shape
    dff = wg.shape[1]
    nf = dff // tf
    assert n % tm == 0 and dff % tf == 0
    return pl.pallas_call(
        functools.partial(_ffn_ln_body, nf=nf),
        grid=(n // tm, nf),
        in_specs=[
            pl.BlockSpec((tm, d), lambda i, f: (i, 0)),
            pl.BlockSpec((d, tf), lambda i, f: (0, f)),
            pl.BlockSpec((d, tf), lambda i, f: (0, f)),
            pl.BlockSpec((tf, d), lambda i, f: (f, 0)),
            pl.BlockSpec((1, d), lambda i, f: (0, 0)),
            pl.BlockSpec((1, d), lambda i, f: (0, 0)),
        ],
        out_specs=pl.BlockSpec((tm, d), lambda i, f: (i, 0)),
        out_shape=jax.ShapeDtypeStruct((n, d), F32),
        scratch_shapes=[pltpu.VMEM((tm, d), F32), pltpu.VMEM((tm, d), BF16)],
        compiler_params=_params("parallel", "arbitrary"),
        name="ffn_ln",
    )(x, wg, wu, wd, g, b)


def _proj_body(h_ref, win_ref, qg_ref, kvg_ref, wqn_ref, wqr_ref, wqs_ref, wkn_ref, wv_ref,
               cos_ref, sin_ref, q_ref, k_ref, v_ref, xm_ref, z_ref, *, ql, kvl, dm, scale):
    hb = h_ref[...].astype(BF16)
    u = _dot(hb, win_ref[...])
    o1 = ql
    o2 = o1 + kvl
    o3 = o2 + LANES
    o4 = o3 + LANES
    o5 = o4 + dm
    xm_ref[...] = u[:, o4:o5]
    z_ref[...] = u[:, o5:]
    c = cos_ref[...]
    s = sin_ref[...]
    qn = _rms_norm(u[:, :o1], qg_ref[...]).astype(BF16)
    qa = _dot(qn, wqn_ref[...])
    qr = _dot(qn, wqr_ref[...])
    qs = _dot(qn, wqs_ref[...])
    kvn = _rms_norm(u[:, o1:o2], kvg_ref[...]).astype(BF16)
    kn = _dot(kvn, wkn_ref[...])
    v_ref[...] = _dot(kvn, wv_ref[...]).astype(BF16)
    k_rope = (u[:, o2:o3] * c + u[:, o3:o4] * s).astype(BF16)
    for h in range(ATT_HEADS):
        sl = slice(h * LANES, (h + 1) * LANES)
        lo = 2 * h * LANES
        q_ref[:, lo:lo + LANES] = (qa[:, sl] * scale).astype(BF16)
        q_ref[:, lo + LANES:lo + 2 * LANES] = ((qr[:, sl] * c + qs[:, sl] * s) * scale).astype(BF16)
        k_ref[:, lo:lo + LANES] = kn[:, sl].astype(BF16)
        k_ref[:, lo + LANES:lo + 2 * LANES] = k_rope


def _proj(h, win, qg, kvg, wqn, wqr, wqs, wkn, wv, ctab, stab, *, tm, dm):
    n, d = h.shape
    ql = qg.shape[1]
    kvl = kvg.shape[1]
    da = wv.shape[1]
    npos = ctab.shape[0] // tm
    hq = ATT_HEADS * 2 * LANES
    const = lambda i: (0, 0)
    return pl.pallas_call(
        functools.partial(_proj_body, ql=ql, kvl=kvl, dm=dm, scale=float((QK_NOPE + QK_ROPE) ** -0.5)),
        grid=(n // tm,),
        in_specs=[
            pl.BlockSpec((tm, d), lambda i: (i, 0)),
            pl.BlockSpec(win.shape, const),
            pl.BlockSpec(qg.shape, const),
            pl.BlockSpec(kvg.shape, const),
            pl.BlockSpec(wqn.shape, const),
            pl.BlockSpec(wqr.shape, const),
            pl.BlockSpec(wqs.shape, const),
            pl.BlockSpec(wkn.shape, const),
            pl.BlockSpec(wv.shape, const),
            pl.BlockSpec((tm, LANES), lambda i: (i % npos, 0)),
            pl.BlockSpec((tm, LANES), lambda i: (i % npos, 0)),
        ],
        out_specs=[
            pl.BlockSpec((tm, hq), lambda i: (i, 0)),
            pl.BlockSpec((tm, hq), lambda i: (i, 0)),
            pl.BlockSpec((tm, da), lambda i: (i, 0)),
            pl.BlockSpec((tm, dm), lambda i: (i, 0)),
            pl.BlockSpec((tm, dm), lambda i: (i, 0)),
        ],
        out_shape=[
            jax.ShapeDtypeStruct((n, hq), BF16),
            jax.ShapeDtypeStruct((n, hq), BF16),
            jax.ShapeDtypeStruct((n, da), BF16),
            jax.ShapeDtypeStruct((n, dm), F32),
            jax.ShapeDtypeStruct((n, dm), F32),
        ],
        compiler_params=_params("parallel"),
        name="proj",
    )(h, win, qg, kvg, wqn, wqr, wqs, wkn, wv, ctab, stab)


def _log_sigmoid(x):
    return jnp.minimum(x, 0.0) - jnp.log1p(jnp.exp(-jnp.abs(x)))


def _gate_scans(gi, lf, chunk):
    row = lax.broadcasted_iota(jnp.int32, (chunk, LANES), 0)
    lane = lax.broadcasted_iota(jnp.int32, (chunk, LANES), 1)
    j = lane % GATE_GROUP
    fwd = j < 3

    def scan(val, combine, identity):
        pre = val
        suf = val
        k = 1
        while k < chunk:
            pre = combine(pre, jnp.where(row >= k, pltpu.roll(pre, k, 0), identity))
            suf = combine(suf, jnp.where(row < chunk - k, pltpu.roll(suf, chunk - k, 0), identity))
            k *= 2
        return jnp.where(fwd, pre, suf)

    b = scan(lf, jnp.add, 0.0)
    r = gi - b
    cm = scan(r, jnp.maximum, LOWEST)
    return jnp.where(j % 3 == 0, b, jnp.where(j % 3 == 1, cm, r))


def _prep_body(prev_ref, x_ref, next_ref, mtail_ref, cw_ref, cb_ref, wq_ref, wk_ref, wv_ref,
               wgc_ref, wgm_ref, bg_ref, *rest, tm, chunk, dh, meta, with_rows):
    if with_rows:
        q_ref, k_ref, v_ref, xc_ref, gc_ref, gr_ref, xs_ref = rest
    else:
        q_ref, k_ref, v_ref, xc_ref, gc_ref, xs_ref = rest
        gr_ref = None
    i = pl.program_id(1)
    nt = pl.num_programs(1)
    x = x_ref[...]
    if meta:
        prev = jnp.zeros_like(prev_ref[...])
        nxt = next_ref[...]
    else:
        prev = jnp.where(i == 0, mtail_ref[...], prev_ref[...])
        nxt = jnp.where(i == nt - 1, 0.0, next_ref[...])
    xs_ref[0:SUBLANES, :] = prev
    xs_ref[SUBLANES:SUBLANES + tm, :] = x
    xs_ref[SUBLANES + tm:2 * SUBLANES + tm, :] = nxt
    acc = jnp.broadcast_to(cb_ref[...], x.shape)
    for t in range(CONV_K):
        off = SUBLANES - CONV_K // 2 + t
        acc = acc + cw_ref[t:t + 1, :] * xs_ref[off:off + tm, :]
    xc = acc * jax.nn.sigmoid(acc)
    xcb = xc.astype(BF16)
    xmb = x.astype(BF16)
    xc_ref[...] = xcb
    for h in range(MLSTM_HEADS):
        sl = slice(h * dh, (h + 1) * dh)
        q_ref[:, sl] = _dot(xcb[:, sl], wq_ref[h]).astype(BF16)
        k_ref[:, sl] = (_dot(xcb[:, sl], wk_ref[h]) * (dh ** -0.5)).astype(BF16)
        v_ref[:, sl] = _dot(xmb[:, sl], wv_ref[h]).astype(BF16)
    g = _dot(xcb, wgc_ref[...]) + _dot(xmb, wgm_ref[...]) + bg_ref[...]
    gi = g[:, :LANES]
    lf = _log_sigmoid(g[:, LANES:])
    for ci in range(tm // chunk):
        rows = slice(ci * chunk, (ci + 1) * chunk)
        out = _gate_scans(gi[rows], lf[rows], chunk)
        gc_ref[rows, :] = out
        if gr_ref is not None:
            gr_ref[ci] = out.T


def _prep(xm, xm_meta, cw, cb, wq, wk, wv, wgc, wgm, bg, *, batch, tm, chunk, meta):
    dm = xm.shape[1]
    dh = dm // MLSTM_HEADS
    seq = xm.shape[0] // batch
    if meta:
        nt = 1
        n_out = batch * N_META
        x_arr = xm_meta
        x_spec = pl.BlockSpec((tm, dm), lambda b, i: (0, 0))
        prev_spec = pl.BlockSpec((SUBLANES, dm), lambda b, i: (0, 0))
        next_spec = pl.BlockSpec((SUBLANES, dm), lambda b, i: (b * (seq // SUBLANES), 0))
    else:
        nt = seq // tm
        n_out = batch * seq
        x_arr = xm
        last_blk = batch * seq // SUBLANES - 1
        x_spec = pl.BlockSpec((tm, dm), lambda b, i: (b * nt + i, 0))
        prev_spec = pl.BlockSpec(
            (SUBLANES, dm), lambda b, i: (jnp.maximum((b * nt + i) * (tm // SUBLANES) - 1, 0), 0))
        next_spec = pl.BlockSpec(
            (SUBLANES, dm), lambda b, i: (jnp.minimum((b * nt + i + 1) * (tm // SUBLANES), last_blk), 0))
    const2 = lambda b, i: (0, 0)
    const3 = lambda b, i: (0, 0, 0)
    row_spec = lambda w: pl.BlockSpec((tm, w), lambda b, i: (b * nt + i, 0))
    out_specs = [row_spec(dm), row_spec(dm), row_spec(dm), row_spec(dm), row_spec(LANES)]
    out_shape = [jax.ShapeDtypeStruct((n_out, dm), BF16)] * 4 + [jax.ShapeDtypeStruct((n_out, LANES), F32)]
    with_rows = not meta
    if with_rows:
        ncb = tm // chunk
        out_specs.append(pl.BlockSpec((ncb, LANES, chunk), lambda b, i: (b * nt + i, 0, 0)))
        out_shape.append(jax.ShapeDtypeStruct((n_out // chunk, LANES, chunk), F32))
    return pl.pallas_call(
        functools.partial(_prep_body, tm=tm, chunk=chunk, dh=dh, meta=meta, with_rows=with_rows),
        grid=(batch, nt),
        in_specs=[
            prev_spec, x_spec, next_spec,
            pl.BlockSpec((SUBLANES, dm), lambda b, i: (1, 0)),
            pl.BlockSpec(cw.shape, const2),
            pl.BlockSpec(cb.shape, const2),
            pl.BlockSpec(wq.shape, const3),
            pl.BlockSpec(wk.shape, const3),
            pl.BlockSpec(wv.shape, const3),
            pl.BlockSpec(wgc.shape, const2),
            pl.BlockSpec(wgm.shape, const2),
            pl.BlockSpec(bg.shape, const2),
        ],
        out_specs=out_specs,
        out_shape=out_shape,
        scratch_shapes=[pltpu.VMEM((tm + 2 * SUBLANES, dm), F32)],
        compiler_params=_params("parallel", "parallel"),
        name="mlstm_prep_meta" if meta else "mlstm_prep",
    )(xm, x_arr, xm, xm_meta, cw, cb, wq, wk, wv, wgc, wgm, bg)


def _scan_chain(q, k, v, gc, r_row, c_ref, n_ref, m_ref, mask, edge, lane0):
    b_col = gc[:, lane0:lane0 + 1]
    cm_col = gc[:, lane0 + 1:lane0 + 2]
    r_col = gc[:, lane0 + 2:lane0 + 3]
    m_old = m_ref[...]
    c_old = c_ref[...]
    n_old = n_ref[...]

    s = _dot_nt(q, k)
    mt = jnp.maximum(m_old, cm_col)
    a = jnp.exp(jnp.where(mask, r_row - mt, NEG)) * s
    w_inter = jnp.exp(m_old - mt)
    num = _dot(a.astype(BF16), v) + w_inter * _dot(q, c_old.astype(BF16))
    qn = jnp.sum(q.astype(F32) * n_old, axis=1, keepdims=True)
    den = jnp.sum(a, axis=1, keepdims=True) + w_inter * qn
    scale = 1.0 / jnp.maximum(jnp.abs(den), jnp.exp(-(b_col + mt)))
    h = num * scale

    b_end = b_col[edge:edge + 1, :]
    g_max = b_end + cm_col[edge:edge + 1, :]
    m_new = jnp.maximum(b_end + m_old, g_max)
    decay = jnp.exp(b_end + m_old - m_new)
    kw = k.astype(F32) * jnp.exp(b_end + r_col - m_new)
    c_ref[...] = decay * c_old + _dot_tn(kw.astype(BF16), v)
    n_ref[...] = decay * n_old + jnp.sum(kw, axis=0, keepdims=True)
    m_ref[...] = m_new
    return h


def _scan_body(qf_ref, kf_ref, vf_ref, gcf_ref, grf_ref, qb_ref, kb_ref, vb_ref, gcb_ref, grb_ref,
               km_ref, vm_ref, gcm_ref, hf_ref, hb_ref, c_ref, n_ref, m_ref, *, chunk, dh):
    ci = pl.program_id(1)
    nh = MLSTM_HEADS

    @pl.when(ci == 0)
    def _():
        gcm = gcm_ref[0]
        km = km_ref[0]
        vm = vm_ref[0]
        for h in range(nh):
            sl = slice(h * dh, (h + 1) * dh)
            lane0 = h * GATE_GROUP
            b_end = gcm[N_META - 1:N_META, lane0:lane0 + 1]
            m_new = jnp.maximum(b_end, b_end + gcm[N_META - 1:N_META, lane0 + 1:lane0 + 2])
            kw = km[:, sl].astype(F32) * jnp.exp(b_end + gcm[:, lane0 + 2:lane0 + 3] - m_new)
            c_ref[h] = _dot_tn(kw.astype(BF16), vm[:, sl])
            n_ref[h] = jnp.sum(kw, axis=0, keepdims=True)
            m_ref[h] = m_new
            c_ref[nh + h] = jnp.zeros((dh, dh), F32)
            n_ref[nh + h] = jnp.zeros((1, dh), F32)
            m_ref[nh + h] = jnp.zeros((1, 1), F32)

    row = lax.broadcasted_iota(jnp.int32, (chunk, chunk), 0)
    col = lax.broadcasted_iota(jnp.int32, (chunk, chunk), 1)
    gcf = gcf_ref[...]
    gcb = gcb_ref[...]
    for h in range(nh):
        sl = slice(h * dh, (h + 1) * dh)
        lane0 = h * GATE_GROUP
        hf_ref[:, sl] = _scan_chain(
            qf_ref[:, sl], kf_ref[:, sl], vf_ref[:, sl], gcf, grf_ref[0, lane0 + 2:lane0 + 3, :],
            c_ref.at[h], n_ref.at[h], m_ref.at[h], col <= row, chunk - 1, lane0)
        hb_ref[:, sl] = _scan_chain(
            qb_ref[:, sl], kb_ref[:, sl], vb_ref[:, sl], gcb, grb_ref[0, lane0 + 5:lane0 + 6, :],
            c_ref.at[nh + h], n_ref.at[nh + h], m_ref.at[nh + h], col >= row, 0, lane0 + 3)


def _scan(q, k, v, gc, gr, km, vm, gcm, *, batch, chunk):
    n, dm = q.shape
    dh = dm // MLSTM_HEADS
    nc = n // batch // chunk
    fwd = lambda b, c: (b * nc + c, 0)
    bwd = lambda b, c: (b * nc + nc - 1 - c, 0)
    fwd3 = lambda b, c: (b * nc + c, 0, 0)
    bwd3 = lambda b, c: (b * nc + nc - 1 - c, 0, 0)
    per_b = lambda b, c: (b, 0, 0)

    def specs(idx, idx3):
        return [pl.BlockSpec((chunk, dm), idx)] * 3 + [
            pl.BlockSpec((chunk, LANES), idx), pl.BlockSpec((1, LANES, chunk), idx3)]

    return pl.pallas_call(
        functools.partial(_scan_body, chunk=chunk, dh=dh),
        grid=(batch, nc),
        in_specs=specs(fwd, fwd3) + specs(bwd, bwd3) + [
            pl.BlockSpec((1, N_META, dm), per_b),
            pl.BlockSpec((1, N_META, dm), per_b),
            pl.BlockSpec((1, N_META, LANES), per_b),
        ],
        out_specs=[pl.BlockSpec((chunk, dm), fwd), pl.BlockSpec((chunk, dm), bwd)],
        out_shape=[jax.ShapeDtypeStruct((n, dm), F32)] * 2,
        scratch_shapes=[
            pltpu.VMEM((2 * MLSTM_HEADS, dh, dh), F32),
            pltpu.VMEM((2 * MLSTM_HEADS, 1, dh), F32),
            pltpu.VMEM((2 * MLSTM_HEADS, 1, 1), F32),
        ],
        compiler_params=_params("parallel", "arbitrary"),
        name="mlstm_scan",
    )(q, k, v, gc, gr, q, k, v, gc, gr,
      km.reshape(batch, N_META, dm), vm.reshape(batch, N_META, dm), gcm.reshape(batch, N_META, LANES))


def _attn_body(q_ref, k_ref, v_ref, km_ref, vm_ref, o_ref, *, tk, nk):
    q = q_ref[...]
    s0 = _dot_nt(q, km_ref[...])
    m0 = jnp.max(s0, axis=1, keepdims=True)
    p0 = jnp.exp(s0 - m0)
    l0 = jnp.sum(p0, axis=1, keepdims=True)
    acc0 = _dot(p0.astype(BF16), vm_ref[...])

    def step(j, carry):
        m, l, acc = carry
        start = pl.multiple_of(j * tk, tk)
        s = _dot_nt(q, k_ref[pl.ds(start, tk), :])
        m_new = jnp.maximum(m, jnp.max(s, axis=1, keepdims=True))
        alpha = jnp.exp(m - m_new)
        p = jnp.exp(s - m_new)
        l = alpha * l + jnp.sum(p, axis=1, keepdims=True)
        acc = alpha * acc + _dot(p.astype(BF16), v_ref[pl.ds(start, tk), :])
        return m_new, l, acc

    _, l, acc = lax.fori_loop(0, nk, step, (m0, l0, acc0))
    o_ref[...] = acc / l


def _attn(q, k, v, km, vm, *, batch, tq, tk):
    n = q.shape[0]
    seq = n // batch
    nq = seq // tq
    dv = v.shape[1] // ATT_HEADS
    dk = 2 * LANES
    return pl.pallas_call(
        functools.partial(_attn_body, tk=tk, nk=seq // tk),
        grid=(batch, ATT_HEADS, nq),
        in_specs=[
            pl.BlockSpec((tq, dk), lambda b, h, i: (b * nq + i, h)),
            pl.BlockSpec((seq, dk), lambda b, h, i: (b, h)),
            pl.BlockSpec((seq, dv), lambda b, h, i: (b, h)),
            pl.BlockSpec((N_META, dk), lambda b, h, i: (0, h)),
            pl.BlockSpec((N_META, dv), lambda b, h, i: (0, h)),
        ],
        out_specs=pl.BlockSpec((tq, dv), lambda b, h, i: (b * nq + i, h)),
        out_shape=jax.ShapeDtypeStruct((n, ATT_HEADS * dv), F32),
        compiler_params=_params("parallel", "parallel", "arbitrary"),
        name="attn",
    )(q, k, v, km, vm)


def _outproj_body(o_ref, hf_ref, hb_ref, z_ref, xc_ref, h1_ref, og_ref, gn_ref, sk_ref,
                  wa_ref, wm_ref, g_ref, b_ref, out_ref, *, dh):
    y_att = _rms_norm(o_ref[...], og_ref[...]).astype(BF16)
    hs = (hf_ref[...] + hb_ref[...]) * jax.nn.sigmoid(z_ref[...])
    parts = []
    for h in range(MLSTM_HEADS):
        seg = hs[:, h * dh:(h + 1) * dh]
        mu = jnp.mean(seg, axis=-1, keepdims=True)
        sc = seg - mu
        var = jnp.mean(sc * sc, axis=-1, keepdims=True)
        parts.append(sc * lax.rsqrt(var + LN_EPS))
    hn = jnp.concatenate(parts, axis=-1)
    y_ml = (hn * gn_ref[...] + sk_ref[...] * xc_ref[...].astype(F32)).astype(BF16)
    y = _dot(y_att, wa_ref[...]) + _dot(y_ml, wm_ref[...])
    out_ref[...] = _layer_norm(ALPHA * h1_ref[...] + y, g_ref[...], b_ref[...])


def _outproj(o, hf, hb, z, xc, h1, og, gn, sk, wa, wm, g, b, *, tm):
    n, d = h1.shape
    da = o.shape[1]
    dm = hf.shape[1]
    const = lambda i: (0, 0)
    row = lambda w: pl.BlockSpec((tm, w), lambda i: (i, 0))
    return pl.pallas_call(
        functools.partial(_outproj_body, dh=dm // MLSTM_HEADS),
        grid=(n // tm,),
        in_specs=[row(da), row(dm), row(dm), row(dm), row(dm), row(d),
                  pl.BlockSpec(og.shape, const), pl.BlockSpec(gn.shape, const), pl.BlockSpec(sk.shape, const),
                  pl.BlockSpec(wa.shape, const), pl.BlockSpec(wm.shape, const),
                  pl.BlockSpec(g.shape, const), pl.BlockSpec(b.shape, const)],
        out_specs=row(d),
        out_shape=jax.ShapeDtypeStruct((n, d), F32),
        compiler_params=_params("parallel"),
        name="outproj",
    )(o, hf, hb, z, xc, h1, og, gn, sk, wa, wm, g, b)


def _pad_lanes(w):
    return jnp.pad(w, [(0, 0)] * (w.ndim - 1) + [(0, LANES - w.shape[-1])])


def _pick(n, pref):
    for t in pref:
        if n % t == 0:
            return t
    raise ValueError(f"no tile for {n}")


def kernel(x, meta_tokens, ffn1_w_gate, ffn1_w_up, ffn1_w_down, ln1_g, ln1_b, w_in, mla_q_norm_g, mla_w_uq, mla_kv_norm_g, mla_w_ukv, attn_out_g, mlstm_conv_w, mlstm_conv_b, mlstm_w_q, mlstm_w_k, mlstm_w_v, mlstm_w_gates, mlstm_b_gates, mlstm_gn_g, mlstm_skip, w_out, ln2_g, ln2_b, ffn2_w_gate, ffn2_w_up, ffn2_w_down, ln3_g, ln3_b):
    batch, seq, d = x.shape
    assert ffn1_w_gate.shape[0] == DEPTH and meta_tokens.shape[0] == N_META
    n = batch * seq
    ql = mla_q_norm_g.shape[-1]
    kvl = mla_kv_norm_g.shape[-1]
    da = attn_out_g.shape[-1]
    dm = mlstm_gn_g.shape[-1]
    dh = dm // MLSTM_HEADS
    dv = da // ATT_HEADS
    nh = MLSTM_HEADS
    half = QK_ROPE // 2
    assert dv == LANES and dh % LANES == 0 and ql % LANES == 0 and kvl % LANES == 0
    assert seq % MLSTM_CHUNK == 0

    wi = w_in[0]
    o1, o2, o3, o4 = ql, ql + kvl, ql + kvl + QK_ROPE, ql + kvl + QK_ROPE + dm
    swap = jnp.concatenate([jnp.arange(half, QK_ROPE), jnp.arange(half)])
    w_kr = wi[:, o2:o3]
    win = jnp.concatenate(
        [wi[:, :o2], _pad_lanes(w_kr), _pad_lanes(w_kr[:, swap]), wi[:, o3:]], axis=1).astype(BF16)
    wuq = mla_w_uq[0].reshape(ql, ATT_HEADS, QK_NOPE + QK_ROPE)
    wqn = wuq[:, :, :QK_NOPE].reshape(ql, -1).astype(BF16)
    wqr = _pad_lanes(wuq[:, :, QK_NOPE:]).reshape(ql, -1).astype(BF16)
    wqs = _pad_lanes(wuq[:, :, QK_NOPE:][:, :, swap]).reshape(ql, -1).astype(BF16)
    wukv = mla_w_ukv[0].reshape(kvl, ATT_HEADS, QK_NOPE + dv)
    wkn = wukv[:, :, :QK_NOPE].reshape(kvl, -1).astype(BF16)
    wv = wukv[:, :, QK_NOPE:].reshape(kvl, -1).astype(BF16)

    wg = mlstm_w_gates[0].reshape(2 * dm, 4, nh)
    bgr = mlstm_b_gates[0].reshape(4, nh)

    def gate_tile(w4, kind_f, kind_b):
        zeros = jnp.zeros_like(w4[..., 0, :])
        cols = [w4[..., kind_f, :]] * 3 + [w4[..., kind_b, :]] * 3 + [zeros] * (GATE_GROUP - 6)
        tile = jnp.stack(cols, axis=-1).reshape(*w4.shape[:-2], nh * GATE_GROUP)
        return _pad_lanes(tile)

    wgate = jnp.concatenate([gate_tile(wg, 0, 2), gate_tile(wg, 1, 3)], axis=-1)
    wgc = wgate[:dm].astype(BF16)
    wgm = wgate[dm:].astype(BF16)
    bg = jnp.concatenate([gate_tile(bgr, 0, 2), gate_tile(bgr, 1, 3)], axis=-1)[None, :]
    cw = jnp.pad(mlstm_conv_w[0], ((0, SUBLANES - CONV_K), (0, 0)))
    cb = mlstm_conv_b[0][None, :]
    wq_m, wk_m, wv_m = (w[0].astype(BF16) for w in (mlstm_w_q, mlstm_w_k, mlstm_w_v))
    wo = w_out[0].astype(BF16)
    row = lambda p: p[0][None, :]

    pos = jnp.arange(N_META + seq, dtype=F32)
    inv = ROPE_BASE ** (-jnp.arange(0, QK_ROPE, 2, dtype=F32) / QK_ROPE)
    ang = pos[:, None] * inv[None, :]
    ctab = _pad_lanes(jnp.concatenate([jnp.cos(ang), jnp.cos(ang)], axis=-1))
    stab = _pad_lanes(jnp.concatenate([-jnp.sin(ang), jnp.sin(ang)], axis=-1))

    ffn1 = [w[0].astype(BF16) for w in (ffn1_w_gate, ffn1_w_up, ffn1_w_down)]
    ffn2 = [w[0].astype(BF16) for w in (ffn2_w_gate, ffn2_w_up, ffn2_w_down)]
    tf = _pick(ffn1[0].shape[1], (512, 256, 128))
    tm_ffn = _pick(n, (512, 256))
    tm_proj = _pick(seq, (256,))

    xr = x.reshape(n, d)
    h1 = _ffn_ln(xr, *ffn1, row(ln1_g), row(ln1_b), tm=tm_ffn, tf=tf)
    h1m = _ffn_ln(meta_tokens.astype(x.dtype), *ffn1, row(ln1_g), row(ln1_b), tm=N_META, tf=tf)
    proj_w = (win, row(mla_q_norm_g), row(mla_kv_norm_g), wqn, wqr, wqs, wkn, wv)
    q_a, k_a, v_a, xm, z = _proj(h1, *proj_w, ctab[N_META:], stab[N_META:], tm=tm_proj, dm=dm)
    _, k_am, v_am, xm_m, _ = _proj(h1m, *proj_w, ctab[:N_META], stab[:N_META], tm=N_META, dm=dm)

    prep_w = (cw, cb, wq_m, wk_m, wv_m, wgc, wgm, bg)
    tm_prep = _pick(seq, (512, MLSTM_CHUNK))
    q_m, k_m, v_m, xc, gc, gr = _prep(xm, xm_m, *prep_w, batch=batch, tm=tm_prep, chunk=MLSTM_CHUNK, meta=False)
    _, k_mm, v_mm, _, gc_m = _prep(xm, xm_m, *prep_w, batch=batch, tm=N_META, chunk=N_META, meta=True)
    hf, hb = _scan(q_m, k_m, v_m, gc, gr, k_mm, v_mm, gc_m, batch=batch, chunk=MLSTM_CHUNK)

    o = _attn(q_a, k_a, v_a, k_am, v_am, batch=batch, tq=_pick(seq, (512, 256)), tk=_pick(seq, (512, 256)))

    h2 = _outproj(o, hf, hb, z, xc, h1, row(attn_out_g), row(mlstm_gn_g), row(mlstm_skip),
                  wo[:da], wo[da:], row(ln2_g), row(ln2_b), tm=tm_proj)
    out = _ffn_ln(h2, *ffn2, row(ln3_g), row(ln3_b), tm=tm_ffn, tf=tf)
    return out.reshape(batch, seq, d)
```

```python
import functools

import jax
import jax.numpy as jnp
from jax import lax
from jax.experimental import pallas as pl
from jax.experimental.pallas import tpu as pltpu

F32 = jnp.float32
BF16 = jnp.bfloat16

N_META = 16
ATT_HEADS = 8
QK_NOPE = 128
QK_ROPE = 64
MLSTM_HEADS = 4
CONV_K = 5
ROPE_BASE = 10000.0
LN_EPS = 1e-5
RMS_EPS = 1e-6
DEPTH = 1
ALPHA = (2 * DEPTH) ** 0.25

LANES = 128
SUBLANES = 8
VMEM_LIMIT = 56 * 1024 * 1024

MLSTM_CHUNK = 256
GATE_GROUP = 8
LOWEST = float(jnp.finfo(jnp.float32).min)
NEG = -1e30
LOG2E = 1.4426950408889634


def _dot(a, b):
    return jnp.dot(a, b, preferred_element_type=F32)


def _dot_nt(a, b):
    return lax.dot_general(a, b, (((1,), (1,)), ((), ())), preferred_element_type=F32)


def _dot_tn(a, b):
    return lax.dot_general(a, b, (((0,), (0,)), ((), ())), preferred_element_type=F32)


def _layer_norm(y, g, b):
    mu = jnp.mean(y, axis=-1, keepdims=True)
    yc = y - mu
    var = jnp.mean(yc * yc, axis=-1, keepdims=True)
    return yc * lax.rsqrt(var + LN_EPS) * g + b


def _rms_norm(y, g):
    return y * lax.rsqrt(jnp.mean(y * y, axis=-1, keepdims=True) + RMS_EPS) * g


def _params(*sem):
    return pltpu.CompilerParams(dimension_semantics=sem, vmem_limit_bytes=VMEM_LIMIT)


def _ffn_ln_body(x_ref, wg_ref, wu_ref, wd_ref, g_ref, b_ref, o_ref, acc_ref, xb_ref, *, nf):
    f = pl.program_id(1)

    @pl.when(f == 0)
    def _():
        xb_ref[...] = x_ref[...].astype(BF16)
        acc_ref[...] = jnp.zeros_like(acc_ref)

    xb = xb_ref[...]
    gate = _dot(xb, wg_ref[...])
    up = _dot(xb, wu_ref[...])
    act = (gate * jax.nn.sigmoid(gate) * up).astype(BF16)
    acc_ref[...] += _dot(act, wd_ref[...])

    @pl.when(f == nf - 1)
    def _():
        y = ALPHA * x_ref[...] + 0.5 * acc_ref[...]
        o_ref[...] = _layer_norm(y, g_ref[...], b_ref[...])


def _ffn_ln(x, wg, wu, wd, g, b, *, tm, tf):
    n, d = x.shape
    dff = wg.shape[1]
    nf = dff // tf
    assert n % tm == 0 and dff % tf == 0
    return pl.pallas_call(
        functools.partial(_ffn_ln_body, nf=nf),
        grid=(n // tm, nf),
        in_specs=[
            pl.BlockSpec((tm, d), lambda i, f: (i, 0)),
            pl.BlockSpec((d, tf), lambda i, f: (0, f)),
            pl.BlockSpec((d, tf), lambda i, f: (0, f)),
            pl.BlockSpec((tf, d), lambda i, f: (f, 0)),
            pl.BlockSpec((1, d), lambda i, f: (0, 0)),
            pl.BlockSpec((1, d), lambda i, f: (0, 0)),
        ],
        out_specs=pl.BlockSpec((tm, d), lambda i, f: (i, 0)),
        out_shape=jax.ShapeDtypeStruct((n, d), F32),
        scratch_shapes=[pltpu.VMEM((tm, d), F32), pltpu.VMEM((tm, d), BF16)],
        compiler_params=_params("parallel", "arbitrary"),
        name="ffn_ln",
    )(x, wg, wu, wd, g, b)


def _proj_body(h_ref, win_ref, qg_ref, kvg_ref, wqn_ref, wqr_ref, wqs_ref, wkn_ref, wvt_ref,
               cos_ref, sin_ref, q_ref, k_ref, vt_ref, xm_ref, z_ref, *, ql, kvl, dm, scale):
    hb = h_ref[...].astype(BF16)
    u = _dot(hb, win_ref[...])
    o1 = ql
    o2 = o1 + kvl
    o3 = o2 + LANES
    o4 = o3 + LANES
    o5 = o4 + dm
    xm_ref[...] = u[:, o4:o5]
    z_ref[...] = u[:, o5:]
    c = cos_ref[...]
    s = sin_ref[...]
    qn = _rms_norm(u[:, :o1], qg_ref[...]).astype(BF16)
    qa = _dot(qn, wqn_ref[...])
    qr = _dot(qn, wqr_ref[...])
    qs = _dot(qn, wqs_ref[...])
    kvn = _rms_norm(u[:, o1:o2], kvg_ref[...]).astype(BF16)
    kn = _dot(kvn, wkn_ref[...])
    vt_ref[...] = _dot_nt(wvt_ref[...], kvn).astype(BF16)
    k_rope = (u[:, o2:o3] * c + u[:, o3:o4] * s).astype(BF16)
    for h in range(ATT_HEADS):
        sl = slice(h * LANES, (h + 1) * LANES)
        lo = 2 * h * LANES
        q_ref[:, lo:lo + LANES] = (qa[:, sl] * scale).astype(BF16)
        q_ref[:, lo + LANES:lo + 2 * LANES] = ((qr[:, sl] * c + qs[:, sl] * s) * scale).astype(BF16)
        k_ref[:, lo:lo + LANES] = kn[:, sl].astype(BF16)
        k_ref[:, lo + LANES:lo + 2 * LANES] = k_rope


def _proj(h, win, qg, kvg, wqn, wqr, wqs, wkn, wvt, ctab, stab, *, tm, dm):
    n, d = h.shape
    ql = qg.shape[1]
    kvl = kvg.shape[1]
    da = wvt.shape[0]
    npos = ctab.shape[0] // tm
    hq = ATT_HEADS * 2 * LANES
    const = lambda i: (0, 0)
    scale = float((QK_NOPE + QK_ROPE) ** -0.5 * LOG2E)
    return pl.pallas_call(
        functools.partial(_proj_body, ql=ql, kvl=kvl, dm=dm, scale=scale),
        grid=(n // tm,),
        in_specs=[
            pl.BlockSpec((tm, d), lambda i: (i, 0)),
            pl.BlockSpec(win.shape, const),
            pl.BlockSpec(qg.shape, const),
            pl.BlockSpec(kvg.shape, const),
            pl.BlockSpec(wqn.shape, const),
            pl.BlockSpec(wqr.shape, const),
            pl.BlockSpec(wqs.shape, const),
            pl.BlockSpec(wkn.shape, const),
            pl.BlockSpec(wvt.shape, const),
            pl.BlockSpec((tm, LANES), lambda i: (i % npos, 0)),
            pl.BlockSpec((tm, LANES), lambda i: (i % npos, 0)),
        ],
        out_specs=[
            pl.BlockSpec((tm, hq), lambda i: (i, 0)),
            pl.BlockSpec((tm, hq), lambda i: (i, 0)),
            pl.BlockSpec((da, tm), lambda i: (0, i)),
            pl.BlockSpec((tm, dm), lambda i: (i, 0)),
            pl.BlockSpec((tm, dm), lambda i: (i, 0)),
        ],
        out_shape=[
            jax.ShapeDtypeStruct((n, hq), BF16),
            jax.ShapeDtypeStruct((n, hq), BF16),
            jax.ShapeDtypeStruct((da, n), BF16),
            jax.ShapeDtypeStruct((n, dm), F32),
            jax.ShapeDtypeStruct((n, dm), F32),
        ],
        compiler_params=_params("parallel"),
        name="proj",
    )(h, win, qg, kvg, wqn, wqr, wqs, wkn, wvt, ctab, stab)


def _log_sigmoid(x):
    return jnp.minimum(x, 0.0) - jnp.log1p(jnp.exp(-jnp.abs(x)))


def _gate_scans(gi, lf, chunk):
    row = lax.broadcasted_iota(jnp.int32, (chunk, LANES), 0)
    lane = lax.broadcasted_iota(jnp.int32, (chunk, LANES), 1)
    j = lane % GATE_GROUP
    fwd = j < 3

    def scan(val, combine, identity):
        pre = val
        suf = val
        k = 1
        while k < chunk:
            pre = combine(pre, jnp.where(row >= k, pltpu.roll(pre, k, 0), identity))
            suf = combine(suf, jnp.where(row < chunk - k, pltpu.roll(suf, chunk - k, 0), identity))
            k *= 2
        return jnp.where(fwd, pre, suf)

    b = scan(lf, jnp.add, 0.0)
    r = gi - b
    cm = scan(r, jnp.maximum, LOWEST)
    return jnp.where(j % 3 == 0, b, jnp.where(j % 3 == 1, cm, r))


def _prep_body(prev_ref, x_ref, next_ref, mtail_ref, cw_ref, cb_ref, wq_ref, wk_ref, wv_ref,
               wgc_ref, wgm_ref, bg_ref, *rest, tm, chunk, dh, meta, with_rows):
    if with_rows:
        q_ref, k_ref, v_ref, xc_ref, gc_ref, gr_ref, xs_ref = rest
    else:
        q_ref, k_ref, v_ref, xc_ref, gc_ref, xs_ref = rest
        gr_ref = None
    i = pl.program_id(1)
    nt = pl.num_programs(1)
    x = x_ref[...]
    if meta:
        prev = jnp.zeros_like(prev_ref[...])
        nxt = next_ref[...]
    else:
        prev = jnp.where(i == 0, mtail_ref[...], prev_ref[...])
        nxt = jnp.where(i == nt - 1, 0.0, next_ref[...])
    xs_ref[0:SUBLANES, :] = prev
    xs_ref[SUBLANES:SUBLANES + tm, :] = x
    xs_ref[SUBLANES + tm:2 * SUBLANES + tm, :] = nxt
    acc = jnp.broadcast_to(cb_ref[...], x.shape)
    for t in range(CONV_K):
        off = SUBLANES - CONV_K // 2 + t
        acc = acc + cw_ref[t:t + 1, :] * xs_ref[off:off + tm, :]
    xc = acc * jax.nn.sigmoid(acc)
    xcb = xc.astype(BF16)
    xmb = x.astype(BF16)
    xc_ref[...] = xcb
    for h in range(MLSTM_HEADS):
        sl = slice(h * dh, (h + 1) * dh)
        q_ref[:, sl] = _dot(xcb[:, sl], wq_ref[h]).astype(BF16)
        k_ref[:, sl] = (_dot(xcb[:, sl], wk_ref[h]) * (dh ** -0.5)).astype(BF16)
        v_ref[:, sl] = _dot(xmb[:, sl], wv_ref[h]).astype(BF16)
    g = _dot(xcb, wgc_ref[...]) + _dot(xmb, wgm_ref[...]) + bg_ref[...]
    gi = g[:, :LANES]
    lf = _log_sigmoid(g[:, LANES:])
    for ci in range(tm // chunk):
        rows = slice(ci * chunk, (ci + 1) * chunk)
        out = _gate_scans(gi[rows], lf[rows], chunk)
        gc_ref[rows, :] = out
        if gr_ref is not None:
            gr_ref[ci] = out.T


def _prep(xm, xm_meta, cw, cb, wq, wk, wv, wgc, wgm, bg, *, batch, tm, chunk, meta):
    dm = xm.shape[1]
    dh = dm // MLSTM_HEADS
    seq = xm.shape[0] // batch
    if meta:
        nt = 1
        n_out = batch * N_META
        x_arr = xm_meta
        x_spec = pl.BlockSpec((tm, dm), lambda b, i: (0, 0))
        prev_spec = pl.BlockSpec((SUBLANES, dm), lambda b, i: (0, 0))
        next_spec = pl.BlockSpec((SUBLANES, dm), lambda b, i: (b * (seq // SUBLANES), 0))
    else:
        nt = seq // tm
        n_out = batch * seq
        x_arr = xm
        last_blk = batch * seq // SUBLANES - 1
        x_spec = pl.BlockSpec((tm, dm), lambda b, i: (b * nt + i, 0))
        prev_spec = pl.BlockSpec(
            (SUBLANES, dm), lambda b, i: (jnp.maximum((b * nt + i) * (tm // SUBLANES) - 1, 0), 0))
        next_spec = pl.BlockSpec(
            (SUBLANES, dm), lambda b, i: (jnp.minimum((b * nt + i + 1) * (tm // SUBLANES), last_blk), 0))
    const2 = lambda b, i: (0, 0)
    const3 = lambda b, i: (0, 0, 0)
    row_spec = lambda w: pl.BlockSpec((tm, w), lambda b, i: (b * nt + i, 0))
    out_specs = [row_spec(dm), row_spec(dm), row_spec(dm), row_spec(dm), row_spec(LANES)]
    out_shape = [jax.ShapeDtypeStruct((n_out, dm), BF16)] * 4 + [jax.ShapeDtypeStruct((n_out, LANES), F32)]
    with_rows = not meta
    if with_rows:
        ncb = tm // chunk
        out_specs.append(pl.BlockSpec((ncb, LANES, chunk), lambda b, i: (b * nt + i, 0, 0)))
        out_shape.append(jax.ShapeDtypeStruct((n_out // chunk, LANES, chunk), F32))
    return pl.pallas_call(
        functools.partial(_prep_body, tm=tm, chunk=chunk, dh=dh, meta=meta, with_rows=with_rows),
        grid=(batch, nt),
        in_specs=[
            prev_spec, x_spec, next_spec,
            pl.BlockSpec((SUBLANES, dm), lambda b, i: (1, 0)),
            pl.BlockSpec(cw.shape, const2),
            pl.BlockSpec(cb.shape, const2),
            pl.BlockSpec(wq.shape, const3),
            pl.BlockSpec(wk.shape, const3),
            pl.BlockSpec(wv.shape, const3),
            pl.BlockSpec(wgc.shape, const2),
            pl.BlockSpec(wgm.shape, const2),
            pl.BlockSpec(bg.shape, const2),
        ],
        out_specs=out_specs,
        out_shape=out_shape,
        scratch_shapes=[pltpu.VMEM((tm + 2 * SUBLANES, dm), F32)],
        compiler_params=_params("parallel", "parallel"),
        name="mlstm_prep_meta" if meta else "mlstm_prep",
    )(xm, x_arr, xm, xm_meta, cw, cb, wq, wk, wv, wgc, wgm, bg)


def _scan_chain(q, k, v, gc, r_row, c_ref, n_ref, m_ref, mask, edge, lane0):
    b_col = gc[:, lane0:lane0 + 1]
    cm_col = gc[:, lane0 + 1:lane0 + 2]
    r_col = gc[:, lane0 + 2:lane0 + 3]
    m_old = m_ref[...]
    c_old = c_ref[...]
    n_old = n_ref[...]

    s = _dot_nt(q, k)
    mt = jnp.maximum(m_old, cm_col)
    a = jnp.exp(jnp.where(mask, r_row - mt, NEG)) * s
    w_inter = jnp.exp(m_old - mt)
    num = _dot(a.astype(BF16), v) + w_inter * _dot(q, c_old.astype(BF16))
    qn = jnp.sum(q.astype(F32) * n_old, axis=1, keepdims=True)
    den = jnp.sum(a, axis=1, keepdims=True) + w_inter * qn
    scale = 1.0 / jnp.maximum(jnp.abs(den), jnp.exp(-(b_col + mt)))
    h = num * scale

    b_end = b_col[edge:edge + 1, :]
    g_max = b_end + cm_col[edge:edge + 1, :]
    m_new = jnp.maximum(b_end + m_old, g_max)
    decay = jnp.exp(b_end + m_old - m_new)
    kw = k.astype(F32) * jnp.exp(b_end + r_col - m_new)
    c_ref[...] = decay * c_old + _dot_tn(kw.astype(BF16), v)
    n_ref[...] = decay * n_old + jnp.sum(kw, axis=0, keepdims=True)
    m_ref[...] = m_new
    return h


def _scan_body(qf_ref, kf_ref, vf_ref, gcf_ref, grf_ref, qb_ref, kb_ref, vb_ref, gcb_ref, grb_ref,
               km_ref, vm_ref, gcm_ref, hf_ref, hb_ref, c_ref, n_ref, m_ref, *, chunk, dh):
    ci = pl.program_id(1)
    nh = MLSTM_HEADS

    @pl.when(ci == 0)
    def _():
        gcm = gcm_ref[0]
        km = km_ref[0]
        vm = vm_ref[0]
        for h in range(nh):
            sl = slice(h * dh, (h + 1) * dh)
            lane0 = h * GATE_GROUP
            b_end = gcm[N_META - 1:N_META, lane0:lane0 + 1]
            m_new = jnp.maximum(b_end, b_end + gcm[N_META - 1:N_META, lane0 + 1:lane0 + 2])
            kw = km[:, sl].astype(F32) * jnp.exp(b_end + gcm[:, lane0 + 2:lane0 + 3] - m_new)
            c_ref[h] = _dot_tn(kw.astype(BF16), vm[:, sl])
            n_ref[h] = jnp.sum(kw, axis=0, keepdims=True)
            m_ref[h] = m_new
            c_ref[nh + h] = jnp.zeros((dh, dh), F32)
            n_ref[nh + h] = jnp.zeros((1, dh), F32)
            m_ref[nh + h] = jnp.zeros((1, 1), F32)

    row = lax.broadcasted_iota(jnp.int32, (chunk, chunk), 0)
    col = lax.broadcasted_iota(jnp.int32, (chunk, chunk), 1)
    gcf = gcf_ref[...]
    gcb = gcb_ref[...]
    for h in range(nh):
        sl = slice(h * dh, (h + 1) * dh)
        lane0 = h * GATE_GROUP
        hf_ref[:, sl] = _scan_chain(
            qf_ref[:, sl], kf_ref[:, sl], vf_ref[:, sl], gcf, grf_ref[0, lane0 + 2:lane0 + 3, :],
            c_ref.at[h], n_ref.at[h], m_ref.at[h], col <= row, chunk - 1, lane0)
        hb_ref[:, sl] = _scan_chain(
            qb_ref[:, sl], kb_ref[:, sl], vb_ref[:, sl], gcb, grb_ref[0, lane0 + 5:lane0 + 6, :],
            c_ref.at[nh + h], n_ref.at[nh + h], m_ref.at[nh + h], col >= row, 0, lane0 + 3)


def _scan(q, k, v, gc, gr, km, vm, gcm, *, batch, chunk):
    n, dm = q.shape
    dh = dm // MLSTM_HEADS
    nc = n // batch // chunk
    fwd = lambda b, c: (b * nc + c, 0)
    bwd = lambda b, c: (b * nc + nc - 1 - c, 0)
    fwd3 = lambda b, c: (b * nc + c, 0, 0)
    bwd3 = lambda b, c: (b * nc + nc - 1 - c, 0, 0)
    per_b = lambda b, c: (b, 0, 0)

    def specs(idx, idx3):
        return [pl.BlockSpec((chunk, dm), idx)] * 3 + [
            pl.BlockSpec((chunk, LANES), idx), pl.BlockSpec((1, LANES, chunk), idx3)]

    return pl.pallas_call(
        functools.partial(_scan_body, chunk=chunk, dh=dh),
        grid=(batch, nc),
        in_specs=specs(fwd, fwd3) + specs(bwd, bwd3) + [
            pl.BlockSpec((1, N_META, dm), per_b),
            pl.BlockSpec((1, N_META, dm), per_b),
            pl.BlockSpec((1, N_META, LANES), per_b),
        ],
        out_specs=[pl.BlockSpec((chunk, dm), fwd), pl.BlockSpec((chunk, dm), bwd)],
        out_shape=[jax.ShapeDtypeStruct((n, dm), F32)] * 2,
        scratch_shapes=[
            pltpu.VMEM((2 * MLSTM_HEADS, dh, dh), F32),
            pltpu.VMEM((2 * MLSTM_HEADS, 1, dh), F32),
            pltpu.VMEM((2 * MLSTM_HEADS, 1, 1), F32),
        ],
        compiler_params=_params("parallel", "arbitrary"),
        name="mlstm_scan",
    )(q, k, v, gc, gr, q, k, v, gc, gr,
      km.reshape(batch, N_META, dm), vm.reshape(batch, N_META, dm), gcm.reshape(batch, N_META, LANES))


def _attn_body(q_ref, k_ref, vt_ref, km_ref, vmt_ref, o_ref, *, tk, nk):
    q = q_ref[...]
    s = _dot_nt(km_ref[...], q)
    m = jnp.max(s, axis=0, keepdims=True)
    p = jnp.exp2(s - m)
    l = jnp.sum(p, axis=0, keepdims=True)
    acc = _dot(vmt_ref[...], p.astype(BF16))
    s_next = _dot_nt(k_ref[0:tk, :], q)
    for j in range(nk):
        s = s_next
        if j + 1 < nk:
            s_next = _dot_nt(k_ref[(j + 1) * tk:(j + 2) * tk, :], q)
        m_new = jnp.maximum(m, jnp.max(s, axis=0, keepdims=True))
        alpha = jnp.exp2(m - m_new)
        p = jnp.exp2(s - m_new)
        l = alpha * l + jnp.sum(p, axis=0, keepdims=True)
        acc = alpha * acc + _dot(vt_ref[:, j * tk:(j + 1) * tk], p.astype(BF16))
        m = m_new
    o_ref[...] = (acc * (1.0 / l)).T


def _attn(q, k, vt, km, vmt, *, batch, tq, tk):
    n = q.shape[0]
    seq = n // batch
    nq = seq // tq
    dv = vt.shape[0] // ATT_HEADS
    dk = 2 * LANES
    return pl.pallas_call(
        functools.partial(_attn_body, tk=tk, nk=seq // tk),
        grid=(batch, ATT_HEADS, nq),
        in_specs=[
            pl.BlockSpec((tq, dk), lambda b, h, i: (b * nq + i, h)),
            pl.BlockSpec((seq, dk), lambda b, h, i: (b, h)),
            pl.BlockSpec((dv, seq), lambda b, h, i: (h, b)),
            pl.BlockSpec((N_META, dk), lambda b, h, i: (0, h)),
            pl.BlockSpec((dv, N_META), lambda b, h, i: (h, 0)),
        ],
        out_specs=pl.BlockSpec((tq, dv), lambda b, h, i: (b * nq + i, h)),
        out_shape=jax.ShapeDtypeStruct((n, ATT_HEADS * dv), F32),
        compiler_params=_params("parallel", "parallel", "arbitrary"),
        name="attn",
    )(q, k, vt, km, vmt)


def _outproj_body(o_ref, hf_ref, hb_ref, z_ref, xc_ref, h1_ref, og_ref, gn_ref, sk_ref,
                  wa_ref, wm_ref, g_ref, b_ref, out_ref, *, dh):
    y_att = _rms_norm(o_ref[...], og_ref[...]).astype(BF16)
    hs = (hf_ref[...] + hb_ref[...]) * jax.nn.sigmoid(z_ref[...])
    parts = []
    for h in range(MLSTM_HEADS):
        seg = hs[:, h * dh:(h + 1) * dh]
        mu = jnp.mean(seg, axis=-1, keepdims=True)
        sc = seg - mu
        var = jnp.mean(sc * sc, axis=-1, keepdims=True)
        parts.append(sc * lax.rsqrt(var + LN_EPS))
    hn = jnp.concatenate(parts, axis=-1)
    y_ml = (hn * gn_ref[...] + sk_ref[...] * xc_ref[...].astype(F32)).astype(BF16)
    y = _dot(y_att, wa_ref[...]) + _dot(y_ml, wm_ref[...])
    out_ref[...] = _layer_norm(ALPHA * h1_ref[...] + y, g_ref[...], b_ref[...])


def _outproj(o, hf, hb, z, xc, h1, og, gn, sk, wa, wm, g, b, *, tm):
    n, d = h1.shape
    da = o.shape[1]
    dm = hf.shape[1]
    const = lambda i: (0, 0)
    row = lambda w: pl.BlockSpec((tm, w), lambda i: (i, 0))
    return pl.pallas_call(
        functools.partial(_outproj_body, dh=dm // MLSTM_HEADS),
        grid=(n // tm,),
        in_specs=[row(da), row(dm), row(dm), row(dm), row(dm), row(d),
                  pl.BlockSpec(og.shape, const), pl.BlockSpec(gn.shape, const), pl.BlockSpec(sk.shape, const),
                  pl.BlockSpec(wa.shape, const), pl.BlockSpec(wm.shape, const),
                  pl.BlockSpec(g.shape, const), pl.BlockSpec(b.shape, const)],
        out_specs=row(d),
        out_shape=jax.ShapeDtypeStruct((n, d), F32),
        compiler_params=_params("parallel"),
        name="outproj",
    )(o, hf, hb, z, xc, h1, og, gn, sk, wa, wm, g, b)


def _pad_lanes(w):
    return jnp.pad(w, [(0, 0)] * (w.ndim - 1) + [(0, LANES - w.shape[-1])])


def _pick(n, pref):
    for t in pref:
        if n % t == 0:
            return t
    raise ValueError(f"no tile for {n}")


def kernel(x, meta_tokens, ffn1_w_gate, ffn1_w_up, ffn1_w_down, ln1_g, ln1_b, w_in, mla_q_norm_g, mla_w_uq, mla_kv_norm_g, mla_w_ukv, attn_out_g, mlstm_conv_w, mlstm_conv_b, mlstm_w_q, mlstm_w_k, mlstm_w_v, mlstm_w_gates, mlstm_b_gates, mlstm_gn_g, mlstm_skip, w_out, ln2_g, ln2_b, ffn2_w_gate, ffn2_w_up, ffn2_w_down, ln3_g, ln3_b):
    batch, seq, d = x.shape
    assert ffn1_w_gate.shape[0] == DEPTH and meta_tokens.shape[0] == N_META
    n = batch * seq
    ql = mla_q_norm_g.shape[-1]
    kvl = mla_kv_norm_g.shape[-1]
    da = attn_out_g.shape[-1]
    dm = mlstm_gn_g.shape[-1]
    dh = dm // MLSTM_HEADS
    dv = da // ATT_HEADS
    nh = MLSTM_HEADS
    half = QK_ROPE // 2
    assert dv == LANES and dh % LANES == 0 and ql % LANES == 0 and kvl % LANES == 0
    assert seq % MLSTM_CHUNK == 0

    wi = w_in[0]
    o1, o2, o3, o4 = ql, ql + kvl, ql + kvl + QK_ROPE, ql + kvl + QK_ROPE + dm
    swap = jnp.concatenate([jnp.arange(half, QK_ROPE), jnp.arange(half)])
    w_kr = wi[:, o2:o3]
    win = jnp.concatenate(
        [wi[:, :o2], _pad_lanes(w_kr), _pad_lanes(w_kr[:, swap]), wi[:, o3:]], axis=1).astype(BF16)
    wuq = mla_w_uq[0].reshape(ql, ATT_HEADS, QK_NOPE + QK_ROPE)
    wqn = wuq[:, :, :QK_NOPE].reshape(ql, -1).astype(BF16)
    wqr = _pad_lanes(wuq[:, :, QK_NOPE:]).reshape(ql, -1).astype(BF16)
    wqs = _pad_lanes(wuq[:, :, QK_NOPE:][:, :, swap]).reshape(ql, -1).astype(BF16)
    wukv = mla_w_ukv[0].reshape(kvl, ATT_HEADS, QK_NOPE + dv)
    wkn = wukv[:, :, :QK_NOPE].reshape(kvl, -1).astype(BF16)
    wvt = wukv[:, :, QK_NOPE:].reshape(kvl, -1).T.astype(BF16)

    wg = mlstm_w_gates[0].reshape(2 * dm, 4, nh)
    bgr = mlstm_b_gates[0].reshape(4, nh)

    def gate_tile(w4, kind_f, kind_b):
        zeros = jnp.zeros_like(w4[..., 0, :])
        cols = [w4[..., kind_f, :]] * 3 + [w4[..., kind_b, :]] * 3 + [zeros] * (GATE_GROUP - 6)
        tile = jnp.stack(cols, axis=-1).reshape(*w4.shape[:-2], nh * GATE_GROUP)
        return _pad_lanes(tile)

    wgate = jnp.concatenate([gate_tile(wg, 0, 2), gate_tile(wg, 1, 3)], axis=-1)
    wgc = wgate[:dm].astype(BF16)
    wgm = wgate[dm:].astype(BF16)
    bg = jnp.concatenate([gate_tile(bgr, 0, 2), gate_tile(bgr, 1, 3)], axis=-1)[None, :]
    cw = jnp.pad(mlstm_conv_w[0], ((0, SUBLANES - CONV_K), (0, 0)))
    cb = mlstm_conv_b[0][None, :]
    wq_m, wk_m, wv_m = (w[0].astype(BF16) for w in (mlstm_w_q, mlstm_w_k, mlstm_w_v))
    wo = w_out[0].astype(BF16)
    row = lambda p: p[0][None, :]

    pos = jnp.arange(N_META + seq, dtype=F32)
    inv = ROPE_BASE ** (-jnp.arange(0, QK_ROPE, 2, dtype=F32) / QK_ROPE)
    ang = pos[:, None] * inv[None, :]
    ctab = _pad_lanes(jnp.concatenate([jnp.cos(ang), jnp.cos(ang)], axis=-1))
    stab = _pad_lanes(jnp.concatenate([-jnp.sin(ang), jnp.sin(ang)], axis=-1))

    ffn1 = [w[0].astype(BF16) for w in (ffn1_w_gate, ffn1_w_up, ffn1_w_down)]
    ffn2 = [w[0].astype(BF16) for w in (ffn2_w_gate, ffn2_w_up, ffn2_w_down)]
    tf = _pick(ffn1[0].shape[1], (512, 256, 128))
    tm_ffn = _pick(n, (512, 256))
    tm_proj = _pick(seq, (256,))

    xr = x.reshape(n, d)
    h1 = _ffn_ln(xr, *ffn1, row(ln1_g), row(ln1_b), tm=tm_ffn, tf=tf)
    h1m = _ffn_ln(meta_tokens.astype(x.dtype), *ffn1, row(ln1_g), row(ln1_b), tm=N_META, tf=tf)
    proj_w = (win, row(mla_q_norm_g), row(mla_kv_norm_g), wqn, wqr, wqs, wkn, wvt)
    q_a, k_a, vt_a, xm, z = _proj(h1, *proj_w, ctab[N_META:], stab[N_META:], tm=tm_proj, dm=dm)
    _, k_am, vt_am, xm_m, _ = _proj(h1m, *proj_w, ctab[:N_META], stab[:N_META], tm=N_META, dm=dm)

    prep_w = (cw, cb, wq_m, wk_m, wv_m, wgc, wgm, bg)
    tm_prep = _pick(seq, (512, MLSTM_CHUNK))
    q_m, k_m, v_m, xc, gc, gr = _prep(xm, xm_m, *prep_w, batch=batch, tm=tm_prep, chunk=MLSTM_CHUNK, meta=False)
    _, k_mm, v_mm, _, gc_m = _prep(xm, xm_m, *prep_w, batch=batch, tm=N_META, chunk=N_META, meta=True)
    hf, hb = _scan(q_m, k_m, v_m, gc, gr, k_mm, v_mm, gc_m, batch=batch, chunk=MLSTM_CHUNK)

    o = _attn(q_a, k_a, vt_a, k_am, vt_am, batch=batch, tq=_pick(seq, (512, 256)), tk=_pick(seq, (512, 256)))

    h2 = _outproj(o, hf, hb, z, xc, h1, row(attn_out_g), row(mlstm_gn_g), row(mlstm_skip),
                  wo[:da], wo[da:], row(ln2_g), row(ln2_b), tm=tm_proj)
    out = _ffn_ln(h2, *ffn2, row(ln3_g), row(ln3_b), tm=tm_ffn, tf=tf)
    return out.reshape(batch, seq, d)
```

```python
import functools

import jax
import jax.numpy as jnp
from jax import lax
from jax.experimental import pallas as pl
from jax.experimental.pallas import tpu as pltpu

F32 = jnp.float32
BF16 = jnp.bfloat16

N_META = 16
ATT_HEADS = 8
QK_NOPE = 128
QK_ROPE = 64
MLSTM_HEADS = 4
CONV_K = 5
ROPE_BASE = 10000.0
LN_EPS = 1e-5
RMS_EPS = 1e-6
DEPTH = 1
ALPHA = (2 * DEPTH) ** 0.25

LANES = 128
SUBLANES = 8
VMEM_LIMIT = 56 * 1024 * 1024

BF16_ROWS = 16
V_PAD = BF16_ROWS
MLSTM_CHUNK = 256
GATE_GROUP = 8
LOWEST = float(jnp.finfo(jnp.float32).min)
NEG = -1e30
LOG2E = 1.4426950408889634


def _dot(a, b):
    return jnp.dot(a, b, preferred_element_type=F32)


def _dot_nt(a, b):
    return lax.dot_general(a, b, (((1,), (1,)), ((), ())), preferred_element_type=F32)


def _dot_tn(a, b):
    return lax.dot_general(a, b, (((0,), (0,)), ((), ())), preferred_element_type=F32)


def _layer_norm(y, g, b):
    mu = jnp.mean(y, axis=-1, keepdims=True)
    yc = y - mu
    var = jnp.mean(yc * yc, axis=-1, keepdims=True)
    return yc * lax.rsqrt(var + LN_EPS) * g + b


def _rms_norm(y, g):
    return y * lax.rsqrt(jnp.mean(y * y, axis=-1, keepdims=True) + RMS_EPS) * g


def _params(*sem):
    return pltpu.CompilerParams(dimension_semantics=sem, vmem_limit_bytes=VMEM_LIMIT)


def _ffn_ln_body(x_ref, wg_ref, wu_ref, wd_ref, g_ref, b_ref, o_ref, acc_ref, xb_ref, *, nf):
    f = pl.program_id(1)

    @pl.when(f == 0)
    def _():
        xb_ref[...] = x_ref[...].astype(BF16)
        acc_ref[...] = jnp.zeros_like(acc_ref)

    xb = xb_ref[...]
    gate = _dot(xb, wg_ref[...])
    up = _dot(xb, wu_ref[...])
    act = (gate * jax.nn.sigmoid(gate) * up).astype(BF16)
    acc_ref[...] += _dot(act, wd_ref[...])

    @pl.when(f == nf - 1)
    def _():
        y = ALPHA * x_ref[...] + 0.5 * acc_ref[...]
        o_ref[...] = _layer_norm(y, g_ref[...], b_ref[...])


def _ffn_ln(x, wg, wu, wd, g, b, *, tm, tf):
    n, d = x.shape
    dff = wg.shape[1]
    nf = dff // tf
    assert n % tm == 0 and dff % tf == 0
    return pl.pallas_call(
        functools.partial(_ffn_ln_body, nf=nf),
        grid=(n // tm, nf),
        in_specs=[
            pl.BlockSpec((tm, d), lambda i, f: (i, 0)),
            pl.BlockSpec((d, tf), lambda i, f: (0, f)),
            pl.BlockSpec((d, tf), lambda i, f: (0, f)),
            pl.BlockSpec((tf, d), lambda i, f: (f, 0)),
            pl.BlockSpec((1, d), lambda i, f: (0, 0)),
            pl.BlockSpec((1, d), lambda i, f: (0, 0)),
        ],
        out_specs=pl.BlockSpec((tm, d), lambda i, f: (i, 0)),
        out_shape=jax.ShapeDtypeStruct((n, d), F32),
        scratch_shapes=[pltpu.VMEM((tm, d), F32), pltpu.VMEM((tm, d), BF16)],
        compiler_params=_params("parallel", "arbitrary"),
        name="ffn_ln",
    )(x, wg, wu, wd, g, b)


def _proj_body(h_ref, win_ref, qg_ref, kvg_ref, wqn_ref, wqr_ref, wqs_ref, wkn_ref, wvt_ref,
               cos_ref, sin_ref, q_ref, k_ref, vt_ref, xm_ref, z_ref, *, ql, kvl, dm, scale):
    hb = h_ref[...].astype(BF16)
    u = _dot(hb, win_ref[...])
    o1 = ql
    o2 = o1 + kvl
    o3 = o2 + LANES
    o4 = o3 + LANES
    o5 = o4 + dm
    xm_ref[...] = u[:, o4:o5]
    z_ref[...] = u[:, o5:]
    c = cos_ref[...]
    s = sin_ref[...]
    qn = _rms_norm(u[:, :o1], qg_ref[...]).astype(BF16)
    qa = _dot(qn, wqn_ref[...])
    qr = _dot(qn, wqr_ref[...])
    qs = _dot(qn, wqs_ref[...])
    kvn = _rms_norm(u[:, o1:o2], kvg_ref[...]).astype(BF16)
    kn = _dot(kvn, wkn_ref[...])
    vt = _dot_nt(wvt_ref[...], kvn).astype(BF16)
    dv = vt.shape[0] // ATT_HEADS
    ones_rows = (lax.broadcasted_iota(jnp.int32, (V_PAD, vt.shape[1]), 0) == 0).astype(BF16)
    for h in range(ATT_HEADS):
        lo = h * (dv + V_PAD)
        vt_ref[lo:lo + dv, :] = vt[h * dv:(h + 1) * dv]
        vt_ref[lo + dv:lo + dv + V_PAD, :] = ones_rows
    k_rope = (u[:, o2:o3] * c + u[:, o3:o4] * s).astype(BF16)
    for h in range(ATT_HEADS):
        sl = slice(h * LANES, (h + 1) * LANES)
        lo = 2 * h * LANES
        q_ref[:, lo:lo + LANES] = (qa[:, sl] * scale).astype(BF16)
        q_ref[:, lo + LANES:lo + 2 * LANES] = ((qr[:, sl] * c + qs[:, sl] * s) * scale).astype(BF16)
        k_ref[:, lo:lo + LANES] = kn[:, sl].astype(BF16)
        k_ref[:, lo + LANES:lo + 2 * LANES] = k_rope


def _proj(h, win, qg, kvg, wqn, wqr, wqs, wkn, wvt, ctab, stab, *, tm, dm):
    n, d = h.shape
    ql = qg.shape[1]
    kvl = kvg.shape[1]
    da = wvt.shape[0] + ATT_HEADS * V_PAD
    npos = ctab.shape[0] // tm
    hq = ATT_HEADS * 2 * LANES
    const = lambda i: (0, 0)
    scale = float((QK_NOPE + QK_ROPE) ** -0.5 * LOG2E)
    return pl.pallas_call(
        functools.partial(_proj_body, ql=ql, kvl=kvl, dm=dm, scale=scale),
        grid=(n // tm,),
        in_specs=[
            pl.BlockSpec((tm, d), lambda i: (i, 0)),
            pl.BlockSpec(win.shape, const),
            pl.BlockSpec(qg.shape, const),
            pl.BlockSpec(kvg.shape, const),
            pl.BlockSpec(wqn.shape, const),
            pl.BlockSpec(wqr.shape, const),
            pl.BlockSpec(wqs.shape, const),
            pl.BlockSpec(wkn.shape, const),
            pl.BlockSpec(wvt.shape, const),
            pl.BlockSpec((tm, LANES), lambda i: (i % npos, 0)),
            pl.BlockSpec((tm, LANES), lambda i: (i % npos, 0)),
        ],
        out_specs=[
            pl.BlockSpec((tm, hq), lambda i: (i, 0)),
            pl.BlockSpec((tm, hq), lambda i: (i, 0)),
            pl.BlockSpec((da, tm), lambda i: (0, i)),
            pl.BlockSpec((tm, dm), lambda i: (i, 0)),
            pl.BlockSpec((tm, dm), lambda i: (i, 0)),
        ],
        out_shape=[
            jax.ShapeDtypeStruct((n, hq), BF16),
            jax.ShapeDtypeStruct((n, hq), BF16),
            jax.ShapeDtypeStruct((da, n), BF16),
            jax.ShapeDtypeStruct((n, dm), F32),
            jax.ShapeDtypeStruct((n, dm), F32),
        ],
        compiler_params=_params("parallel"),
        name="proj",
    )(h, win, qg, kvg, wqn, wqr, wqs, wkn, wvt, ctab, stab)


def _log_sigmoid(x):
    return jnp.minimum(x, 0.0) - jnp.log1p(jnp.exp(-jnp.abs(x)))


def _gate_scans(gi, lf, chunk):
    row = lax.broadcasted_iota(jnp.int32, (chunk, LANES), 0)
    lane = lax.broadcasted_iota(jnp.int32, (chunk, LANES), 1)
    j = lane % GATE_GROUP
    fwd = j < 3

    def scan(val, combine, identity):
        pre = val
        suf = val
        k = 1
        while k < chunk:
            pre = combine(pre, jnp.where(row >= k, pltpu.roll(pre, k, 0), identity))
            suf = combine(suf, jnp.where(row < chunk - k, pltpu.roll(suf, chunk - k, 0), identity))
            k *= 2
        return jnp.where(fwd, pre, suf)

    b = scan(lf, jnp.add, 0.0)
    r = gi - b
    cm = scan(r, jnp.maximum, LOWEST)
    return jnp.where(j % 3 == 0, b, jnp.where(j % 3 == 1, cm, r))


def _prep_body(prev_ref, x_ref, next_ref, mtail_ref, cw_ref, cb_ref, wk_ref, wv_ref, wqt_ref, wkt_ref,
               wvt_ref, wgc_ref, wgm_ref, bg_ref, *rest, tm, chunk, dh, meta):
    if meta:
        k_ref, v_ref, gc_ref, xs_ref = rest
    else:
        k_ref, qt_ref, kt_ref, vt_ref, xc_ref, gc_ref, gr_ref, xs_ref = rest
    i = pl.program_id(1)
    nt = pl.num_programs(1)
    x = x_ref[...]
    if meta:
        prev = jnp.zeros_like(prev_ref[...])
        nxt = next_ref[...]
    else:
        prev = jnp.where(i == 0, mtail_ref[...], prev_ref[...])
        nxt = jnp.where(i == nt - 1, 0.0, next_ref[...])
    xs_ref[0:SUBLANES, :] = prev
    xs_ref[SUBLANES:SUBLANES + tm, :] = x
    xs_ref[SUBLANES + tm:2 * SUBLANES + tm, :] = nxt
    acc = jnp.broadcast_to(cb_ref[...], x.shape)
    for t in range(CONV_K):
        off = SUBLANES - CONV_K // 2 + t
        acc = acc + cw_ref[t:t + 1, :] * xs_ref[off:off + tm, :]
    xc = acc * jax.nn.sigmoid(acc)
    xcb = xc.astype(BF16)
    xmb = x.astype(BF16)
    k_scale = dh ** -0.5
    if not meta:
        xc_ref[...] = xcb
        ones_rows = (lax.broadcasted_iota(jnp.int32, (V_PAD, tm), 0) == 0).astype(BF16)
    for h in range(MLSTM_HEADS):
        sl = slice(h * dh, (h + 1) * dh)
        k_ref[:, sl] = (_dot(xcb[:, sl], wk_ref[h]) * k_scale).astype(BF16)
        if meta:
            v_ref[:, sl] = _dot(xmb[:, sl], wv_ref[h]).astype(BF16)
        else:
            qt_ref[sl, :] = _dot_nt(wqt_ref[h], xcb[:, sl]).astype(BF16)
            kt_ref[sl, :] = (_dot_nt(wkt_ref[h], xcb[:, sl]) * k_scale).astype(BF16)
            lo = h * (dh + V_PAD)
            vt_ref[lo:lo + dh, :] = _dot_nt(wvt_ref[h], xmb[:, sl]).astype(BF16)
            vt_ref[lo + dh:lo + dh + V_PAD, :] = ones_rows
    g = _dot(xcb, wgc_ref[...]) + _dot(xmb, wgm_ref[...]) + bg_ref[...]
    gi = g[:, :LANES]
    lf = _log_sigmoid(g[:, LANES:])
    for ci in range(tm // chunk):
        rows = slice(ci * chunk, (ci + 1) * chunk)
        out = _gate_scans(gi[rows], lf[rows], chunk)
        gc_ref[rows, :] = out
        if not meta:
            gr_ref[ci] = out.T


def _prep(xm, xm_meta, cw, cb, wk, wv, wqt, wkt, wvt, wgc, wgm, bg, *, batch, tm, chunk, meta):
    dm = xm.shape[1]
    dh = dm // MLSTM_HEADS
    seq = xm.shape[0] // batch
    if meta:
        nt = 1
        n_out = batch * N_META
        x_arr = xm_meta
        x_spec = pl.BlockSpec((tm, dm), lambda b, i: (0, 0))
        prev_spec = pl.BlockSpec((SUBLANES, dm), lambda b, i: (0, 0))
        next_spec = pl.BlockSpec((SUBLANES, dm), lambda b, i: (b * (seq // SUBLANES), 0))
    else:
        nt = seq // tm
        n_out = batch * seq
        x_arr = xm
        last_blk = batch * seq // SUBLANES - 1
        x_spec = pl.BlockSpec((tm, dm), lambda b, i: (b * nt + i, 0))
        prev_spec = pl.BlockSpec(
            (SUBLANES, dm), lambda b, i: (jnp.maximum((b * nt + i) * (tm // SUBLANES) - 1, 0), 0))
        next_spec = pl.BlockSpec(
            (SUBLANES, dm), lambda b, i: (jnp.minimum((b * nt + i + 1) * (tm // SUBLANES), last_blk), 0))
    const2 = lambda b, i: (0, 0)
    const3 = lambda b, i: (0, 0, 0)
    row_spec = lambda w: pl.BlockSpec((tm, w), lambda b, i: (b * nt + i, 0))
    col_spec = lambda r: pl.BlockSpec((r, tm), lambda b, i: (0, b * nt + i))
    bf = lambda *shape: jax.ShapeDtypeStruct(shape, BF16)
    gc_shape = jax.ShapeDtypeStruct((n_out, LANES), F32)
    if meta:
        out_specs = [row_spec(dm), row_spec(dm), row_spec(LANES)]
        out_shape = [bf(n_out, dm), bf(n_out, dm), gc_shape]
    else:
        dva = dm + MLSTM_HEADS * V_PAD
        out_specs = [row_spec(dm), col_spec(dm), col_spec(dm), col_spec(dva), row_spec(dm), row_spec(LANES),
                     pl.BlockSpec((tm // chunk, LANES, chunk), lambda b, i: (b * nt + i, 0, 0))]
        out_shape = [bf(n_out, dm), bf(dm, n_out), bf(dm, n_out), bf(dva, n_out), bf(n_out, dm), gc_shape,
                     jax.ShapeDtypeStruct((n_out // chunk, LANES, chunk), F32)]
    return pl.pallas_call(
        functools.partial(_prep_body, tm=tm, chunk=chunk, dh=dh, meta=meta),
        grid=(batch, nt),
        in_specs=[
            prev_spec, x_spec, next_spec,
            pl.BlockSpec((SUBLANES, dm), lambda b, i: (1, 0)),
            pl.BlockSpec(cw.shape, const2),
            pl.BlockSpec(cb.shape, const2),
            pl.BlockSpec(wk.shape, const3),
            pl.BlockSpec(wv.shape, const3),
            pl.BlockSpec(wqt.shape, const3),
            pl.BlockSpec(wkt.shape, const3),
            pl.BlockSpec(wvt.shape, const3),
            pl.BlockSpec(wgc.shape, const2),
            pl.BlockSpec(wgm.shape, const2),
            pl.BlockSpec(bg.shape, const2),
        ],
        out_specs=out_specs,
        out_shape=out_shape,
        scratch_shapes=[pltpu.VMEM((tm + 2 * SUBLANES, dm), F32)],
        compiler_params=_params("parallel", "parallel"),
        name="mlstm_prep_meta" if meta else "mlstm_prep",
    )(xm, x_arr, xm, xm_meta, cw, cb, wk, wv, wqt, wkt, wvt, wgc, wgm, bg)


def _scan_chain(k, qt, kt, vta, r_col, rows, c_ref, m_ref, mask, edge, dh):
    b_row, cm_row, r_row = rows
    m_old = m_ref[...]
    ct = c_ref[...]

    mt = jnp.maximum(m_old, cm_row)
    at = jnp.exp(jnp.where(mask, r_col - mt, NEG)) * _dot(k, qt)
    inter = _dot(ct.astype(BF16), qt)
    w = jnp.exp(m_old - mt)
    den = jnp.sum(at, axis=0, keepdims=True) + w * inter[dh:dh + 1, :]
    scale = 1.0 / jnp.maximum(jnp.abs(den), jnp.exp(-(b_row + mt)))
    ht = _dot(vta[:dh], (at * scale).astype(BF16)) + inter[:dh] * (w * scale)

    b_end = b_row[:, edge:edge + 1]
    m_new = jnp.maximum(b_end + m_old, b_end + cm_row[:, edge:edge + 1])
    decay = jnp.exp(b_end + m_old - m_new)
    kwt = (kt.astype(F32) * jnp.exp(b_end + r_row - m_new)).astype(BF16)
    c_ref[...] = decay * ct + _dot_nt(vta, kwt)
    m_ref[...] = m_new
    return ht


def _scan_body(kf_ref, qtf_ref, ktf_ref, vtf_ref, gcf_ref, grf_ref,
               kb_ref, qtb_ref, ktb_ref, vtb_ref, gcb_ref, grb_ref,
               km_ref, vm_ref, gcm_ref, hf_ref, hb_ref, c_ref, m_ref, *, chunk, dh):
    ci = pl.program_id(1)
    nh = MLSTM_HEADS
    dva = dh + V_PAD

    @pl.when(ci == 0)
    def _():
        gcm = gcm_ref[0]
        km = km_ref[0]
        vm = vm_ref[0]
        first = lax.broadcasted_iota(jnp.int32, (V_PAD, dh), 0) == 0
        for h in range(nh):
            sl = slice(h * dh, (h + 1) * dh)
            lane0 = h * GATE_GROUP
            b_end = gcm[N_META - 1:N_META, lane0:lane0 + 1]
            m_new = jnp.maximum(b_end, b_end + gcm[N_META - 1:N_META, lane0 + 1:lane0 + 2])
            kw = km[:, sl].astype(F32) * jnp.exp(b_end + gcm[:, lane0 + 2:lane0 + 3] - m_new)
            c_ref[h, 0:dh, :] = _dot_tn(vm[:, sl], kw.astype(BF16))
            c_ref[h, dh:dva, :] = jnp.where(first, jnp.sum(kw, axis=0, keepdims=True), 0.0)
            m_ref[h] = m_new
            c_ref[nh + h] = jnp.zeros((dva, dh), F32)
            m_ref[nh + h] = jnp.zeros((1, 1), F32)

    srow = lax.broadcasted_iota(jnp.int32, (chunk, chunk), 0)
    tcol = lax.broadcasted_iota(jnp.int32, (chunk, chunk), 1)
    for h in range(nh):
        sl = slice(h * dh, (h + 1) * dh)
        sla = slice(h * dva, (h + 1) * dva)
        lane0 = h * GATE_GROUP
        rows_f = tuple(grf_ref[0, lane0 + j:lane0 + j + 1, :] for j in range(3))
        rows_b = tuple(grb_ref[0, lane0 + j:lane0 + j + 1, :] for j in range(3, 6))
        hf_ref[sl, :] = _scan_chain(
            kf_ref[:, sl], qtf_ref[sl, :], ktf_ref[sl, :], vtf_ref[sla, :], gcf_ref[:, lane0 + 2:lane0 + 3],
            rows_f, c_ref.at[h], m_ref.at[h], srow <= tcol, chunk - 1, dh).astype(hf_ref.dtype)
        hb_ref[sl, :] = _scan_chain(
            kb_ref[:, sl], qtb_ref[sl, :], ktb_ref[sl, :], vtb_ref[sla, :], gcb_ref[:, lane0 + 5:lane0 + 6],
            rows_b, c_ref.at[nh + h], m_ref.at[nh + h], srow >= tcol, 0, dh).astype(hb_ref.dtype)


def _scan(k, qt, kt, vta, gc, gr, km, vm, gcm, *, batch, chunk):
    n, dm = k.shape
    dh = dm // MLSTM_HEADS
    dva = vta.shape[0]
    nc = n // batch // chunk
    fwd = lambda b, c: (b * nc + c, 0)
    bwd = lambda b, c: (b * nc + nc - 1 - c, 0)
    fwd_t = lambda b, c: (0, b * nc + c)
    bwd_t = lambda b, c: (0, b * nc + nc - 1 - c)
    fwd3 = lambda b, c: (b * nc + c, 0, 0)
    bwd3 = lambda b, c: (b * nc + nc - 1 - c, 0, 0)
    per_b = lambda b, c: (b, 0, 0)

    def specs(idx, idx_t, idx3):
        return [pl.BlockSpec((chunk, dm), idx), pl.BlockSpec((dm, chunk), idx_t), pl.BlockSpec((dm, chunk), idx_t),
                pl.BlockSpec((dva, chunk), idx_t), pl.BlockSpec((chunk, LANES), idx),
                pl.BlockSpec((1, LANES, chunk), idx3)]

    return pl.pallas_call(
        functools.partial(_scan_body, chunk=chunk, dh=dh),
        grid=(batch, nc),
        in_specs=specs(fwd, fwd_t, fwd3) + specs(bwd, bwd_t, bwd3) + [
            pl.BlockSpec((1, N_META, dm), per_b),
            pl.BlockSpec((1, N_META, dm), per_b),
            pl.BlockSpec((1, N_META, LANES), per_b),
        ],
        out_specs=[pl.BlockSpec((dm, chunk), fwd_t), pl.BlockSpec((dm, chunk), bwd_t)],
        out_shape=[jax.ShapeDtypeStruct((dm, n), BF16)] * 2,
        scratch_shapes=[
            pltpu.VMEM((2 * MLSTM_HEADS, dh + V_PAD, dh), F32),
            pltpu.VMEM((2 * MLSTM_HEADS, 1, 1), F32),
        ],
        compiler_params=_params("parallel", "arbitrary"),
        name="mlstm_scan",
    )(k, qt, kt, vta, gc, gr, k, qt, kt, vta, gc, gr,
      km.reshape(batch, N_META, dm), vm.reshape(batch, N_META, dm), gcm.reshape(batch, N_META, LANES))


def _attn_body(q_ref, k_ref, vt_ref, km_ref, vmt_ref, o_ref, *, tk, nk):
    q = q_ref[...]
    dv = o_ref.shape[1]

    def scores(j):
        s = _dot_nt(k_ref[j * tk:(j + 1) * tk, :], q)
        return s, jnp.max(s, axis=0, keepdims=True)

    s_meta = _dot_nt(km_ref[...], q)
    ahead = scores(0)
    m = jnp.maximum(jnp.max(s_meta, axis=0, keepdims=True), ahead[1])
    acc = _dot(vmt_ref[...], jnp.exp2(s_meta - m).astype(BF16))
    for j in range(nk):
        s, s_max = ahead
        if j + 1 < nk:
            ahead = scores(j + 1)
        m_new = jnp.maximum(m, s_max)
        p = jnp.exp2(s - m_new).astype(BF16)
        acc = jnp.exp2(m - m_new) * acc + _dot(vt_ref[:, j * tk:(j + 1) * tk], p)
        m = m_new
    o_ref[...] = (acc[:dv] * (1.0 / acc[dv:dv + 1])).T


def _attn(q, k, vt, km, vmt, *, batch, tq, tk):
    n = q.shape[0]
    seq = n // batch
    nq = seq // tq
    dva = vt.shape[0] // ATT_HEADS
    dv = dva - V_PAD
    dk = 2 * LANES
    return pl.pallas_call(
        functools.partial(_attn_body, tk=tk, nk=seq // tk),
        grid=(batch, ATT_HEADS, nq),
        in_specs=[
            pl.BlockSpec((tq, dk), lambda b, h, i: (b * nq + i, h)),
            pl.BlockSpec((seq, dk), lambda b, h, i: (b, h)),
            pl.BlockSpec((dva, seq), lambda b, h, i: (h, b)),
            pl.BlockSpec((N_META, dk), lambda b, h, i: (0, h)),
            pl.BlockSpec((dva, N_META), lambda b, h, i: (h, 0)),
        ],
        out_specs=pl.BlockSpec((tq, dv), lambda b, h, i: (b * nq + i, h)),
        out_shape=jax.ShapeDtypeStruct((n, ATT_HEADS * dv), F32),
        compiler_params=_params("parallel", "parallel", "arbitrary"),
        name="attn",
    )(q, k, vt, km, vmt)


def _outproj_body(o_ref, hf_ref, hb_ref, z_ref, xc_ref, h1_ref, og_ref, gn_ref, sk_ref,
                  wa_ref, wm_ref, g_ref, b_ref, out_ref, *, dh):
    y_att = _rms_norm(o_ref[...], og_ref[...]).astype(BF16)
    h_sum = (hf_ref[...].astype(F32) + hb_ref[...].astype(F32)).T
    hs = h_sum * jax.nn.sigmoid(z_ref[...])
    parts = []
    for h in range(MLSTM_HEADS):
        seg = hs[:, h * dh:(h + 1) * dh]
        mu = jnp.mean(seg, axis=-1, keepdims=True)
        sc = seg - mu
        var = jnp.mean(sc * sc, axis=-1, keepdims=True)
        parts.append(sc * lax.rsqrt(var + LN_EPS))
    hn = jnp.concatenate(parts, axis=-1)
    y_ml = (hn * gn_ref[...] + sk_ref[...] * xc_ref[...].astype(F32)).astype(BF16)
    y = _dot(y_att, wa_ref[...]) + _dot(y_ml, wm_ref[...])
    out_ref[...] = _layer_norm(ALPHA * h1_ref[...] + y, g_ref[...], b_ref[...])


def _outproj(o, hf, hb, z, xc, h1, og, gn, sk, wa, wm, g, b, *, tm):
    n, d = h1.shape
    da = o.shape[1]
    dm = hf.shape[0]
    const = lambda i: (0, 0)
    row = lambda w: pl.BlockSpec((tm, w), lambda i: (i, 0))
    col = pl.BlockSpec((dm, tm), lambda i: (0, i))
    return pl.pallas_call(
        functools.partial(_outproj_body, dh=dm // MLSTM_HEADS),
        grid=(n // tm,),
        in_specs=[row(da), col, col, row(dm), row(dm), row(d),
                  pl.BlockSpec(og.shape, const), pl.BlockSpec(gn.shape, const), pl.BlockSpec(sk.shape, const),
                  pl.BlockSpec(wa.shape, const), pl.BlockSpec(wm.shape, const),
                  pl.BlockSpec(g.shape, const), pl.BlockSpec(b.shape, const)],
        out_specs=row(d),
        out_shape=jax.ShapeDtypeStruct((n, d), F32),
        compiler_params=_params("parallel"),
        name="outproj",
    )(o, hf, hb, z, xc, h1, og, gn, sk, wa, wm, g, b)


def _pad_lanes(w):
    return jnp.pad(w, [(0, 0)] * (w.ndim - 1) + [(0, LANES - w.shape[-1])])


def _pick(n, pref):
    for t in pref:
        if n % t == 0:
            return t
    raise ValueError(f"no tile for {n}")


def kernel(x, meta_tokens, ffn1_w_gate, ffn1_w_up, ffn1_w_down, ln1_g, ln1_b, w_in, mla_q_norm_g, mla_w_uq, mla_kv_norm_g, mla_w_ukv, attn_out_g, mlstm_conv_w, mlstm_conv_b, mlstm_w_q, mlstm_w_k, mlstm_w_v, mlstm_w_gates, mlstm_b_gates, mlstm_gn_g, mlstm_skip, w_out, ln2_g, ln2_b, ffn2_w_gate, ffn2_w_up, ffn2_w_down, ln3_g, ln3_b):
    batch, seq, d = x.shape
    assert ffn1_w_gate.shape[0] == DEPTH and meta_tokens.shape[0] == N_META
    n = batch * seq
    ql = mla_q_norm_g.shape[-1]
    kvl = mla_kv_norm_g.shape[-1]
    da = attn_out_g.shape[-1]
    dm = mlstm_gn_g.shape[-1]
    dh = dm // MLSTM_HEADS
    dv = da // ATT_HEADS
    nh = MLSTM_HEADS
    half = QK_ROPE // 2
    assert dv == LANES and dh % LANES == 0 and ql % LANES == 0 and kvl % LANES == 0
    assert seq % MLSTM_CHUNK == 0

    wi = w_in[0]
    o1, o2, o3, o4 = ql, ql + kvl, ql + kvl + QK_ROPE, ql + kvl + QK_ROPE + dm
    swap = jnp.concatenate([jnp.arange(half, QK_ROPE), jnp.arange(half)])
    w_kr = wi[:, o2:o3]
    win = jnp.concatenate(
        [wi[:, :o2], _pad_lanes(w_kr), _pad_lanes(w_kr[:, swap]), wi[:, o3:]], axis=1).astype(BF16)
    wuq = mla_w_uq[0].reshape(ql, ATT_HEADS, QK_NOPE + QK_ROPE)
    wqn = wuq[:, :, :QK_NOPE].reshape(ql, -1).astype(BF16)
    wqr = _pad_lanes(wuq[:, :, QK_NOPE:]).reshape(ql, -1).astype(BF16)
    wqs = _pad_lanes(wuq[:, :, QK_NOPE:][:, :, swap]).reshape(ql, -1).astype(BF16)
    wukv = mla_w_ukv[0].reshape(kvl, ATT_HEADS, QK_NOPE + dv)
    wkn = wukv[:, :, :QK_NOPE].reshape(kvl, -1).astype(BF16)
    wvt = wukv[:, :, QK_NOPE:].reshape(kvl, -1).T.astype(BF16)

    wg = mlstm_w_gates[0].reshape(2 * dm, 4, nh)
    bgr = mlstm_b_gates[0].reshape(4, nh)

    def gate_tile(w4, kind_f, kind_b):
        zeros = jnp.zeros_like(w4[..., 0, :])
        cols = [w4[..., kind_f, :]] * 3 + [w4[..., kind_b, :]] * 3 + [zeros] * (GATE_GROUP - 6)
        tile = jnp.stack(cols, axis=-1).reshape(*w4.shape[:-2], nh * GATE_GROUP)
        return _pad_lanes(tile)

    wgate = jnp.concatenate([gate_tile(wg, 0, 2), gate_tile(wg, 1, 3)], axis=-1)
    wgc = wgate[:dm].astype(BF16)
    wgm = wgate[dm:].astype(BF16)
    bg = jnp.concatenate([gate_tile(bgr, 0, 2), gate_tile(bgr, 1, 3)], axis=-1)[None, :]
    cw = jnp.pad(mlstm_conv_w[0], ((0, SUBLANES - CONV_K), (0, 0)))
    cb = mlstm_conv_b[0][None, :]
    wk_m, wv_m = (w[0].astype(BF16) for w in (mlstm_w_k, mlstm_w_v))
    wqt_m, wkt_m, wvt_m = (jnp.swapaxes(w[0], 1, 2).astype(BF16) for w in (mlstm_w_q, mlstm_w_k, mlstm_w_v))
    wo = w_out[0].astype(BF16)
    row = lambda p: p[0][None, :]

    pos = jnp.arange(N_META + seq, dtype=F32)
    inv = ROPE_BASE ** (-jnp.arange(0, QK_ROPE, 2, dtype=F32) / QK_ROPE)
    ang = pos[:, None] * inv[None, :]
    ctab = _pad_lanes(jnp.concatenate([jnp.cos(ang), jnp.cos(ang)], axis=-1))
    stab = _pad_lanes(jnp.concatenate([-jnp.sin(ang), jnp.sin(ang)], axis=-1))

    ffn1 = [w[0].astype(BF16) for w in (ffn1_w_gate, ffn1_w_up, ffn1_w_down)]
    ffn2 = [w[0].astype(BF16) for w in (ffn2_w_gate, ffn2_w_up, ffn2_w_down)]
    tf = _pick(ffn1[0].shape[1], (512, 256, 128))
    tm_ffn = _pick(n, (512, 256))
    tm_proj = _pick(seq, (256,))

    xr = x.reshape(n, d)
    h1 = _ffn_ln(xr, *ffn1, row(ln1_g), row(ln1_b), tm=tm_ffn, tf=tf)
    h1m = _ffn_ln(meta_tokens.astype(x.dtype), *ffn1, row(ln1_g), row(ln1_b), tm=N_META, tf=tf)
    proj_w = (win, row(mla_q_norm_g), row(mla_kv_norm_g), wqn, wqr, wqs, wkn, wvt)
    q_a, k_a, vt_a, xm, z = _proj(h1, *proj_w, ctab[N_META:], stab[N_META:], tm=tm_proj, dm=dm)
    _, k_am, vt_am, xm_m, _ = _proj(h1m, *proj_w, ctab[:N_META], stab[:N_META], tm=N_META, dm=dm)

    prep_w = (cw, cb, wk_m, wv_m, wqt_m, wkt_m, wvt_m, wgc, wgm, bg)
    tm_prep = _pick(seq, (512, MLSTM_CHUNK))
    k_m, qt_m, kt_m, vt_m, xc, gc, gr = _prep(
        xm, xm_m, *prep_w, batch=batch, tm=tm_prep, chunk=MLSTM_CHUNK, meta=False)
    k_mm, v_mm, gc_m = _prep(xm, xm_m, *prep_w, batch=batch, tm=N_META, chunk=N_META, meta=True)
    hf, hb = _scan(k_m, qt_m, kt_m, vt_m, gc, gr, k_mm, v_mm, gc_m, batch=batch, chunk=MLSTM_CHUNK)

    o = _attn(q_a, k_a, vt_a, k_am, vt_am, batch=batch, tq=_pick(seq, (1024, 512, 256)), tk=_pick(seq, (512, 256)))

    h2 = _outproj(o, hf, hb, z, xc, h1, row(attn_out_g), row(mlstm_gn_g), row(mlstm_skip),
                  wo[:da], wo[da:], row(ln2_g), row(ln2_b), tm=tm_proj)
    out = _ffn_ln(h2, *ffn2, row(ln3_g), row(ln3_b), tm=tm_ffn, tf=tf)
    return out.reshape(batch, seq, d)
```

```python
import functools

import jax
import jax.numpy as jnp
from jax import lax
from jax.experimental import pallas as pl
from jax.experimental.pallas import tpu as pltpu

F32 = jnp.float32
BF16 = jnp.bfloat16

N_META = 16
ATT_HEADS = 8
QK_NOPE = 128
QK_ROPE = 64
MLSTM_HEADS = 4
CONV_K = 5
ROPE_BASE = 10000.0
LN_EPS = 1e-5
RMS_EPS = 1e-6
DEPTH = 1
ALPHA = (2 * DEPTH) ** 0.25

LANES = 128
SUBLANES = 8
VMEM_LIMIT = 56 * 1024 * 1024

BF16_ROWS = 16
V_PAD = BF16_ROWS
MLSTM_CHUNK = 256
GATE_GROUP = 8
LOWEST = float(jnp.finfo(jnp.float32).min)
NEG = -1e30
LOG2E = 1.4426950408889634


def _dot(a, b):
    return jnp.dot(a, b, preferred_element_type=F32)


def _dot_nt(a, b):
    return lax.dot_general(a, b, (((1,), (1,)), ((), ())), preferred_element_type=F32)


def _dot_tn(a, b):
    return lax.dot_general(a, b, (((0,), (0,)), ((), ())), preferred_element_type=F32)


def _layer_norm(y, g, b):
    mu = jnp.mean(y, axis=-1, keepdims=True)
    yc = y - mu
    var = jnp.mean(yc * yc, axis=-1, keepdims=True)
    return yc * lax.rsqrt(var + LN_EPS) * g + b


def _rms_norm(y, g):
    return y * lax.rsqrt(jnp.mean(y * y, axis=-1, keepdims=True) + RMS_EPS) * g


def _params(*sem):
    return pltpu.CompilerParams(dimension_semantics=sem, vmem_limit_bytes=VMEM_LIMIT)


def _ffn_ln_body(*refs, nf, cast_weights, n_side):
    x_ref, wg_ref, wu_ref, wd_ref, g_ref, b_ref = refs[:6]
    side_in = refs[6:6 + n_side]
    o_ref = refs[6 + n_side]
    copies = refs[7 + n_side:-2]
    acc_ref, xb_ref = refs[-2:]
    f = pl.program_id(1)

    @pl.when(f == 0)
    def _():
        xb_ref[...] = x_ref[...].astype(BF16)
        acc_ref[...] = jnp.zeros_like(acc_ref)

    wg, wu, wd = wg_ref[...], wu_ref[...], wd_ref[...]
    if cast_weights:
        wg, wu, wd = wg.astype(BF16), wu.astype(BF16), wd.astype(BF16)
        for dst, w in zip(copies, (wg, wu, wd)):
            dst[...] = w
    for src, dst in zip(side_in, copies):
        dst[...] = src[...].astype(BF16)

    xb = xb_ref[...]
    gate = _dot(xb, wg)
    up = _dot(xb, wu)
    act = (gate * jax.nn.sigmoid(gate) * up).astype(BF16)
    acc_ref[...] += _dot(act, wd)

    @pl.when(f == nf - 1)
    def _():
        y = ALPHA * x_ref[...] + 0.5 * acc_ref[...]
        o_ref[...] = _layer_norm(y, g_ref[...], b_ref[...])


def _ffn_ln(x, wg, wu, wd, g, b, *, tm, tf, cast_weights=False, side=()):
    n, d = x.shape
    dff = wg.shape[1]
    nf = dff // tf
    nt = n // tm
    assert n % tm == 0 and dff % tf == 0 and not (cast_weights and side)
    in_specs = [
        pl.BlockSpec((tm, d), lambda i, f: (i, 0)),
        pl.BlockSpec((d, tf), lambda i, f: (0, f)),
        pl.BlockSpec((d, tf), lambda i, f: (0, f)),
        pl.BlockSpec((tf, d), lambda i, f: (f, 0)),
        pl.BlockSpec((1, d), lambda i, f: (0, 0)),
        pl.BlockSpec((1, d), lambda i, f: (0, 0)),
    ]
    out_specs = [pl.BlockSpec((tm, d), lambda i, f: (i, 0))]
    out_shape = [jax.ShapeDtypeStruct((n, d), F32)]
    if cast_weights:
        out_specs += in_specs[1:4]
        out_shape += [jax.ShapeDtypeStruct(w.shape, BF16) for w in (wg, wu, wd)]
    if side:
        assert d % (nt * BF16_ROWS) == 0 and dff % (nt * nf * BF16_ROWS) == 0
        up_spec = pl.BlockSpec((d // nt, tf), lambda i, f: (i, f))
        down_spec = pl.BlockSpec((dff // (nt * nf), d), lambda i, f: (i * nf + f, 0))
        side_specs = [up_spec, up_spec, down_spec]
        in_specs += side_specs
        out_specs += side_specs
        out_shape += [jax.ShapeDtypeStruct(w.shape, BF16) for w in side]
    return pl.pallas_call(
        functools.partial(_ffn_ln_body, nf=nf, cast_weights=cast_weights, n_side=len(side)),
        grid=(nt, nf),
        in_specs=in_specs,
        out_specs=out_specs,
        out_shape=out_shape,
        scratch_shapes=[pltpu.VMEM((tm, d), F32), pltpu.VMEM((tm, d), BF16)],
        compiler_params=_params("parallel", "arbitrary"),
        name="ffn_ln",
    )(x, wg, wu, wd, g, b, *side)


def _proj_body(h_ref, win_ref, qg_ref, kvg_ref, wqn_ref, wqr_ref, wqs_ref, wkn_ref, wvt_ref,
               cos_ref, sin_ref, q_ref, k_ref, vt_ref, xm_ref, z_ref, *, ql, kvl, dm, scale):
    hb = h_ref[...].astype(BF16)
    u = _dot(hb, win_ref[...])
    o1 = ql
    o2 = o1 + kvl
    o3 = o2 + LANES
    o4 = o3 + LANES
    o5 = o4 + dm
    xm_ref[...] = u[:, o4:o5]
    z_ref[...] = u[:, o5:]
    c = cos_ref[...]
    s = sin_ref[...]
    qn = _rms_norm(u[:, :o1], qg_ref[...]).astype(BF16)
    qa = _dot(qn, wqn_ref[...])
    qr = _dot(qn, wqr_ref[...])
    qs = _dot(qn, wqs_ref[...])
    kvn = _rms_norm(u[:, o1:o2], kvg_ref[...]).astype(BF16)
    kn = _dot(kvn, wkn_ref[...])
    vt = _dot_nt(wvt_ref[...], kvn).astype(BF16)
    dv = vt.shape[0] // ATT_HEADS
    ones_rows = (lax.broadcasted_iota(jnp.int32, (V_PAD, vt.shape[1]), 0) == 0).astype(BF16)
    for h in range(ATT_HEADS):
        lo = h * (dv + V_PAD)
        vt_ref[lo:lo + dv, :] = vt[h * dv:(h + 1) * dv]
        vt_ref[lo + dv:lo + dv + V_PAD, :] = ones_rows
    k_rope = (u[:, o2:o3] * c + u[:, o3:o4] * s).astype(BF16)
    for h in range(ATT_HEADS):
        sl = slice(h * LANES, (h + 1) * LANES)
        lo = 2 * h * LANES
        q_ref[:, lo:lo + LANES] = (qa[:, sl] * scale).astype(BF16)
        q_ref[:, lo + LANES:lo + 2 * LANES] = ((qr[:, sl] * c + qs[:, sl] * s) * scale).astype(BF16)
        k_ref[:, lo:lo + LANES] = kn[:, sl].astype(BF16)
        k_ref[:, lo + LANES:lo + 2 * LANES] = k_rope


def _proj(h, win, qg, kvg, wqn, wqr, wqs, wkn, wvt, ctab, stab, *, tm, dm):
    n, d = h.shape
    ql = qg.shape[1]
    kvl = kvg.shape[1]
    da = wvt.shape[0] + ATT_HEADS * V_PAD
    npos = ctab.shape[0] // tm
    hq = ATT_HEADS * 2 * LANES
    const = lambda i: (0, 0)
    scale = float((QK_NOPE + QK_ROPE) ** -0.5 * LOG2E)
    return pl.pallas_call(
        functools.partial(_proj_body, ql=ql, kvl=kvl, dm=dm, scale=scale),
        grid=(n // tm,),
        in_specs=[
            pl.BlockSpec((tm, d), lambda i: (i, 0)),
            pl.BlockSpec(win.shape, const),
            pl.BlockSpec(qg.shape, const),
            pl.BlockSpec(kvg.shape, const),
            pl.BlockSpec(wqn.shape, const),
            pl.BlockSpec(wqr.shape, const),
            pl.BlockSpec(wqs.shape, const),
            pl.BlockSpec(wkn.shape, const),
            pl.BlockSpec(wvt.shape, const),
            pl.BlockSpec((tm, LANES), lambda i: (i % npos, 0)),
            pl.BlockSpec((tm, LANES), lambda i: (i % npos, 0)),
        ],
        out_specs=[
            pl.BlockSpec((tm, hq), lambda i: (i, 0)),
            pl.BlockSpec((tm, hq), lambda i: (i, 0)),
            pl.BlockSpec((da, tm), lambda i: (0, i)),
            pl.BlockSpec((tm, dm), lambda i: (i, 0)),
            pl.BlockSpec((tm, dm), lambda i: (i, 0)),
        ],
        out_shape=[
            jax.ShapeDtypeStruct((n, hq), BF16),
            jax.ShapeDtypeStruct((n, hq), BF16),
            jax.ShapeDtypeStruct((da, n), BF16),
            jax.ShapeDtypeStruct((n, dm), F32),
            jax.ShapeDtypeStruct((n, dm), F32),
        ],
        compiler_params=_params("parallel"),
        name="proj",
    )(h, win, qg, kvg, wqn, wqr, wqs, wkn, wvt, ctab, stab)


def _log_sigmoid(x):
    return jnp.minimum(x, 0.0) - jnp.log1p(jnp.exp(-jnp.abs(x)))


def _gate_scans(gi, lf, chunk):
    row = lax.broadcasted_iota(jnp.int32, (chunk, LANES), 0)
    lane = lax.broadcasted_iota(jnp.int32, (chunk, LANES), 1)
    j = lane % GATE_GROUP
    fwd = j < 3

    def scan(val, combine, identity):
        pre = val
        suf = val
        k = 1
        while k < chunk:
            pre = combine(pre, jnp.where(row >= k, pltpu.roll(pre, k, 0), identity))
            suf = combine(suf, jnp.where(row < chunk - k, pltpu.roll(suf, chunk - k, 0), identity))
            k *= 2
        return jnp.where(fwd, pre, suf)

    b = scan(lf, jnp.add, 0.0)
    r = gi - b
    cm = scan(r, jnp.maximum, LOWEST)
    return jnp.where(j % 3 == 0, b, jnp.where(j % 3 == 1, cm, r))


def _prep_body(prev_ref, x_ref, next_ref, mtail_ref, cw_ref, cb_ref, wk_ref, wv_ref, wqt_ref, wkt_ref,
               wvt_ref, wgc_ref, wgm_ref, bg_ref, *rest, tm, chunk, dh, meta):
    if meta:
        k_ref, v_ref, gc_ref, xs_ref = rest
    else:
        k_ref, qt_ref, kt_ref, vt_ref, xc_ref, gc_ref, gr_ref, xs_ref = rest
    i = pl.program_id(1)
    nt = pl.num_programs(1)
    x = x_ref[...]
    if meta:
        prev = jnp.zeros_like(prev_ref[...])
        nxt = next_ref[...]
    else:
        prev = jnp.where(i == 0, mtail_ref[...], prev_ref[...])
        nxt = jnp.where(i == nt - 1, 0.0, next_ref[...])
    xs_ref[0:SUBLANES, :] = prev
    xs_ref[SUBLANES:SUBLANES + tm, :] = x
    xs_ref[SUBLANES + tm:2 * SUBLANES + tm, :] = nxt
    acc = jnp.broadcast_to(cb_ref[...], x.shape)
    for t in range(CONV_K):
        off = SUBLANES - CONV_K // 2 + t
        acc = acc + cw_ref[t:t + 1, :] * xs_ref[off:off + tm, :]
    xc = acc * jax.nn.sigmoid(acc)
    xcb = xc.astype(BF16)
    xmb = x.astype(BF16)
    k_scale = dh ** -0.5
    if not meta:
        xc_ref[...] = xcb
        ones_rows = (lax.broadcasted_iota(jnp.int32, (V_PAD, tm), 0) == 0).astype(BF16)
    for h in range(MLSTM_HEADS):
        sl = slice(h * dh, (h + 1) * dh)
        k_ref[:, sl] = (_dot(xcb[:, sl], wk_ref[h]) * k_scale).astype(BF16)
        if meta:
            v_ref[:, sl] = _dot(xmb[:, sl], wv_ref[h]).astype(BF16)
        else:
            qt_ref[sl, :] = _dot_nt(wqt_ref[h], xcb[:, sl]).astype(BF16)
            kt_ref[sl, :] = (_dot_nt(wkt_ref[h], xcb[:, sl]) * k_scale).astype(BF16)
            lo = h * (dh + V_PAD)
            vt_ref[lo:lo + dh, :] = _dot_nt(wvt_ref[h], xmb[:, sl]).astype(BF16)
            vt_ref[lo + dh:lo + dh + V_PAD, :] = ones_rows
    g = _dot(xcb, wgc_ref[...]) + _dot(xmb, wgm_ref[...]) + bg_ref[...]
    gi = g[:, :LANES]
    lf = _log_sigmoid(g[:, LANES:])
    for ci in range(tm // chunk):
        rows = slice(ci * chunk, (ci + 1) * chunk)
        out = _gate_scans(gi[rows], lf[rows], chunk)
        gc_ref[rows, :] = out
        if not meta:
            gr_ref[ci] = out.T


def _prep(xm, xm_meta, cw, cb, wk, wv, wqt, wkt, wvt, wgc, wgm, bg, *, batch, tm, chunk, meta):
    dm = xm.shape[1]
    dh = dm // MLSTM_HEADS
    seq = xm.shape[0] // batch
    if meta:
        nt = 1
        n_out = batch * N_META
        x_arr = xm_meta
        x_spec = pl.BlockSpec((tm, dm), lambda b, i: (0, 0))
        prev_spec = pl.BlockSpec((SUBLANES, dm), lambda b, i: (0, 0))
        next_spec = pl.BlockSpec((SUBLANES, dm), lambda b, i: (b * (seq // SUBLANES), 0))
    else:
        nt = seq // tm
        n_out = batch * seq
        x_arr = xm
        last_blk = batch * seq // SUBLANES - 1
        x_spec = pl.BlockSpec((tm, dm), lambda b, i: (b * nt + i, 0))
        prev_spec = pl.BlockSpec(
            (SUBLANES, dm), lambda b, i: (jnp.maximum((b * nt + i) * (tm // SUBLANES) - 1, 0), 0))
        next_spec = pl.BlockSpec(
            (SUBLANES, dm), lambda b, i: (jnp.minimum((b * nt + i + 1) * (tm // SUBLANES), last_blk), 0))
    const2 = lambda b, i: (0, 0)
    const3 = lambda b, i: (0, 0, 0)
    row_spec = lambda w: pl.BlockSpec((tm, w), lambda b, i: (b * nt + i, 0))
    col_spec = lambda r: pl.BlockSpec((r, tm), lambda b, i: (0, b * nt + i))
    bf = lambda *shape: jax.ShapeDtypeStruct(shape, BF16)
    gc_shape = jax.ShapeDtypeStruct((n_out, LANES), F32)
    if meta:
        out_specs = [row_spec(dm), row_spec(dm), row_spec(LANES)]
        out_shape = [bf(n_out, dm), bf(n_out, dm), gc_shape]
    else:
        dva = dm + MLSTM_HEADS * V_PAD
        out_specs = [row_spec(dm), col_spec(dm), col_spec(dm), col_spec(dva), row_spec(dm), row_spec(LANES),
                     pl.BlockSpec((tm // chunk, LANES, chunk), lambda b, i: (b * nt + i, 0, 0))]
        out_shape = [bf(n_out, dm), bf(dm, n_out), bf(dm, n_out), bf(dva, n_out), bf(n_out, dm), gc_shape,
                     jax.ShapeDtypeStruct((n_out // chunk, LANES, chunk), F32)]
    return pl.pallas_call(
        functools.partial(_prep_body, tm=tm, chunk=chunk, dh=dh, meta=meta),
        grid=(batch, nt),
        in_specs=[
            prev_spec, x_spec, next_spec,
            pl.BlockSpec((SUBLANES, dm), lambda b, i: (1, 0)),
            pl.BlockSpec(cw.shape, const2),
            pl.BlockSpec(cb.shape, const2),
            pl.BlockSpec(wk.shape, const3),
            pl.BlockSpec(wv.shape, const3),
            pl.BlockSpec(wqt.shape, const3),
            pl.BlockSpec(wkt.shape, const3),
            pl.BlockSpec(wvt.shape, const3),
            pl.BlockSpec(wgc.shape, const2),
            pl.BlockSpec(wgm.shape, const2),
            pl.BlockSpec(bg.shape, const2),
        ],
        out_specs=out_specs,
        out_shape=out_shape,
        scratch_shapes=[pltpu.VMEM((tm + 2 * SUBLANES, dm), F32)],
        compiler_params=_params("parallel", "parallel"),
        name="mlstm_prep_meta" if meta else "mlstm_prep",
    )(xm, x_arr, xm, xm_meta, cw, cb, wk, wv, wqt, wkt, wvt, wgc, wgm, bg)


def _scan_chain(k, qt, kt, vta, r_col, rows, c_ref, m_ref, mask, edge, dh):
    b_row, cm_row, r_row = rows
    m_old = m_ref[...]
    ct = c_ref[...]

    mt = jnp.maximum(m_old, cm_row)
    at = jnp.exp(jnp.where(mask, r_col - mt, NEG)) * _dot(k, qt)
    inter = _dot(ct.astype(BF16), qt)
    w = jnp.exp(m_old - mt)
    den = jnp.sum(at, axis=0, keepdims=True) + w * inter[dh:dh + 1, :]
    scale = 1.0 / jnp.maximum(jnp.abs(den), jnp.exp(-(b_row + mt)))
    ht = _dot(vta[:dh], (at * scale).astype(BF16)) + inter[:dh] * (w * scale)

    b_end = b_row[:, edge:edge + 1]
    m_new = jnp.maximum(b_end + m_old, b_end + cm_row[:, edge:edge + 1])
    decay = jnp.exp(b_end + m_old - m_new)
    kwt = (kt.astype(F32) * jnp.exp(b_end + r_row - m_new)).astype(BF16)
    c_ref[...] = decay * ct + _dot_nt(vta, kwt)
    m_ref[...] = m_new
    return ht


def _scan_body(kf_ref, qtf_ref, ktf_ref, vtf_ref, gcf_ref, grf_ref,
               kb_ref, qtb_ref, ktb_ref, vtb_ref, gcb_ref, grb_ref,
               km_ref, vm_ref, gcm_ref, hf_ref, hb_ref, c_ref, m_ref, *, chunk, dh):
    ci = pl.program_id(1)
    nh = MLSTM_HEADS
    dva = dh + V_PAD

    @pl.when(ci == 0)
    def _():
        gcm = gcm_ref[0]
        km = km_ref[0]
        vm = vm_ref[0]
        first = lax.broadcasted_iota(jnp.int32, (V_PAD, dh), 0) == 0
        for h in range(nh):
            sl = slice(h * dh, (h + 1) * dh)
            lane0 = h * GATE_GROUP
            b_end = gcm[N_META - 1:N_META, lane0:lane0 + 1]
            m_new = jnp.maximum(b_end, b_end + gcm[N_META - 1:N_META, lane0 + 1:lane0 + 2])
            kw = km[:, sl].astype(F32) * jnp.exp(b_end + gcm[:, lane0 + 2:lane0 + 3] - m_new)
            c_ref[h, 0:dh, :] = _dot_tn(vm[:, sl], kw.astype(BF16))
            c_ref[h, dh:dva, :] = jnp.where(first, jnp.sum(kw, axis=0, keepdims=True), 0.0)
            m_ref[h] = m_new
            c_ref[nh + h] = jnp.zeros((dva, dh), F32)
            m_ref[nh + h] = jnp.zeros((1, 1), F32)

    srow = lax.broadcasted_iota(jnp.int32, (chunk, chunk), 0)
    tcol = lax.broadcasted_iota(jnp.int32, (chunk, chunk), 1)
    for h in range(nh):
        sl = slice(h * dh, (h + 1) * dh)
        sla = slice(h * dva, (h + 1) * dva)
        lane0 = h * GATE_GROUP
        rows_f = tuple(grf_ref[0, lane0 + j:lane0 + j + 1, :] for j in range(3))
        rows_b = tuple(grb_ref[0, lane0 + j:lane0 + j + 1, :] for j in range(3, 6))
        hf_ref[sl, :] = _scan_chain(
            kf_ref[:, sl], qtf_ref[sl, :], ktf_ref[sl, :], vtf_ref[sla, :], gcf_ref[:, lane0 + 2:lane0 + 3],
            rows_f, c_ref.at[h], m_ref.at[h], srow <= tcol, chunk - 1, dh).astype(hf_ref.dtype)
        hb_ref[sl, :] = _scan_chain(
            kb_ref[:, sl], qtb_ref[sl, :], ktb_ref[sl, :], vtb_ref[sla, :], gcb_ref[:, lane0 + 5:lane0 + 6],
            rows_b, c_ref.at[nh + h], m_ref.at[nh + h], srow >= tcol, 0, dh).astype(hb_ref.dtype)


def _scan(k, qt, kt, vta, gc, gr, km, vm, gcm, *, batch, chunk):
    n, dm = k.shape
    dh = dm // MLSTM_HEADS
    dva = vta.shape[0]
    nc = n // batch // chunk
    fwd = lambda b, c: (b * nc + c, 0)
    bwd = lambda b, c: (b * nc + nc - 1 - c, 0)
    fwd_t = lambda b, c: (0, b * nc + c)
    bwd_t = lambda b, c: (0, b * nc + nc - 1 - c)
    fwd3 = lambda b, c: (b * nc + c, 0, 0)
    bwd3 = lambda b, c: (b * nc + nc - 1 - c, 0, 0)
    per_b = lambda b, c: (b, 0, 0)

    def specs(idx, idx_t, idx3):
        return [pl.BlockSpec((chunk, dm), idx), pl.BlockSpec((dm, chunk), idx_t), pl.BlockSpec((dm, chunk), idx_t),
                pl.BlockSpec((dva, chunk), idx_t), pl.BlockSpec((chunk, LANES), idx),
                pl.BlockSpec((1, LANES, chunk), idx3)]

    return pl.pallas_call(
        functools.partial(_scan_body, chunk=chunk, dh=dh),
        grid=(batch, nc),
        in_specs=specs(fwd, fwd_t, fwd3) + specs(bwd, bwd_t, bwd3) + [
            pl.BlockSpec((1, N_META, dm), per_b),
            pl.BlockSpec((1, N_META, dm), per_b),
            pl.BlockSpec((1, N_META, LANES), per_b),
        ],
        out_specs=[pl.BlockSpec((dm, chunk), fwd_t), pl.BlockSpec((dm, chunk), bwd_t)],
        out_shape=[jax.ShapeDtypeStruct((dm, n), BF16)] * 2,
        scratch_shapes=[
            pltpu.VMEM((2 * MLSTM_HEADS, dh + V_PAD, dh), F32),
            pltpu.VMEM((2 * MLSTM_HEADS, 1, 1), F32),
        ],
        compiler_params=_params("parallel", "arbitrary"),
        name="mlstm_scan",
    )(k, qt, kt, vta, gc, gr, k, qt, kt, vta, gc, gr,
      km.reshape(batch, N_META, dm), vm.reshape(batch, N_META, dm), gcm.reshape(batch, N_META, LANES))


def _attn_body(q_ref, k_ref, vt_ref, km_ref, vmt_ref, o_ref, *, tk, nk):
    q = q_ref[...]
    dv = o_ref.shape[1]

    def scores(j):
        s = _dot_nt(k_ref[j * tk:(j + 1) * tk, :], q)
        return s, jnp.max(s, axis=0, keepdims=True)

    s_meta = _dot_nt(km_ref[...], q)
    ahead = scores(0)
    m = jnp.maximum(jnp.max(s_meta, axis=0, keepdims=True), ahead[1])
    acc = _dot(vmt_ref[...], jnp.exp2(s_meta - m).astype(BF16))
    for j in range(nk):
        s, s_max = ahead
        if j + 1 < nk:
            ahead = scores(j + 1)
        m_new = jnp.maximum(m, s_max)
        p = jnp.exp2(s - m_new).astype(BF16)
        acc = jnp.exp2(m - m_new) * acc + _dot(vt_ref[:, j * tk:(j + 1) * tk], p)
        m = m_new
    o_ref[...] = (acc[:dv] * (1.0 / acc[dv:dv + 1])).T


def _attn(q, k, vt, km, vmt, *, batch, tq, tk):
    n = q.shape[0]
    seq = n // batch
    nq = seq // tq
    dva = vt.shape[0] // ATT_HEADS
    dv = dva - V_PAD
    dk = 2 * LANES
    return pl.pallas_call(
        functools.partial(_attn_body, tk=tk, nk=seq // tk),
        grid=(batch, ATT_HEADS, nq),
        in_specs=[
            pl.BlockSpec((tq, dk), lambda b, h, i: (b * nq + i, h)),
            pl.BlockSpec((seq, dk), lambda b, h, i: (b, h)),
            pl.BlockSpec((dva, seq), lambda b, h, i: (h, b)),
            pl.BlockSpec((N_META, dk), lambda b, h, i: (0, h)),
            pl.BlockSpec((dva, N_META), lambda b, h, i: (h, 0)),
        ],
        out_specs=pl.BlockSpec((tq, dv), lambda b, h, i: (b * nq + i, h)),
        out_shape=jax.ShapeDtypeStruct((n, ATT_HEADS * dv), F32),
        compiler_params=_params("parallel", "parallel", "arbitrary"),
        name="attn",
    )(q, k, vt, km, vmt)


def _outproj_body(o_ref, hf_ref, hb_ref, z_ref, xc_ref, h1_ref, og_ref, gn_ref, sk_ref,
                  wa_ref, wm_ref, g_ref, b_ref, out_ref, *, dh):
    y_att = _rms_norm(o_ref[...], og_ref[...]).astype(BF16)
    h_sum = (hf_ref[...].astype(F32) + hb_ref[...].astype(F32)).T
    hs = h_sum * jax.nn.sigmoid(z_ref[...])
    parts = []
    for h in range(MLSTM_HEADS):
        seg = hs[:, h * dh:(h + 1) * dh]
        mu = jnp.mean(seg, axis=-1, keepdims=True)
        sc = seg - mu
        var = jnp.mean(sc * sc, axis=-1, keepdims=True)
        parts.append(sc * lax.rsqrt(var + LN_EPS))
    hn = jnp.concatenate(parts, axis=-1)
    y_ml = (hn * gn_ref[...] + sk_ref[...] * xc_ref[...].astype(F32)).astype(BF16)
    y = _dot(y_att, wa_ref[...]) + _dot(y_ml, wm_ref[...])
    out_ref[...] = _layer_norm(ALPHA * h1_ref[...] + y, g_ref[...], b_ref[...])


def _outproj(o, hf, hb, z, xc, h1, og, gn, sk, wa, wm, g, b, *, tm):
    n, d = h1.shape
    da = o.shape[1]
    dm = hf.shape[0]
    const = lambda i: (0, 0)
    row = lambda w: pl.BlockSpec((tm, w), lambda i: (i, 0))
    col = pl.BlockSpec((dm, tm), lambda i: (0, i))
    return pl.pallas_call(
        functools.partial(_outproj_body, dh=dm // MLSTM_HEADS),
        grid=(n // tm,),
        in_specs=[row(da), col, col, row(dm), row(dm), row(d),
                  pl.BlockSpec(og.shape, const), pl.BlockSpec(gn.shape, const), pl.BlockSpec(sk.shape, const),
                  pl.BlockSpec(wa.shape, const, pipeline_mode=pl.Buffered(1)),
                  pl.BlockSpec(wm.shape, const, pipeline_mode=pl.Buffered(1)),
                  pl.BlockSpec(g.shape, const), pl.BlockSpec(b.shape, const)],
        out_specs=row(d),
        out_shape=jax.ShapeDtypeStruct((n, d), F32),
        compiler_params=_params("parallel"),
        name="outproj",
    )(o, hf, hb, z, xc, h1, og, gn, sk, wa, wm, g, b)


def _pad_lanes(w):
    return jnp.pad(w, [(0, 0)] * (w.ndim - 1) + [(0, LANES - w.shape[-1])])


def _pick(n, pref):
    for t in pref:
        if n % t == 0:
            return t
    raise ValueError(f"no tile for {n}")


def kernel(x, meta_tokens, ffn1_w_gate, ffn1_w_up, ffn1_w_down, ln1_g, ln1_b, w_in, mla_q_norm_g, mla_w_uq, mla_kv_norm_g, mla_w_ukv, attn_out_g, mlstm_conv_w, mlstm_conv_b, mlstm_w_q, mlstm_w_k, mlstm_w_v, mlstm_w_gates, mlstm_b_gates, mlstm_gn_g, mlstm_skip, w_out, ln2_g, ln2_b, ffn2_w_gate, ffn2_w_up, ffn2_w_down, ln3_g, ln3_b):
    batch, seq, d = x.shape
    assert ffn1_w_gate.shape[0] == DEPTH and meta_tokens.shape[0] == N_META
    n = batch * seq
    ql = mla_q_norm_g.shape[-1]
    kvl = mla_kv_norm_g.shape[-1]
    da = attn_out_g.shape[-1]
    dm = mlstm_gn_g.shape[-1]
    dh = dm // MLSTM_HEADS
    dv = da // ATT_HEADS
    nh = MLSTM_HEADS
    half = QK_ROPE // 2
    assert dv == LANES and dh % LANES == 0 and ql % LANES == 0 and kvl % LANES == 0
    assert seq % MLSTM_CHUNK == 0

    wi = w_in[0]
    o1, o2, o3, o4 = ql, ql + kvl, ql + kvl + QK_ROPE, ql + kvl + QK_ROPE + dm
    swap = jnp.concatenate([jnp.arange(half, QK_ROPE), jnp.arange(half)])
    w_kr = wi[:, o2:o3]
    win = jnp.concatenate(
        [wi[:, :o2], _pad_lanes(w_kr), _pad_lanes(w_kr[:, swap]), wi[:, o3:]], axis=1).astype(BF16)
    wuq = mla_w_uq[0].reshape(ql, ATT_HEADS, QK_NOPE + QK_ROPE)
    wqn = wuq[:, :, :QK_NOPE].reshape(ql, -1).astype(BF16)
    wqr = _pad_lanes(wuq[:, :, QK_NOPE:]).reshape(ql, -1).astype(BF16)
    wqs = _pad_lanes(wuq[:, :, QK_NOPE:][:, :, swap]).reshape(ql, -1).astype(BF16)
    wukv = mla_w_ukv[0].reshape(kvl, ATT_HEADS, QK_NOPE + dv)
    wkn = wukv[:, :, :QK_NOPE].reshape(kvl, -1).astype(BF16)
    wvt = wukv[:, :, QK_NOPE:].reshape(kvl, -1).T.astype(BF16)

    wg = mlstm_w_gates[0].reshape(2 * dm, 4, nh)
    bgr = mlstm_b_gates[0].reshape(4, nh)

    def gate_tile(w4, kind_f, kind_b):
        zeros = jnp.zeros_like(w4[..., 0, :])
        cols = [w4[..., kind_f, :]] * 3 + [w4[..., kind_b, :]] * 3 + [zeros] * (GATE_GROUP - 6)
        tile = jnp.stack(cols, axis=-1).reshape(*w4.shape[:-2], nh * GATE_GROUP)
        return _pad_lanes(tile)

    wgate = jnp.concatenate([gate_tile(wg, 0, 2), gate_tile(wg, 1, 3)], axis=-1)
    wgc = wgate[:dm].astype(BF16)
    wgm = wgate[dm:].astype(BF16)
    bg = jnp.concatenate([gate_tile(bgr, 0, 2), gate_tile(bgr, 1, 3)], axis=-1)[None, :]
    cw = jnp.pad(mlstm_conv_w[0], ((0, SUBLANES - CONV_K), (0, 0)))
    cb = mlstm_conv_b[0][None, :]
    wk_m, wv_m = (w[0].astype(BF16) for w in (mlstm_w_k, mlstm_w_v))
    wqt_m, wkt_m, wvt_m = (jnp.swapaxes(w[0], 1, 2).astype(BF16) for w in (mlstm_w_q, mlstm_w_k, mlstm_w_v))
    wo = w_out[0].astype(BF16)
    row = lambda p: p[0][None, :]

    pos = jnp.arange(N_META + seq, dtype=F32)
    inv = ROPE_BASE ** (-jnp.arange(0, QK_ROPE, 2, dtype=F32) / QK_ROPE)
    ang = pos[:, None] * inv[None, :]
    ctab = _pad_lanes(jnp.concatenate([jnp.cos(ang), jnp.cos(ang)], axis=-1))
    stab = _pad_lanes(jnp.concatenate([-jnp.sin(ang), jnp.sin(ang)], axis=-1))

    ffn1_f32 = [w[0] for w in (ffn1_w_gate, ffn1_w_up, ffn1_w_down)]
    ffn2_f32 = [w[0] for w in (ffn2_w_gate, ffn2_w_up, ffn2_w_down)]
    tf = _pick(ffn1_f32[0].shape[1], (512, 256, 128))
    tm_ffn = _pick(n, (512, 256))
    tm_proj = _pick(seq, (256,))

    xr = x.reshape(n, d)
    h1m, *ffn1 = _ffn_ln(meta_tokens.astype(x.dtype), *ffn1_f32, row(ln1_g), row(ln1_b),
                         tm=N_META, tf=tf, cast_weights=True)
    h1, *ffn2 = _ffn_ln(xr, *ffn1, row(ln1_g), row(ln1_b), tm=tm_ffn, tf=tf, side=ffn2_f32)
    proj_w = (win, row(mla_q_norm_g), row(mla_kv_norm_g), wqn, wqr, wqs, wkn, wvt)
    q_a, k_a, vt_a, xm, z = _proj(h1, *proj_w, ctab[N_META:], stab[N_META:], tm=tm_proj, dm=dm)
    _, k_am, vt_am, xm_m, _ = _proj(h1m, *proj_w, ctab[:N_META], stab[:N_META], tm=N_META, dm=dm)

    prep_w = (cw, cb, wk_m, wv_m, wqt_m, wkt_m, wvt_m, wgc, wgm, bg)
    tm_prep = _pick(seq, (512, MLSTM_CHUNK))
    k_m, qt_m, kt_m, vt_m, xc, gc, gr = _prep(
        xm, xm_m, *prep_w, batch=batch, tm=tm_prep, chunk=MLSTM_CHUNK, meta=False)
    k_mm, v_mm, gc_m = _prep(xm, xm_m, *prep_w, batch=batch, tm=N_META, chunk=N_META, meta=True)
    hf, hb = _scan(k_m, qt_m, kt_m, vt_m, gc, gr, k_mm, v_mm, gc_m, batch=batch, chunk=MLSTM_CHUNK)

    o = _attn(q_a, k_a, vt_a, k_am, vt_am, batch=batch, tq=_pick(seq, (2048, 1024, 512, 256)), tk=_pick(seq, (512, 256)))

    h2 = _outproj(o, hf, hb, z, xc, h1, row(attn_out_g), row(mlstm_gn_g), row(mlstm_skip),
                  wo[:da], wo[da:], row(ln2_g), row(ln2_b), tm=_pick(seq, (512, 256)))
    out, = _ffn_ln(h2, *ffn2, row(ln3_g), row(ln3_b), tm=tm_ffn, tf=tf)
    return out.reshape(batch, seq, d)
```

```python
import functools

import jax
import jax.numpy as jnp
from jax import lax
from jax.experimental import pallas as pl
from jax.experimental.pallas import tpu as pltpu

F32 = jnp.float32
BF16 = jnp.bfloat16

N_META = 16
ATT_HEADS = 8
QK_NOPE = 128
QK_ROPE = 64
MLSTM_HEADS = 4
CONV_K = 5
ROPE_BASE = 10000.0
LN_EPS = 1e-5
RMS_EPS = 1e-6
DEPTH = 1
ALPHA = (2 * DEPTH) ** 0.25

LANES = 128
SUBLANES = 8
VMEM_LIMIT = 56 * 1024 * 1024

BF16_ROWS = 16
V_PAD = BF16_ROWS
MLSTM_CHUNK = 256
GATE_GROUP = 8
LOWEST = float(jnp.finfo(jnp.float32).min)
NEG = -1e30
LOG2E = 1.4426950408889634


def _dot(a, b):
    return jnp.dot(a, b, preferred_element_type=F32)


def _dot_nt(a, b):
    return lax.dot_general(a, b, (((1,), (1,)), ((), ())), preferred_element_type=F32)


def _dot_tn(a, b):
    return lax.dot_general(a, b, (((0,), (0,)), ((), ())), preferred_element_type=F32)


def _layer_norm(y, g, b):
    mu = jnp.mean(y, axis=-1, keepdims=True)
    yc = y - mu
    var = jnp.mean(yc * yc, axis=-1, keepdims=True)
    return yc * lax.rsqrt(var + LN_EPS) * g + b


def _rms_norm(y, g):
    return y * lax.rsqrt(jnp.mean(y * y, axis=-1, keepdims=True) + RMS_EPS) * g


def _params(*sem):
    return pltpu.CompilerParams(dimension_semantics=sem, vmem_limit_bytes=VMEM_LIMIT)


def _ffn_ln_body(x_ref, wg_ref, wu_ref, wd_ref, g_ref, b_ref, o_ref, *rest, cast_weights, n_sub):
    copies = rest[:-2]
    acc_ref, xb_ref = rest[-2:]
    i = pl.program_id(0)
    f = pl.program_id(1)
    nt = pl.num_programs(0) - 1
    cur = i % 2
    rows = o_ref.shape[0] // n_sub

    @pl.when((i == 0) & (f == 0))
    def _():
        acc_ref[1] = jnp.zeros(acc_ref.shape[1:], F32)

    def norm_previous_tile():
        r0 = pl.multiple_of(jnp.minimum(f, n_sub - 1) * rows, rows)
        y = 0.5 * acc_ref[1 - cur, pl.ds(r0, rows), :]
        o_ref[pl.ds(r0, rows), :] = _layer_norm(y, g_ref[...], b_ref[...])

    @pl.when(i < nt)
    def _():
        @pl.when(f == 0)
        def _():
            x = x_ref[...]
            xb_ref[...] = x.astype(BF16)
            acc_ref[cur] = (2.0 * ALPHA) * x

        wg, wu, wd = wg_ref[...], wu_ref[...], wd_ref[...]
        if cast_weights:
            wg, wu, wd = wg.astype(BF16), wu.astype(BF16), wd.astype(BF16)
            for dst, w in zip(copies, (wg, wu, wd)):
                dst[...] = w

        norm_previous_tile()
        xb = xb_ref[...]
        gate = _dot(xb, wg)
        up = _dot(xb, wu)
        act = (gate * jax.nn.sigmoid(gate) * up).astype(BF16)
        acc_ref[cur] += _dot(act, wd)

    @pl.when(i == nt)
    def _():
        norm_previous_tile()


def _ffn_ln(x, wg, wu, wd, g, b, *, tm, tf, cast_weights=False):
    n, d = x.shape
    dff = wg.shape[1]
    nf = dff // tf
    nt = n // tm
    assert n % tm == 0 and dff % tf == 0
    n_sub = 1
    while 2 * n_sub <= nf and tm % (2 * n_sub * SUBLANES) == 0:
        n_sub *= 2
    row_i = lambda i: jnp.minimum(i, nt - 1)
    col_f = lambda i, f: jnp.where(i < nt, f, nf - 1)
    in_specs = [
        pl.BlockSpec((tm, d), lambda i, f: (row_i(i), 0)),
        pl.BlockSpec((d, tf), lambda i, f: (0, col_f(i, f))),
        pl.BlockSpec((d, tf), lambda i, f: (0, col_f(i, f))),
        pl.BlockSpec((tf, d), lambda i, f: (col_f(i, f), 0)),
        pl.BlockSpec((1, d), lambda i, f: (0, 0)),
        pl.BlockSpec((1, d), lambda i, f: (0, 0)),
    ]
    out_specs = [pl.BlockSpec((tm, d), lambda i, f: (jnp.maximum(i - 1, 0), 0))]
    out_shape = [jax.ShapeDtypeStruct((n, d), F32)]
    if cast_weights:
        out_specs += in_specs[1:4]
        out_shape += [jax.ShapeDtypeStruct(w.shape, BF16) for w in (wg, wu, wd)]
    return pl.pallas_call(
        functools.partial(_ffn_ln_body, cast_weights=cast_weights, n_sub=n_sub),
        grid=(nt + 1, nf),
        in_specs=in_specs,
        out_specs=out_specs,
        out_shape=out_shape,
        scratch_shapes=[pltpu.VMEM((2, tm, d), F32), pltpu.VMEM((tm, d), BF16)],
        compiler_params=_params("arbitrary", "arbitrary"),
        name="ffn_ln",
    )(x, wg, wu, wd, g, b)


def _proj_body(h_ref, win_ref, qg_ref, kvg_ref, wqn_ref, wqr_ref, wqs_ref, wkn_ref, wvt_ref,
               cos_ref, sin_ref, *rest, ql, kvl, dm, scale, n_side):
    side_in = rest[:n_side]
    q_ref, k_ref, vt_ref, xm_ref, z_ref = rest[n_side:n_side + 5]
    for src, dst in zip(side_in, rest[n_side + 5:]):
        dst[...] = src[...].astype(BF16)
    hb = h_ref[...].astype(BF16)
    u = _dot(hb, win_ref[...])
    o1 = ql
    o2 = o1 + kvl
    o3 = o2 + LANES
    o4 = o3 + LANES
    o5 = o4 + dm
    xm_ref[...] = u[:, o4:o5]
    z_ref[...] = u[:, o5:]
    c = cos_ref[...]
    s = sin_ref[...]
    qn = _rms_norm(u[:, :o1], qg_ref[...]).astype(BF16)
    qa = _dot(qn, wqn_ref[...])
    qr = _dot(qn, wqr_ref[...])
    qs = _dot(qn, wqs_ref[...])
    kvn = _rms_norm(u[:, o1:o2], kvg_ref[...]).astype(BF16)
    kn = _dot(kvn, wkn_ref[...])
    vt = _dot_nt(wvt_ref[...], kvn).astype(BF16)
    dv = vt.shape[0] // ATT_HEADS
    ones_rows = (lax.broadcasted_iota(jnp.int32, (V_PAD, vt.shape[1]), 0) == 0).astype(BF16)
    for h in range(ATT_HEADS):
        lo = h * (dv + V_PAD)
        vt_ref[lo:lo + dv, :] = vt[h * dv:(h + 1) * dv]
        vt_ref[lo + dv:lo + dv + V_PAD, :] = ones_rows
    k_rope = (u[:, o2:o3] * c + u[:, o3:o4] * s).astype(BF16)
    for h in range(ATT_HEADS):
        sl = slice(h * LANES, (h + 1) * LANES)
        lo = 2 * h * LANES
        q_ref[:, lo:lo + LANES] = (qa[:, sl] * scale).astype(BF16)
        q_ref[:, lo + LANES:lo + 2 * LANES] = ((qr[:, sl] * c + qs[:, sl] * s) * scale).astype(BF16)
        k_ref[:, lo:lo + LANES] = kn[:, sl].astype(BF16)
        k_ref[:, lo + LANES:lo + 2 * LANES] = k_rope


def _slice_spec(shape, steps):
    rows, cols = shape
    for csplit in (1, 2, 4, 8):
        rblocks = steps // csplit
        if (steps % csplit == 0 and rows % (rblocks * BF16_ROWS) == 0 and cols % (csplit * LANES) == 0):
            return pl.BlockSpec((rows // rblocks, cols // csplit), lambda i: (i // csplit, i % csplit))
    raise ValueError(f"cannot slice {shape} over {steps} steps")


def _proj(h, win, qg, kvg, wqn, wqr, wqs, wkn, wvt, ctab, stab, *, tm, dm, side=()):
    n, d = h.shape
    ql = qg.shape[1]
    kvl = kvg.shape[1]
    da = wvt.shape[0] + ATT_HEADS * V_PAD
    npos = ctab.shape[0] // tm
    hq = ATT_HEADS * 2 * LANES
    const = lambda i: (0, 0)
    scale = float((QK_NOPE + QK_ROPE) ** -0.5 * LOG2E)
    side_specs = [_slice_spec(w.shape, n // tm) for w in side]
    return pl.pallas_call(
        functools.partial(_proj_body, ql=ql, kvl=kvl, dm=dm, scale=scale, n_side=len(side)),
        grid=(n // tm,),
        in_specs=[
            pl.BlockSpec((tm, d), lambda i: (i, 0)),
            pl.BlockSpec(win.shape, const),
            pl.BlockSpec(qg.shape, const),
            pl.BlockSpec(kvg.shape, const),
            pl.BlockSpec(wqn.shape, const),
            pl.BlockSpec(wqr.shape, const),
            pl.BlockSpec(wqs.shape, const),
            pl.BlockSpec(wkn.shape, const),
            pl.BlockSpec(wvt.shape, const),
            pl.BlockSpec((tm, LANES), lambda i: (i % npos, 0)),
            pl.BlockSpec((tm, LANES), lambda i: (i % npos, 0)),
        ] + side_specs,
        out_specs=[
            pl.BlockSpec((tm, hq), lambda i: (i, 0)),
            pl.BlockSpec((tm, hq), lambda i: (i, 0)),
            pl.BlockSpec((da, tm), lambda i: (0, i)),
            pl.BlockSpec((tm, dm), lambda i: (i, 0)),
            pl.BlockSpec((tm, dm), lambda i: (i, 0)),
        ] + side_specs,
        out_shape=[
            jax.ShapeDtypeStruct((n, hq), BF16),
            jax.ShapeDtypeStruct((n, hq), BF16),
            jax.ShapeDtypeStruct((da, n), BF16),
            jax.ShapeDtypeStruct((n, dm), F32),
            jax.ShapeDtypeStruct((n, dm), F32),
        ] + [jax.ShapeDtypeStruct(w.shape, BF16) for w in side],
        compiler_params=_params("parallel"),
        name="proj",
    )(h, win, qg, kvg, wqn, wqr, wqs, wkn, wvt, ctab, stab, *side)


def _log_sigmoid(x):
    return jnp.minimum(x, 0.0) - jnp.log1p(jnp.exp(-jnp.abs(x)))


def _gate_scans(gi, lf, chunk):
    row = lax.broadcasted_iota(jnp.int32, (chunk, LANES), 0)
    lane = lax.broadcasted_iota(jnp.int32, (chunk, LANES), 1)
    j = lane % GATE_GROUP
    fwd = j < 3

    def scan(val, combine, identity):
        pre = val
        suf = val
        k = 1
        while k < chunk:
            pre = combine(pre, jnp.where(row >= k, pltpu.roll(pre, k, 0), identity))
            suf = combine(suf, jnp.where(row < chunk - k, pltpu.roll(suf, chunk - k, 0), identity))
            k *= 2
        return jnp.where(fwd, pre, suf)

    b = scan(lf, jnp.add, 0.0)
    r = gi - b
    cm = scan(r, jnp.maximum, LOWEST)
    return jnp.where(j % 3 == 0, b, jnp.where(j % 3 == 1, cm, r))


def _prep_body(prev_ref, x_ref, next_ref, mtail_ref, cw_ref, cb_ref, wk_ref, wv_ref, wqt_ref, wkt_ref,
               wvt_ref, wgc_ref, wgm_ref, bg_ref, *rest, tm, chunk, dh, meta):
    if meta:
        k_ref, v_ref, gc_ref, xs_ref = rest
    else:
        k_ref, qt_ref, kt_ref, vt_ref, xc_ref, gc_ref, gr_ref, xs_ref = rest
    i = pl.program_id(1)
    nt = pl.num_programs(1)
    x = x_ref[...]
    if meta:
        prev = jnp.zeros_like(prev_ref[...])
        nxt = next_ref[...]
    else:
        prev = jnp.where(i == 0, mtail_ref[...], prev_ref[...])
        nxt = jnp.where(i == nt - 1, 0.0, next_ref[...])
    xs_ref[0:SUBLANES, :] = prev
    xs_ref[SUBLANES:SUBLANES + tm, :] = x
    xs_ref[SUBLANES + tm:2 * SUBLANES + tm, :] = nxt
    acc = jnp.broadcast_to(cb_ref[...], x.shape)
    for t in range(CONV_K):
        off = SUBLANES - CONV_K // 2 + t
        acc = acc + cw_ref[t:t + 1, :] * xs_ref[off:off + tm, :]
    xc = acc * jax.nn.sigmoid(acc)
    xcb = xc.astype(BF16)
    xmb = x.astype(BF16)
    k_scale = dh ** -0.5
    if not meta:
        xc_ref[...] = xcb
        ones_rows = (lax.broadcasted_iota(jnp.int32, (V_PAD, tm), 0) == 0).astype(BF16)
    for h in range(MLSTM_HEADS):
        sl = slice(h * dh, (h + 1) * dh)
        k_ref[:, sl] = (_dot(xcb[:, sl], wk_ref[h]) * k_scale).astype(BF16)
        if meta:
            v_ref[:, sl] = _dot(xmb[:, sl], wv_ref[h]).astype(BF16)
        else:
            qt_ref[sl, :] = _dot_nt(wqt_ref[h], xcb[:, sl]).astype(BF16)
            kt_ref[sl, :] = (_dot_nt(wkt_ref[h], xcb[:, sl]) * k_scale).astype(BF16)
            lo = h * (dh + V_PAD)
            vt_ref[lo:lo + dh, :] = _dot_nt(wvt_ref[h], xmb[:, sl]).astype(BF16)
            vt_ref[lo + dh:lo + dh + V_PAD, :] = ones_rows
    g = _dot(xcb, wgc_ref[...]) + _dot(xmb, wgm_ref[...]) + bg_ref[...]
    gi = g[:, :LANES]
    lf = _log_sigmoid(g[:, LANES:])
    for ci in range(tm // chunk):
        rows = slice(ci * chunk, (ci + 1) * chunk)
        out = _gate_scans(gi[rows], lf[rows], chunk)
        gc_ref[rows, :] = out
        if not meta:
            gr_ref[ci] = out.T


def _prep(xm, xm_meta, cw, cb, wk, wv, wqt, wkt, wvt, wgc, wgm, bg, *, batch, tm, chunk, meta):
    dm = xm.shape[1]
    dh = dm // MLSTM_HEADS
    seq = xm.shape[0] // batch
    if meta:
        nt = 1
        n_out = batch * N_META
        x_arr = xm_meta
        x_spec = pl.BlockSpec((tm, dm), lambda b, i: (0, 0))
        prev_spec = pl.BlockSpec((SUBLANES, dm), lambda b, i: (0, 0))
        next_spec = pl.BlockSpec((SUBLANES, dm), lambda b, i: (b * (seq // SUBLANES), 0))
    else:
        nt = seq // tm
        n_out = batch * seq
        x_arr = xm
        last_blk = batch * seq // SUBLANES - 1
        x_spec = pl.BlockSpec((tm, dm), lambda b, i: (b * nt + i, 0))
        prev_spec = pl.BlockSpec(
            (SUBLANES, dm), lambda b, i: (jnp.maximum((b * nt + i) * (tm // SUBLANES) - 1, 0), 0))
        next_spec = pl.BlockSpec(
            (SUBLANES, dm), lambda b, i: (jnp.minimum((b * nt + i + 1) * (tm // SUBLANES), last_blk), 0))
    const2 = lambda b, i: (0, 0)
    const3 = lambda b, i: (0, 0, 0)
    row_spec = lambda w: pl.BlockSpec((tm, w), lambda b, i: (b * nt + i, 0))
    col_spec = lambda r: pl.BlockSpec((r, tm), lambda b, i: (0, b * nt + i))
    bf = lambda *shape: jax.ShapeDtypeStruct(shape, BF16)
    gc_shape = jax.ShapeDtypeStruct((n_out, LANES), F32)
    if meta:
        out_specs = [row_spec(dm), row_spec(dm), row_spec(LANES)]
        out_shape = [bf(n_out, dm), bf(n_out, dm), gc_shape]
    else:
        dva = dm + MLSTM_HEADS * V_PAD
        out_specs = [row_spec(dm), col_spec(dm), col_spec(dm), col_spec(dva), row_spec(dm), row_spec(LANES),
                     pl.BlockSpec((tm // chunk, LANES, chunk), lambda b, i: (b * nt + i, 0, 0))]
        out_shape = [bf(n_out, dm), bf(dm, n_out), bf(dm, n_out), bf(dva, n_out), bf(n_out, dm), gc_shape,
                     jax.ShapeDtypeStruct((n_out // chunk, LANES, chunk), F32)]
    return pl.pallas_call(
        functools.partial(_prep_body, tm=tm, chunk=chunk, dh=dh, meta=meta),
        grid=(batch, nt),
        in_specs=[
            prev_spec, x_spec, next_spec,
            pl.BlockSpec((SUBLANES, dm), lambda b, i: (1, 0)),
            pl.BlockSpec(cw.shape, const2),
            pl.BlockSpec(cb.shape, const2),
            pl.BlockSpec(wk.shape, const3),
            pl.BlockSpec(wv.shape, const3),
            pl.BlockSpec(wqt.shape, const3),
            pl.BlockSpec(wkt.shape, const3),
            pl.BlockSpec(wvt.shape, const3),
            pl.BlockSpec(wgc.shape, const2),
            pl.BlockSpec(wgm.shape, const2),
            pl.BlockSpec(bg.shape, const2),
        ],
        out_specs=out_specs,
        out_shape=out_shape,
        scratch_shapes=[pltpu.VMEM((tm + 2 * SUBLANES, dm), F32)],
        compiler_params=_params("parallel", "parallel"),
        name="mlstm_prep_meta" if meta else "mlstm_prep",
    )(xm, x_arr, xm, xm_meta, cw, cb, wk, wv, wqt, wkt, wvt, wgc, wgm, bg)


def _scan_chain(k, qt, kt, vta, r_col, rows, c_ref, m_ref, mask, edge, dh):
    b_row, cm_row, r_row = rows
    m_old = m_ref[...]
    ct = c_ref[...]

    mt = jnp.maximum(m_old, cm_row)
    at = jnp.exp(jnp.where(mask, r_col - mt, NEG)) * _dot(k, qt)
    inter = _dot(ct.astype(BF16), qt)
    w = jnp.exp(m_old - mt)
    den = jnp.sum(at, axis=0, keepdims=True) + w * inter[dh:dh + 1, :]
    scale = 1.0 / jnp.maximum(jnp.abs(den), jnp.exp(-(b_row + mt)))
    ht = _dot(vta[:dh], (at * scale).astype(BF16)) + inter[:dh] * (w * scale)

    b_end = b_row[:, edge:edge + 1]
    m_new = jnp.maximum(b_end + m_old, b_end + cm_row[:, edge:edge + 1])
    decay = jnp.exp(b_end + m_old - m_new)
    kwt = (kt.astype(F32) * jnp.exp(b_end + r_row - m_new)).astype(BF16)
    c_ref[...] = decay * ct + _dot_nt(vta, kwt)
    m_ref[...] = m_new
    return ht


def _scan_body(kf_ref, qtf_ref, ktf_ref, vtf_ref, gcf_ref, grf_ref,
               kb_ref, qtb_ref, ktb_ref, vtb_ref, gcb_ref, grb_ref,
               km_ref, vm_ref, gcm_ref, hf_ref, hb_ref, c_ref, m_ref, *, chunk, dh):
    ci = pl.program_id(1)
    nh = MLSTM_HEADS
    dva = dh + V_PAD

    @pl.when(ci == 0)
    def _():
        gcm = gcm_ref[0]
        km = km_ref[0]
        vm = vm_ref[0]
        first = lax.broadcasted_iota(jnp.int32, (V_PAD, dh), 0) == 0
        for h in range(nh):
            sl = slice(h * dh, (h + 1) * dh)
            lane0 = h * GATE_GROUP
            b_end = gcm[N_META - 1:N_META, lane0:lane0 + 1]
            m_new = jnp.maximum(b_end, b_end + gcm[N_META - 1:N_META, lane0 + 1:lane0 + 2])
            kw = km[:, sl].astype(F32) * jnp.exp(b_end + gcm[:, lane0 + 2:lane0 + 3] - m_new)
            c_ref[h, 0:dh, :] = _dot_tn(vm[:, sl], kw.astype(BF16))
            c_ref[h, dh:dva, :] = jnp.where(first, jnp.sum(kw, axis=0, keepdims=True), 0.0)
            m_ref[h] = m_new
            c_ref[nh + h] = jnp.zeros((dva, dh), F32)
            m_ref[nh + h] = jnp.zeros((1, 1), F32)

    srow = lax.broadcasted_iota(jnp.int32, (chunk, chunk), 0)
    tcol = lax.broadcasted_iota(jnp.int32, (chunk, chunk), 1)
    for h in range(nh):
        sl = slice(h * dh, (h + 1) * dh)
        sla = slice(h * dva, (h + 1) * dva)
        lane0 = h * GATE_GROUP
        rows_f = tuple(grf_ref[0, lane0 + j:lane0 + j + 1, :] for j in range(3))
        rows_b = tuple(grb_ref[0, lane0 + j:lane0 + j + 1, :] for j in range(3, 6))
        hf_ref[sl, :] = _scan_chain(
            kf_ref[:, sl], qtf_ref[sl, :], ktf_ref[sl, :], vtf_ref[sla, :], gcf_ref[:, lane0 + 2:lane0 + 3],
            rows_f, c_ref.at[h], m_ref.at[h], srow <= tcol, chunk - 1, dh).astype(hf_ref.dtype)
        hb_ref[sl, :] = _scan_chain(
            kb_ref[:, sl], qtb_ref[sl, :], ktb_ref[sl, :], vtb_ref[sla, :], gcb_ref[:, lane0 + 5:lane0 + 6],
            rows_b, c_ref.at[nh + h], m_ref.at[nh + h], srow >= tcol, 0, dh).astype(hb_ref.dtype)


def _scan(k, qt, kt, vta, gc, gr, km, vm, gcm, *, batch, chunk):
    n, dm = k.shape
    dh = dm // MLSTM_HEADS
    dva = vta.shape[0]
    nc = n // batch // chunk
    fwd = lambda b, c: (b * nc + c, 0)
    bwd = lambda b, c: (b * nc + nc - 1 - c, 0)
    fwd_t = lambda b, c: (0, b * nc + c)
    bwd_t = lambda b, c: (0, b * nc + nc - 1 - c)
    fwd3 = lambda b, c: (b * nc + c, 0, 0)
    bwd3 = lambda b, c: (b * nc + nc - 1 - c, 0, 0)
    per_b = lambda b, c: (b, 0, 0)

    def specs(idx, idx_t, idx3):
        return [pl.BlockSpec((chunk, dm), idx), pl.BlockSpec((dm, chunk), idx_t), pl.BlockSpec((dm, chunk), idx_t),
                pl.BlockSpec((dva, chunk), idx_t), pl.BlockSpec((chunk, LANES), idx),
                pl.BlockSpec((1, LANES, chunk), idx3)]

    return pl.pallas_call(
        functools.partial(_scan_body, chunk=chunk, dh=dh),
        grid=(batch, nc),
        in_specs=specs(fwd, fwd_t, fwd3) + specs(bwd, bwd_t, bwd3) + [
            pl.BlockSpec((1, N_META, dm), per_b),
            pl.BlockSpec((1, N_META, dm), per_b),
            pl.BlockSpec((1, N_META, LANES), per_b),
        ],
        out_specs=[pl.BlockSpec((dm, chunk), fwd_t), pl.BlockSpec((dm, chunk), bwd_t)],
        out_shape=[jax.ShapeDtypeStruct((dm, n), BF16)] * 2,
        scratch_shapes=[
            pltpu.VMEM((2 * MLSTM_HEADS, dh + V_PAD, dh), F32),
            pltpu.VMEM((2 * MLSTM_HEADS, 1, 1), F32),
        ],
        compiler_params=_params("parallel", "arbitrary"),
        name="mlstm_scan",
    )(k, qt, kt, vta, gc, gr, k, qt, kt, vta, gc, gr,
      km.reshape(batch, N_META, dm), vm.reshape(batch, N_META, dm), gcm.reshape(batch, N_META, LANES))


def _attn_body(q_ref, k_ref, vt_ref, km_ref, vmt_ref, o_ref, *, tk, nk):
    q = q_ref[...]
    dv = o_ref.shape[1]

    def scores(j):
        s = _dot_nt(k_ref[j * tk:(j + 1) * tk, :], q)
        return s, jnp.max(s, axis=0, keepdims=True)

    s_meta = _dot_nt(km_ref[...], q)
    ahead = scores(0)
    m = jnp.maximum(jnp.max(s_meta, axis=0, keepdims=True), ahead[1])
    acc = _dot(vmt_ref[...], jnp.exp2(s_meta - m).astype(BF16))
    for j in range(nk):
        s, s_max = ahead
        if j + 1 < nk:
            ahead = scores(j + 1)
        m_new = jnp.maximum(m, s_max)
        p = jnp.exp2(s - m_new).astype(BF16)
        acc = jnp.exp2(m - m_new) * acc + _dot(vt_ref[:, j * tk:(j + 1) * tk], p)
        m = m_new
    o_ref[...] = (acc[:dv] * (1.0 / acc[dv:dv + 1])).T


def _attn(q, k, vt, km, vmt, *, batch, tq, tk):
    n = q.shape[0]
    seq = n // batch
    nq = seq // tq
    dva = vt.shape[0] // ATT_HEADS
    dv = dva - V_PAD
    dk = 2 * LANES
    return pl.pallas_call(
        functools.partial(_attn_body, tk=tk, nk=seq // tk),
        grid=(batch, ATT_HEADS, nq),
        in_specs=[
            pl.BlockSpec((tq, dk), lambda b, h, i: (b * nq + i, h)),
            pl.BlockSpec((seq, dk), lambda b, h, i: (b, h)),
            pl.BlockSpec((dva, seq), lambda b, h, i: (h, b)),
            pl.BlockSpec((N_META, dk), lambda b, h, i: (0, h)),
            pl.BlockSpec((dva, N_META), lambda b, h, i: (h, 0)),
        ],
        out_specs=pl.BlockSpec((tq, dv), lambda b, h, i: (b * nq + i, h)),
        out_shape=jax.ShapeDtypeStruct((n, ATT_HEADS * dv), F32),
        compiler_params=_params("parallel", "parallel", "arbitrary"),
        name="attn",
    )(q, k, vt, km, vmt)


def _outproj_body(o_ref, hf_ref, hb_ref, z_ref, xc_ref, h1_ref, og_ref, gn_ref, sk_ref,
                  wa_ref, wm_ref, g_ref, b_ref, out_ref, *, dh, rows):
    for r in range(out_ref.shape[0] // rows):
        rs = slice(r * rows, (r + 1) * rows)
        y_att = _rms_norm(o_ref[rs, :], og_ref[...]).astype(BF16)
        h_sum = (hf_ref[:, rs].astype(F32) + hb_ref[:, rs].astype(F32)).T
        hs = h_sum * jax.nn.sigmoid(z_ref[rs, :])
        parts = []
        for h in range(MLSTM_HEADS):
            seg = hs[:, h * dh:(h + 1) * dh]
            mu = jnp.mean(seg, axis=-1, keepdims=True)
            sc = seg - mu
            var = jnp.mean(sc * sc, axis=-1, keepdims=True)
            parts.append(sc * lax.rsqrt(var + LN_EPS))
        hn = jnp.concatenate(parts, axis=-1)
        y_ml = (hn * gn_ref[...] + sk_ref[...] * xc_ref[rs, :].astype(F32)).astype(BF16)
        y = _dot(y_att, wa_ref[...]) + _dot(y_ml, wm_ref[...])
        out_ref[rs, :] = _layer_norm(ALPHA * h1_ref[rs, :] + y, g_ref[...], b_ref[...])


def _outproj(o, hf, hb, z, xc, h1, og, gn, sk, wo, g, b, *, tm, rows):
    n, d = h1.shape
    da = o.shape[1]
    dm = hf.shape[0]
    assert da == dm and wo.shape[0] == da + dm
    const = lambda i: (0, 0)
    row = lambda w: pl.BlockSpec((tm, w), lambda i: (i, 0))
    col = pl.BlockSpec((dm, tm), lambda i: (0, i))
    return pl.pallas_call(
        functools.partial(_outproj_body, dh=dm // MLSTM_HEADS, rows=rows),
        grid=(n // tm,),
        in_specs=[row(da), col, col, row(dm), row(dm), row(d),
                  pl.BlockSpec(og.shape, const), pl.BlockSpec(gn.shape, const), pl.BlockSpec(sk.shape, const),
                  pl.BlockSpec((da, d), lambda i: (0, 0), pipeline_mode=pl.Buffered(1)),
                  pl.BlockSpec((dm, d), lambda i: (1, 0), pipeline_mode=pl.Buffered(1)),
                  pl.BlockSpec(g.shape, const), pl.BlockSpec(b.shape, const)],
        out_specs=row(d),
        out_shape=jax.ShapeDtypeStruct((n, d), F32),
        compiler_params=_params("parallel"),
        name="outproj",
    )(o, hf, hb, z, xc, h1, og, gn, sk, wo, wo, g, b)


def _pad_lanes(w):
    return jnp.pad(w, [(0, 0)] * (w.ndim - 1) + [(0, LANES - w.shape[-1])])


def _pick(n, pref):
    for t in pref:
        if n % t == 0:
            return t
    raise ValueError(f"no tile for {n}")


def kernel(x, meta_tokens, ffn1_w_gate, ffn1_w_up, ffn1_w_down, ln1_g, ln1_b, w_in, mla_q_norm_g, mla_w_uq, mla_kv_norm_g, mla_w_ukv, attn_out_g, mlstm_conv_w, mlstm_conv_b, mlstm_w_q, mlstm_w_k, mlstm_w_v, mlstm_w_gates, mlstm_b_gates, mlstm_gn_g, mlstm_skip, w_out, ln2_g, ln2_b, ffn2_w_gate, ffn2_w_up, ffn2_w_down, ln3_g, ln3_b):
    batch, seq, d = x.shape
    assert ffn1_w_gate.shape[0] == DEPTH and meta_tokens.shape[0] == N_META
    n = batch * seq
    ql = mla_q_norm_g.shape[-1]
    kvl = mla_kv_norm_g.shape[-1]
    da = attn_out_g.shape[-1]
    dm = mlstm_gn_g.shape[-1]
    dh = dm // MLSTM_HEADS
    dv = da // ATT_HEADS
    nh = MLSTM_HEADS
    half = QK_ROPE // 2
    assert dv == LANES and dh % LANES == 0 and ql % LANES == 0 and kvl % LANES == 0
    assert seq % MLSTM_CHUNK == 0

    wi = w_in[0]
    o1, o2, o3, o4 = ql, ql + kvl, ql + kvl + QK_ROPE, ql + kvl + QK_ROPE + dm
    swap = jnp.concatenate([jnp.arange(half, QK_ROPE), jnp.arange(half)])
    w_kr = wi[:, o2:o3]
    win = jnp.concatenate(
        [wi[:, :o2], _pad_lanes(w_kr), _pad_lanes(w_kr[:, swap]), wi[:, o3:]], axis=1).astype(BF16)
    wuq = mla_w_uq[0].reshape(ql, ATT_HEADS, QK_NOPE + QK_ROPE)
    wqn = wuq[:, :, :QK_NOPE].reshape(ql, -1).astype(BF16)
    wqr = _pad_lanes(wuq[:, :, QK_NOPE:]).reshape(ql, -1).astype(BF16)
    wqs = _pad_lanes(wuq[:, :, QK_NOPE:][:, :, swap]).reshape(ql, -1).astype(BF16)
    wukv = mla_w_ukv[0].reshape(kvl, ATT_HEADS, QK_NOPE + dv)
    wkn = wukv[:, :, :QK_NOPE].reshape(kvl, -1).astype(BF16)
    wvt = wukv[:, :, QK_NOPE:].reshape(kvl, -1).T.astype(BF16)

    wg = mlstm_w_gates[0].reshape(2 * dm, 4, nh)
    bgr = mlstm_b_gates[0].reshape(4, nh)

    def gate_tile(w4, kind_f, kind_b):
        zeros = jnp.zeros_like(w4[..., 0, :])
        cols = [w4[..., kind_f, :]] * 3 + [w4[..., kind_b, :]] * 3 + [zeros] * (GATE_GROUP - 6)
        tile = jnp.stack(cols, axis=-1).reshape(*w4.shape[:-2], nh * GATE_GROUP)
        return _pad_lanes(tile)

    wgate = jnp.concatenate([gate_tile(wg, 0, 2), gate_tile(wg, 1, 3)], axis=-1)
    wgc = wgate[:dm].astype(BF16)
    wgm = wgate[dm:].astype(BF16)
    bg = jnp.concatenate([gate_tile(bgr, 0, 2), gate_tile(bgr, 1, 3)], axis=-1)[None, :]
    cw = jnp.pad(mlstm_conv_w[0], ((0, SUBLANES - CONV_K), (0, 0)))
    cb = mlstm_conv_b[0][None, :]
    wk_m, wv_m = (w[0].astype(BF16) for w in (mlstm_w_k, mlstm_w_v))
    wqt_m, wkt_m, wvt_m = (jnp.swapaxes(w[0], 1, 2).astype(BF16) for w in (mlstm_w_q, mlstm_w_k, mlstm_w_v))
    wo = w_out[0].astype(BF16)
    row = lambda p: p[0][None, :]

    pos = jnp.arange(N_META + seq, dtype=F32)
    inv = ROPE_BASE ** (-jnp.arange(0, QK_ROPE, 2, dtype=F32) / QK_ROPE)
    ang = pos[:, None] * inv[None, :]
    ctab = _pad_lanes(jnp.concatenate([jnp.cos(ang), jnp.cos(ang)], axis=-1))
    stab = _pad_lanes(jnp.concatenate([-jnp.sin(ang), jnp.sin(ang)], axis=-1))

    ffn1_f32 = [w[0] for w in (ffn1_w_gate, ffn1_w_up, ffn1_w_down)]
    ffn2_f32 = [w[0] for w in (ffn2_w_gate, ffn2_w_up, ffn2_w_down)]
    tf = _pick(ffn1_f32[0].shape[1], (512, 256, 128))
    tm_ffn = _pick(n, (512, 256))
    tm_proj = _pick(seq, (256,))

    xr = x.reshape(n, d)
    h1m, *ffn1 = _ffn_ln(meta_tokens.astype(x.dtype), *ffn1_f32, row(ln1_g), row(ln1_b),
                         tm=N_META, tf=tf, cast_weights=True)
    h1, = _ffn_ln(xr, *ffn1, row(ln1_g), row(ln1_b), tm=tm_ffn, tf=tf)
    proj_w = (win, row(mla_q_norm_g), row(mla_kv_norm_g), wqn, wqr, wqs, wkn, wvt)
    q_a, k_a, vt_a, xm, z, *ffn2 = _proj(
        h1, *proj_w, ctab[N_META:], stab[N_META:], tm=tm_proj, dm=dm, side=ffn2_f32)
    _, k_am, vt_am, xm_m, _ = _proj(h1m, *proj_w, ctab[:N_META], stab[:N_META], tm=N_META, dm=dm)

    prep_w = (cw, cb, wk_m, wv_m, wqt_m, wkt_m, wvt_m, wgc, wgm, bg)
    tm_prep = _pick(seq, (512, MLSTM_CHUNK))
    k_m, qt_m, kt_m, vt_m, xc, gc, gr = _prep(
        xm, xm_m, *prep_w, batch=batch, tm=tm_prep, chunk=MLSTM_CHUNK, meta=False)
    k_mm, v_mm, gc_m = _prep(xm, xm_m, *prep_w, batch=batch, tm=N_META, chunk=N_META, meta=True)
    hf, hb = _scan(k_m, qt_m, kt_m, vt_m, gc, gr, k_mm, v_mm, gc_m, batch=batch, chunk=MLSTM_CHUNK)

    o = _attn(q_a, k_a, vt_a, k_am, vt_am, batch=batch, tq=_pick(seq, (2048, 1024, 512, 256)), tk=_pick(seq, (512, 256)))

    h2 = _outproj(o, hf, hb, z, xc, h1, row(attn_out_g), row(mlstm_gn_g), row(mlstm_skip),
                  wo, row(ln2_g), row(ln2_b), tm=_pick(seq, (512, 256)), rows=_pick(seq, (256,)))
    out, = _ffn_ln(h2, *ffn2, row(ln3_g), row(ln3_b), tm=tm_ffn, tf=tf)
    return out.reshape(batch, seq, d)
```

```python
import functools

import jax
import jax.numpy as jnp
from jax import lax
from jax.experimental import pallas as pl
from jax.experimental.pallas import tpu as pltpu

F32 = jnp.float32
BF16 = jnp.bfloat16

N_META = 16
ATT_HEADS = 8
QK_NOPE = 128
QK_ROPE = 64
MLSTM_HEADS = 4
CONV_K = 5
ROPE_BASE = 10000.0
LN_EPS = 1e-5
RMS_EPS = 1e-6
DEPTH = 1
ALPHA = (2 * DEPTH) ** 0.25

LANES = 128
SUBLANES = 8
VMEM_LIMIT = 56 * 1024 * 1024

BF16_ROWS = 16
V_PAD = BF16_ROWS
MLSTM_CHUNK = 256
GATE_GROUP = 8
LOWEST = float(jnp.finfo(jnp.float32).min)
LARGEST = float(jnp.finfo(jnp.float32).max)
NEG = -1e30
LOG2E = 1.4426950408889634


def _dot(a, b):
    return jnp.dot(a, b, preferred_element_type=F32)


def _dot_nt(a, b):
    return lax.dot_general(a, b, (((1,), (1,)), ((), ())), preferred_element_type=F32)


def _dot_tn(a, b):
    return lax.dot_general(a, b, (((0,), (0,)), ((), ())), preferred_element_type=F32)


def _layer_norm(y, g, b):
    mu = jnp.mean(y, axis=-1, keepdims=True)
    yc = y - mu
    var = jnp.mean(yc * yc, axis=-1, keepdims=True)
    return yc * lax.rsqrt(var + LN_EPS) * g + b


def _rms_norm(y, g):
    return y * lax.rsqrt(jnp.mean(y * y, axis=-1, keepdims=True) + RMS_EPS) * g


def _params(*sem):
    return pltpu.CompilerParams(dimension_semantics=sem, vmem_limit_bytes=VMEM_LIMIT)


def _ffn_ln_body(x_ref, wg_ref, wu_ref, wd_ref, g_ref, b_ref, o_ref, *rest, cast_weights, n_sub):
    copies = rest[:-2]
    acc_ref, xb_ref = rest[-2:]
    i = pl.program_id(0)
    f = pl.program_id(1)
    nt = pl.num_programs(0) - 1
    cur = i % 2
    rows = o_ref.shape[0]

    @pl.when((i == 0) & (f == 0))
    def _():
        acc_ref[1] = jnp.zeros(acc_ref.shape[1:], F32)

    def norm_previous_tile():
        r0 = pl.multiple_of(jnp.minimum(f, n_sub - 1) * rows, rows)
        y = 0.5 * acc_ref[1 - cur, pl.ds(r0, rows), :]
        out = _layer_norm(y, g_ref[...], b_ref[...])
        o_ref[...] = out
        return jnp.max(out, axis=(0, 1), keepdims=True)

    @pl.when(i < nt)
    def _():
        @pl.when(f == 0)
        def _():
            x = x_ref[...]
            xb_ref[...] = x.astype(BF16)
            acc_ref[cur] = (2.0 * ALPHA) * x

        wg, wu, wd = wg_ref[...], wu_ref[...], wd_ref[...]
        if cast_weights:
            wg, wu, wd = wg.astype(BF16), wu.astype(BF16), wd.astype(BF16)
            for dst, w in zip(copies, (wg, wu, wd)):
                dst[...] = w

        norm_max = norm_previous_tile()
        xb = xb_ref[...]
        gate = _dot(xb, wg)
        up = _dot(xb, wu)
        act = gate * jax.nn.sigmoid(gate) * up
        act = jnp.where(norm_max > LARGEST, 0.0, act).astype(BF16)
        acc_ref[cur] += _dot(act, wd)

    @pl.when(i == nt)
    def _():
        norm_previous_tile()


def _ffn_ln(x, wg, wu, wd, g, b, *, tm, tf, cast_weights=False):
    n, d = x.shape
    dff = wg.shape[1]
    nf = dff // tf
    nt = n // tm
    assert n % tm == 0 and dff % tf == 0
    n_sub = 1
    while 2 * n_sub <= nf and tm % (2 * n_sub * SUBLANES) == 0:
        n_sub *= 2
    row_i = lambda i: jnp.minimum(i, nt - 1)
    col_f = lambda i, f: jnp.where(i < nt, f, nf - 1)
    in_specs = [
        pl.BlockSpec((tm, d), lambda i, f: (row_i(i), 0)),
        pl.BlockSpec((d, tf), lambda i, f: (0, col_f(i, f))),
        pl.BlockSpec((d, tf), lambda i, f: (0, col_f(i, f))),
        pl.BlockSpec((tf, d), lambda i, f: (col_f(i, f), 0)),
        pl.BlockSpec((1, d), lambda i, f: (0, 0)),
        pl.BlockSpec((1, d), lambda i, f: (0, 0)),
    ]
    out_specs = [pl.BlockSpec(
        (tm // n_sub, d),
        lambda i, f: (jnp.where(i == 0, 0, (i - 1) * n_sub + jnp.minimum(f, n_sub - 1)), 0))]
    out_shape = [jax.ShapeDtypeStruct((n, d), F32)]
    if cast_weights:
        out_specs += in_specs[1:4]
        out_shape += [jax.ShapeDtypeStruct(w.shape, BF16) for w in (wg, wu, wd)]
    return pl.pallas_call(
        functools.partial(_ffn_ln_body, cast_weights=cast_weights, n_sub=n_sub),
        grid=(nt + 1, nf),
        in_specs=in_specs,
        out_specs=out_specs,
        out_shape=out_shape,
        scratch_shapes=[pltpu.VMEM((2, tm, d), F32), pltpu.VMEM((tm, d), BF16)],
        compiler_params=_params("arbitrary", "arbitrary"),
        name="ffn_ln",
    )(x, wg, wu, wd, g, b)


def _proj_body(h_ref, win_ref, qg_ref, kvg_ref, wqn_ref, wqr_ref, wqs_ref, wkn_ref, wvt_ref,
               cos_ref, sin_ref, *rest, ql, kvl, dm, scale, n_side):
    side_in = rest[:n_side]
    q_ref, k_ref, vt_ref, xm_ref, z_ref = rest[n_side:n_side + 5]
    for src, dst in zip(side_in, rest[n_side + 5:]):
        dst[...] = src[...].astype(BF16)
    hb = h_ref[...].astype(BF16)
    u = _dot(hb, win_ref[...])
    o1 = ql
    o2 = o1 + kvl
    o3 = o2 + LANES
    o4 = o3 + LANES
    o5 = o4 + dm
    xm_ref[...] = u[:, o4:o5]
    z_ref[...] = u[:, o5:]
    c = cos_ref[...]
    s = sin_ref[...]
    qn = _rms_norm(u[:, :o1], qg_ref[...]).astype(BF16)
    qa = _dot(qn, wqn_ref[...])
    qr = _dot(qn, wqr_ref[...])
    qs = _dot(qn, wqs_ref[...])
    kvn = _rms_norm(u[:, o1:o2], kvg_ref[...]).astype(BF16)
    kn = _dot(kvn, wkn_ref[...])
    vt = _dot_nt(wvt_ref[...], kvn).astype(BF16)
    dv = vt.shape[0] // ATT_HEADS
    ones_rows = (lax.broadcasted_iota(jnp.int32, (V_PAD, vt.shape[1]), 0) == 0).astype(BF16)
    for h in range(ATT_HEADS):
        lo = h * (dv + V_PAD)
        vt_ref[lo:lo + dv, :] = vt[h * dv:(h + 1) * dv]
        vt_ref[lo + dv:lo + dv + V_PAD, :] = ones_rows
    k_rope = (u[:, o2:o3] * c + u[:, o3:o4] * s).astype(BF16)
    for h in range(ATT_HEADS):
        sl = slice(h * LANES, (h + 1) * LANES)
        lo = 2 * h * LANES
        q_ref[:, lo:lo + LANES] = (qa[:, sl] * scale).astype(BF16)
        q_ref[:, lo + LANES:lo + 2 * LANES] = ((qr[:, sl] * c + qs[:, sl] * s) * scale).astype(BF16)
        k_ref[:, lo:lo + LANES] = kn[:, sl].astype(BF16)
        k_ref[:, lo + LANES:lo + 2 * LANES] = k_rope


def _slice_spec(shape, steps):
    rows, cols = shape
    for csplit in (1, 2, 4, 8):
        rblocks = steps // csplit
        if (steps % csplit == 0 and rows % (rblocks * BF16_ROWS) == 0 and cols % (csplit * LANES) == 0):
            return pl.BlockSpec((rows // rblocks, cols // csplit), lambda i: (i // csplit, i % csplit))
    raise ValueError(f"cannot slice {shape} over {steps} steps")


def _proj(h, win, qg, kvg, wqn, wqr, wqs, wkn, wvt, ctab, stab, *, tm, dm, side=()):
    n, d = h.shape
    ql = qg.shape[1]
    kvl = kvg.shape[1]
    da = wvt.shape[0] + ATT_HEADS * V_PAD
    npos = ctab.shape[0] // tm
    hq = ATT_HEADS * 2 * LANES
    const = lambda i: (0, 0)
    scale = float((QK_NOPE + QK_ROPE) ** -0.5 * LOG2E)
    side_specs = [_slice_spec(w.shape, n // tm) for w in side]
    return pl.pallas_call(
        functools.partial(_proj_body, ql=ql, kvl=kvl, dm=dm, scale=scale, n_side=len(side)),
        grid=(n // tm,),
        in_specs=[
            pl.BlockSpec((tm, d), lambda i: (i, 0)),
            pl.BlockSpec(win.shape, const),
            pl.BlockSpec(qg.shape, const),
            pl.BlockSpec(kvg.shape, const),
            pl.BlockSpec(wqn.shape, const),
            pl.BlockSpec(wqr.shape, const),
            pl.BlockSpec(wqs.shape, const),
            pl.BlockSpec(wkn.shape, const),
            pl.BlockSpec(wvt.shape, const),
            pl.BlockSpec((tm, LANES), lambda i: (i % npos, 0)),
            pl.BlockSpec((tm, LANES), lambda i: (i % npos, 0)),
        ] + side_specs,
        out_specs=[
            pl.BlockSpec((tm, hq), lambda i: (i, 0)),
            pl.BlockSpec((tm, hq), lambda i: (i, 0)),
            pl.BlockSpec((da, tm), lambda i: (0, i)),
            pl.BlockSpec((tm, dm), lambda i: (i, 0)),
            pl.BlockSpec((tm, dm), lambda i: (i, 0)),
        ] + side_specs,
        out_shape=[
            jax.ShapeDtypeStruct((n, hq), BF16),
            jax.ShapeDtypeStruct((n, hq), BF16),
            jax.ShapeDtypeStruct((da, n), BF16),
            jax.ShapeDtypeStruct((n, dm), F32),
            jax.ShapeDtypeStruct((n, dm), F32),
        ] + [jax.ShapeDtypeStruct(w.shape, BF16) for w in side],
        compiler_params=_params("parallel"),
        name="proj",
    )(h, win, qg, kvg, wqn, wqr, wqs, wkn, wvt, ctab, stab, *side)


def _log_sigmoid(x):
    return jnp.minimum(x, 0.0) - jnp.log1p(jnp.exp(-jnp.abs(x)))


def _gate_scans(gi, lf, chunk):
    row = lax.broadcasted_iota(jnp.int32, (chunk, LANES), 0)
    lane = lax.broadcasted_iota(jnp.int32, (chunk, LANES), 1)
    j = lane % GATE_GROUP
    fwd = j < 3

    def scan(val, combine, identity):
        pre = val
        suf = val
        k = 1
        while k < chunk:
            pre = combine(pre, jnp.where(row >= k, pltpu.roll(pre, k, 0), identity))
            suf = combine(suf, jnp.where(row < chunk - k, pltpu.roll(suf, chunk - k, 0), identity))
            k *= 2
        return jnp.where(fwd, pre, suf)

    b = scan(lf, jnp.add, 0.0)
    r = gi - b
    cm = scan(r, jnp.maximum, LOWEST)
    return jnp.where(j % 3 == 0, b, jnp.where(j % 3 == 1, cm, r))


def _prep_body(prev_ref, x_ref, next_ref, mtail_ref, cw_ref, cb_ref, wk_ref, wv_ref, wqt_ref, wkt_ref,
               wvt_ref, wgc_ref, wgm_ref, bg_ref, *rest, tm, chunk, dh, meta):
    if meta:
        k_ref, v_ref, gc_ref, xs_ref = rest
    else:
        k_ref, qt_ref, kt_ref, vt_ref, xc_ref, gc_ref, gr_ref, xs_ref = rest
    i = pl.program_id(1)
    nt = pl.num_programs(1)
    x = x_ref[...]
    if meta:
        prev = jnp.zeros_like(prev_ref[...])
        nxt = next_ref[...]
    else:
        prev = jnp.where(i == 0, mtail_ref[...], prev_ref[...])
        nxt = jnp.where(i == nt - 1, 0.0, next_ref[...])
    xs_ref[0:SUBLANES, :] = prev
    xs_ref[SUBLANES:SUBLANES + tm, :] = x
    xs_ref[SUBLANES + tm:2 * SUBLANES + tm, :] = nxt
    acc = jnp.broadcast_to(cb_ref[...], x.shape)
    for t in range(CONV_K):
        off = SUBLANES - CONV_K // 2 + t
        acc = acc + cw_ref[t:t + 1, :] * xs_ref[off:off + tm, :]
    xc = acc * jax.nn.sigmoid(acc)
    xcb = xc.astype(BF16)
    xmb = x.astype(BF16)
    k_scale = dh ** -0.5
    if not meta:
        xc_ref[...] = xcb
        ones_rows = (lax.broadcasted_iota(jnp.int32, (V_PAD, tm), 0) == 0).astype(BF16)
    for h in range(MLSTM_HEADS):
        sl = slice(h * dh, (h + 1) * dh)
        k_ref[:, sl] = (_dot(xcb[:, sl], wk_ref[h]) * k_scale).astype(BF16)
        if meta:
            v_ref[:, sl] = _dot(xmb[:, sl], wv_ref[h]).astype(BF16)
        else:
            qt_ref[sl, :] = _dot_nt(wqt_ref[h], xcb[:, sl]).astype(BF16)
            kt_ref[sl, :] = (_dot_nt(wkt_ref[h], xcb[:, sl]) * k_scale).astype(BF16)
            lo = h * (dh + V_PAD)
            vt_ref[lo:lo + dh, :] = _dot_nt(wvt_ref[h], xmb[:, sl]).astype(BF16)
            vt_ref[lo + dh:lo + dh + V_PAD, :] = ones_rows
    g = _dot(xcb, wgc_ref[...]) + _dot(xmb, wgm_ref[...]) + bg_ref[...]
    gi = g[:, :LANES]
    lf = _log_sigmoid(g[:, LANES:])
    for ci in range(tm // chunk):
        rows = slice(ci * chunk, (ci + 1) * chunk)
        out = _gate_scans(gi[rows], lf[rows], chunk)
        gc_ref[rows, :] = out
        if not meta:
            gr_ref[ci] = out.T


def _prep(xm, xm_meta, cw, cb, wk, wv, wqt, wkt, wvt, wgc, wgm, bg, *, batch, tm, chunk, meta):
    dm = xm.shape[1]
    dh = dm // MLSTM_HEADS
    seq = xm.shape[0] // batch
    if meta:
        nt = 1
        n_out = batch * N_META
        x_arr = xm_meta
        x_spec = pl.BlockSpec((tm, dm), lambda b, i: (0, 0))
        prev_spec = pl.BlockSpec((SUBLANES, dm), lambda b, i: (0, 0))
        next_spec = pl.BlockSpec((SUBLANES, dm), lambda b, i: (b * (seq // SUBLANES), 0))
    else:
        nt = seq // tm
        n_out = batch * seq
        x_arr = xm
        last_blk = batch * seq // SUBLANES - 1
        x_spec = pl.BlockSpec((tm, dm), lambda b, i: (b * nt + i, 0))
        prev_spec = pl.BlockSpec(
            (SUBLANES, dm), lambda b, i: (jnp.maximum((b * nt + i) * (tm // SUBLANES) - 1, 0), 0))
        next_spec = pl.BlockSpec(
            (SUBLANES, dm), lambda b, i: (jnp.minimum((b * nt + i + 1) * (tm // SUBLANES), last_blk), 0))
    const2 = lambda b, i: (0, 0)
    const3 = lambda b, i: (0, 0, 0)
    row_spec = lambda w: pl.BlockSpec((tm, w), lambda b, i: (b * nt + i, 0))
    col_spec = lambda r: pl.BlockSpec((r, tm), lambda b, i: (0, b * nt + i))
    bf = lambda *shape: jax.ShapeDtypeStruct(shape, BF16)
    gc_shape = jax.ShapeDtypeStruct((n_out, LANES), F32)
    if meta:
        out_specs = [row_spec(dm), row_spec(dm), row_spec(LANES)]
        out_shape = [bf(n_out, dm), bf(n_out, dm), gc_shape]
    else:
        dva = dm + MLSTM_HEADS * V_PAD
        out_specs = [row_spec(dm), col_spec(dm), col_spec(dm), col_spec(dva), row_spec(dm), row_spec(LANES),
                     pl.BlockSpec((tm // chunk, LANES, chunk), lambda b, i: (b * nt + i, 0, 0))]
        out_shape = [bf(n_out, dm), bf(dm, n_out), bf(dm, n_out), bf(dva, n_out), bf(n_out, dm), gc_shape,
                     jax.ShapeDtypeStruct((n_out // chunk, LANES, chunk), F32)]
    return pl.pallas_call(
        functools.partial(_prep_body, tm=tm, chunk=chunk, dh=dh, meta=meta),
        grid=(batch, nt),
        in_specs=[
            prev_spec, x_spec, next_spec,
            pl.BlockSpec((SUBLANES, dm), lambda b, i: (1, 0)),
            pl.BlockSpec(cw.shape, const2),
            pl.BlockSpec(cb.shape, const2),
            pl.BlockSpec(wk.shape, const3),
            pl.BlockSpec(wv.shape, const3),
            pl.BlockSpec(wqt.shape, const3),
            pl.BlockSpec(wkt.shape, const3),
            pl.BlockSpec(wvt.shape, const3),
            pl.BlockSpec(wgc.shape, const2),
            pl.BlockSpec(wgm.shape, const2),
            pl.BlockSpec(bg.shape, const2),
        ],
        out_specs=out_specs,
        out_shape=out_shape,
        scratch_shapes=[pltpu.VMEM((tm + 2 * SUBLANES, dm), F32)],
        compiler_params=_params("parallel", "parallel"),
        name="mlstm_prep_meta" if meta else "mlstm_prep",
    )(xm, x_arr, xm, xm_meta, cw, cb, wk, wv, wqt, wkt, wvt, wgc, wgm, bg)


def _scan_chain(k, qt, kt, vta, r_col, rows, c_ref, m_ref, mask, edge, dh):
    b_row, cm_row, r_row = rows
    m_old = m_ref[...]
    ct = c_ref[...]

    mt = jnp.maximum(m_old, cm_row)
    at = jnp.exp(jnp.where(mask, r_col - mt, NEG)) * _dot(k, qt)
    inter = _dot(ct.astype(BF16), qt)
    w = jnp.exp(m_old - mt)
    den = jnp.sum(at, axis=0, keepdims=True) + w * inter[dh:dh + 1, :]
    scale = 1.0 / jnp.maximum(jnp.abs(den), jnp.exp(-(b_row + mt)))
    ht = _dot(vta[:dh], (at * scale).astype(BF16)) + inter[:dh] * (w * scale)

    b_end = b_row[:, edge:edge + 1]
    m_new = jnp.maximum(b_end + m_old, b_end + cm_row[:, edge:edge + 1])
    decay = jnp.exp(b_end + m_old - m_new)
    kwt = (kt.astype(F32) * jnp.exp(b_end + r_row - m_new)).astype(BF16)
    c_ref[...] = decay * ct + _dot_nt(vta, kwt)
    m_ref[...] = m_new
    return ht


def _scan_body(kf_ref, qtf_ref, ktf_ref, vtf_ref, gcf_ref, grf_ref,
               kb_ref, qtb_ref, ktb_ref, vtb_ref, gcb_ref, grb_ref,
               km_ref, vm_ref, gcm_ref, hf_ref, hb_ref, c_ref, m_ref, *, chunk, dh):
    ci = pl.program_id(1)
    nh = MLSTM_HEADS
    dva = dh + V_PAD

    @pl.when(ci == 0)
    def _():
        gcm = gcm_ref[0]
        km = km_ref[0]
        vm = vm_ref[0]
        first = lax.broadcasted_iota(jnp.int32, (V_PAD, dh), 0) == 0
        for h in range(nh):
            sl = slice(h * dh, (h + 1) * dh)
            lane0 = h * GATE_GROUP
            b_end = gcm[N_META - 1:N_META, lane0:lane0 + 1]
            m_new = jnp.maximum(b_end, b_end + gcm[N_META - 1:N_META, lane0 + 1:lane0 + 2])
            kw = km[:, sl].astype(F32) * jnp.exp(b_end + gcm[:, lane0 + 2:lane0 + 3] - m_new)
            c_ref[h, 0:dh, :] = _dot_tn(vm[:, sl], kw.astype(BF16))
            c_ref[h, dh:dva, :] = jnp.where(first, jnp.sum(kw, axis=0, keepdims=True), 0.0)
            m_ref[h] = m_new
            c_ref[nh + h] = jnp.zeros((dva, dh), F32)
            m_ref[nh + h] = jnp.zeros((1, 1), F32)

    srow = lax.broadcasted_iota(jnp.int32, (chunk, chunk), 0)
    tcol = lax.broadcasted_iota(jnp.int32, (chunk, chunk), 1)
    for h in range(nh):
        sl = slice(h * dh, (h + 1) * dh)
        sla = slice(h * dva, (h + 1) * dva)
        lane0 = h * GATE_GROUP
        rows_f = tuple(grf_ref[0, lane0 + j:lane0 + j + 1, :] for j in range(3))
        rows_b = tuple(grb_ref[0, lane0 + j:lane0 + j + 1, :] for j in range(3, 6))
        hf_ref[sl, :] = _scan_chain(
            kf_ref[:, sl], qtf_ref[sl, :], ktf_ref[sl, :], vtf_ref[sla, :], gcf_ref[:, lane0 + 2:lane0 + 3],
            rows_f, c_ref.at[h], m_ref.at[h], srow <= tcol, chunk - 1, dh).astype(hf_ref.dtype)
        hb_ref[sl, :] = _scan_chain(
            kb_ref[:, sl], qtb_ref[sl, :], ktb_ref[sl, :], vtb_ref[sla, :], gcb_ref[:, lane0 + 5:lane0 + 6],
            rows_b, c_ref.at[nh + h], m_ref.at[nh + h], srow >= tcol, 0, dh).astype(hb_ref.dtype)


def _scan(k, qt, kt, vta, gc, gr, km, vm, gcm, *, batch, chunk):
    n, dm = k.shape
    dh = dm // MLSTM_HEADS
    dva = vta.shape[0]
    nc = n // batch // chunk
    fwd = lambda b, c: (b * nc + c, 0)
    bwd = lambda b, c: (b * nc + nc - 1 - c, 0)
    fwd_t = lambda b, c: (0, b * nc + c)
    bwd_t = lambda b, c: (0, b * nc + nc - 1 - c)
    fwd3 = lambda b, c: (b * nc + c, 0, 0)
    bwd3 = lambda b, c: (b * nc + nc - 1 - c, 0, 0)
    per_b = lambda b, c: (b, 0, 0)

    def specs(idx, idx_t, idx3):
        return [pl.BlockSpec((chunk, dm), idx), pl.BlockSpec((dm, chunk), idx_t), pl.BlockSpec((dm, chunk), idx_t),
                pl.BlockSpec((dva, chunk), idx_t), pl.BlockSpec((chunk, LANES), idx),
                pl.BlockSpec((1, LANES, chunk), idx3)]

    return pl.pallas_call(
        functools.partial(_scan_body, chunk=chunk, dh=dh),
        grid=(batch, nc),
        in_specs=specs(fwd, fwd_t, fwd3) + specs(bwd, bwd_t, bwd3) + [
            pl.BlockSpec((1, N_META, dm), per_b),
            pl.BlockSpec((1, N_META, dm), per_b),
            pl.BlockSpec((1, N_META, LANES), per_b),
        ],
        out_specs=[pl.BlockSpec((dm, chunk), fwd_t), pl.BlockSpec((dm, chunk), bwd_t)],
        out_shape=[jax.ShapeDtypeStruct((dm, n), BF16)] * 2,
        scratch_shapes=[
            pltpu.VMEM((2 * MLSTM_HEADS, dh + V_PAD, dh), F32),
            pltpu.VMEM((2 * MLSTM_HEADS, 1, 1), F32),
        ],
        compiler_params=_params("parallel", "arbitrary"),
        name="mlstm_scan",
    )(k, qt, kt, vta, gc, gr, k, qt, kt, vta, gc, gr,
      km.reshape(batch, N_META, dm), vm.reshape(batch, N_META, dm), gcm.reshape(batch, N_META, LANES))


def _attn_body(q_ref, k_ref, vt_ref, km_ref, vmt_ref, o_ref, *, tk, nk):
    q = q_ref[...]
    dv = o_ref.shape[1]

    def scores(j):
        s = _dot_nt(k_ref[j * tk:(j + 1) * tk, :], q)
        return s, jnp.max(s, axis=0, keepdims=True)

    s_meta = _dot_nt(km_ref[...], q)
    ahead = scores(0)
    m = jnp.maximum(jnp.max(s_meta, axis=0, keepdims=True), ahead[1])
    acc = _dot(vmt_ref[...], jnp.exp2(s_meta - m).astype(BF16))
    for j in range(nk):
        s, s_max = ahead
        if j + 1 < nk:
            ahead = scores(j + 1)
        m_new = jnp.maximum(m, s_max)
        p = jnp.exp2(s - m_new).astype(BF16)
        acc = jnp.exp2(m - m_new) * acc + _dot(vt_ref[:, j * tk:(j + 1) * tk], p)
        m = m_new
    o_ref[...] = (acc[:dv] * (1.0 / acc[dv:dv + 1])).T


def _attn(q, k, vt, km, vmt, *, batch, tq, tk):
    n = q.shape[0]
    seq = n // batch
    nq = seq // tq
    dva = vt.shape[0] // ATT_HEADS
    dv = dva - V_PAD
    dk = 2 * LANES
    return pl.pallas_call(
        functools.partial(_attn_body, tk=tk, nk=seq // tk),
        grid=(batch, ATT_HEADS, nq),
        in_specs=[
            pl.BlockSpec((tq, dk), lambda b, h, i: (b * nq + i, h)),
            pl.BlockSpec((seq, dk), lambda b, h, i: (b, h)),
            pl.BlockSpec((dva, seq), lambda b, h, i: (h, b)),
            pl.BlockSpec((N_META, dk), lambda b, h, i: (0, h)),
            pl.BlockSpec((dva, N_META), lambda b, h, i: (h, 0)),
        ],
        out_specs=pl.BlockSpec((tq, dv), lambda b, h, i: (b * nq + i, h)),
        out_shape=jax.ShapeDtypeStruct((n, ATT_HEADS * dv), F32),
        compiler_params=_params("parallel", "parallel", "arbitrary"),
        name="attn",
    )(q, k, vt, km, vmt)


def _outproj_body(o_ref, hf_ref, hb_ref, z_ref, xc_ref, h1_ref, og_ref, gn_ref, sk_ref,
                  wa_ref, wm_ref, g_ref, b_ref, out_ref, *, dh, rows):
    for r in range(out_ref.shape[0] // rows):
        rs = slice(r * rows, (r + 1) * rows)
        y_att = _rms_norm(o_ref[rs, :], og_ref[...]).astype(BF16)
        h_sum = (hf_ref[:, rs].astype(F32) + hb_ref[:, rs].astype(F32)).T
        hs = h_sum * jax.nn.sigmoid(z_ref[rs, :])
        parts = []
        for h in range(MLSTM_HEADS):
            seg = hs[:, h * dh:(h + 1) * dh]
            mu = jnp.mean(seg, axis=-1, keepdims=True)
            sc = seg - mu
            var = jnp.mean(sc * sc, axis=-1, keepdims=True)
            parts.append(sc * lax.rsqrt(var + LN_EPS))
        hn = jnp.concatenate(parts, axis=-1)
        y_ml = (hn * gn_ref[...] + sk_ref[...] * xc_ref[rs, :].astype(F32)).astype(BF16)
        y = _dot(y_att, wa_ref[...]) + _dot(y_ml, wm_ref[...])
        out_ref[rs, :] = _layer_norm(ALPHA * h1_ref[rs, :] + y, g_ref[...], b_ref[...])


def _outproj(o, hf, hb, z, xc, h1, og, gn, sk, wo, g, b, *, tm, rows):
    n, d = h1.shape
    da = o.shape[1]
    dm = hf.shape[0]
    assert da == dm and wo.shape[0] == da + dm
    const = lambda i: (0, 0)
    row = lambda w: pl.BlockSpec((tm, w), lambda i: (i, 0))
    col = pl.BlockSpec((dm, tm), lambda i: (0, i))
    return pl.pallas_call(
        functools.partial(_outproj_body, dh=dm // MLSTM_HEADS, rows=rows),
        grid=(n // tm,),
        in_specs=[row(da), col, col, row(dm), row(dm), row(d),
                  pl.BlockSpec(og.shape, const), pl.BlockSpec(gn.shape, const), pl.BlockSpec(sk.shape, const),
                  pl.BlockSpec((da, d), lambda i: (0, 0), pipeline_mode=pl.Buffered(1)),
                  pl.BlockSpec((dm, d), lambda i: (1, 0), pipeline_mode=pl.Buffered(1)),
                  pl.BlockSpec(g.shape, const), pl.BlockSpec(b.shape, const)],
        out_specs=row(d),
        out_shape=jax.ShapeDtypeStruct((n, d), F32),
        compiler_params=_params("parallel"),
        name="outproj",
    )(o, hf, hb, z, xc, h1, og, gn, sk, wo, wo, g, b)


def _pad_lanes(w):
    return jnp.pad(w, [(0, 0)] * (w.ndim - 1) + [(0, LANES - w.shape[-1])])


def _pick(n, pref):
    for t in pref:
        if n % t == 0:
            return t
    raise ValueError(f"no tile for {n}")


def kernel(x, meta_tokens, ffn1_w_gate, ffn1_w_up, ffn1_w_down, ln1_g, ln1_b, w_in, mla_q_norm_g, mla_w_uq, mla_kv_norm_g, mla_w_ukv, attn_out_g, mlstm_conv_w, mlstm_conv_b, mlstm_w_q, mlstm_w_k, mlstm_w_v, mlstm_w_gates, mlstm_b_gates, mlstm_gn_g, mlstm_skip, w_out, ln2_g, ln2_b, ffn2_w_gate, ffn2_w_up, ffn2_w_down, ln3_g, ln3_b):
    batch, seq, d = x.shape
    assert ffn1_w_gate.shape[0] == DEPTH and meta_tokens.shape[0] == N_META
    n = batch * seq
    ql = mla_q_norm_g.shape[-1]
    kvl = mla_kv_norm_g.shape[-1]
    da = attn_out_g.shape[-1]
    dm = mlstm_gn_g.shape[-1]
    dh = dm // MLSTM_HEADS
    dv = da // ATT_HEADS
    nh = MLSTM_HEADS
    half = QK_ROPE // 2
    assert dv == LANES and dh % LANES == 0 and ql % LANES == 0 and kvl % LANES == 0
    assert seq % MLSTM_CHUNK == 0

    wi = w_in[0]
    o1, o2, o3, o4 = ql, ql + kvl, ql + kvl + QK_ROPE, ql + kvl + QK_ROPE + dm
    swap = jnp.concatenate([jnp.arange(half, QK_ROPE), jnp.arange(half)])
    w_kr = wi[:, o2:o3]
    win = jnp.concatenate(
        [wi[:, :o2], _pad_lanes(w_kr), _pad_lanes(w_kr[:, swap]), wi[:, o3:]], axis=1).astype(BF16)
    wuq = mla_w_uq[0].reshape(ql, ATT_HEADS, QK_NOPE + QK_ROPE)
    wqn = wuq[:, :, :QK_NOPE].reshape(ql, -1).astype(BF16)
    wqr = _pad_lanes(wuq[:, :, QK_NOPE:]).reshape(ql, -1).astype(BF16)
    wqs = _pad_lanes(wuq[:, :, QK_NOPE:][:, :, swap]).reshape(ql, -1).astype(BF16)
    wukv = mla_w_ukv[0].reshape(kvl, ATT_HEADS, QK_NOPE + dv)
    wkn = wukv[:, :, :QK_NOPE].reshape(kvl, -1).astype(BF16)
    wvt = wukv[:, :, QK_NOPE:].reshape(kvl, -1).T.astype(BF16)

    wg = mlstm_w_gates[0].reshape(2 * dm, 4, nh)
    bgr = mlstm_b_gates[0].reshape(4, nh)

    def gate_tile(w4, kind_f, kind_b):
        zeros = jnp.zeros_like(w4[..., 0, :])
        cols = [w4[..., kind_f, :]] * 3 + [w4[..., kind_b, :]] * 3 + [zeros] * (GATE_GROUP - 6)
        tile = jnp.stack(cols, axis=-1).reshape(*w4.shape[:-2], nh * GATE_GROUP)
        return _pad_lanes(tile)

    wgate = jnp.concatenate([gate_tile(wg, 0, 2), gate_tile(wg, 1, 3)], axis=-1)
    wgc = wgate[:dm].astype(BF16)
    wgm = wgate[dm:].astype(BF16)
    bg = jnp.concatenate([gate_tile(bgr, 0, 2), gate_tile(bgr, 1, 3)], axis=-1)[None, :]
    cw = jnp.pad(mlstm_conv_w[0], ((0, SUBLANES - CONV_K), (0, 0)))
    cb = mlstm_conv_b[0][None, :]
    wk_m, wv_m = (w[0].astype(BF16) for w in (mlstm_w_k, mlstm_w_v))
    wqt_m, wkt_m, wvt_m = (jnp.swapaxes(w[0], 1, 2).astype(BF16) for w in (mlstm_w_q, mlstm_w_k, mlstm_w_v))
    wo = w_out[0].astype(BF16)
    row = lambda p: p[0][None, :]

    pos = jnp.arange(N_META + seq, dtype=F32)
    inv = ROPE_BASE ** (-jnp.arange(0, QK_ROPE, 2, dtype=F32) / QK_ROPE)
    ang = pos[:, None] * inv[None, :]
    ctab = _pad_lanes(jnp.concatenate([jnp.cos(ang), jnp.cos(ang)], axis=-1))
    stab = _pad_lanes(jnp.concatenate([-jnp.sin(ang), jnp.sin(ang)], axis=-1))

    ffn1_f32 = [w[0] for w in (ffn1_w_gate, ffn1_w_up, ffn1_w_down)]
    ffn2_f32 = [w[0] for w in (ffn2_w_gate, ffn2_w_up, ffn2_w_down)]
    tf = _pick(ffn1_f32[0].shape[1], (512, 256, 128))
    tm_ffn = _pick(n, (1024, 512, 256))
    tm_proj = _pick(seq, (256,))

    xr = x.reshape(n, d)
    h1m, *ffn1 = _ffn_ln(meta_tokens.astype(x.dtype), *ffn1_f32, row(ln1_g), row(ln1_b),
                         tm=N_META, tf=tf, cast_weights=True)
    h1, = _ffn_ln(xr, *ffn1, row(ln1_g), row(ln1_b), tm=tm_ffn, tf=tf)
    proj_w = (win, row(mla_q_norm_g), row(mla_kv_norm_g), wqn, wqr, wqs, wkn, wvt)
    q_a, k_a, vt_a, xm, z, *ffn2 = _proj(
        h1, *proj_w, ctab[N_META:], stab[N_META:], tm=tm_proj, dm=dm, side=ffn2_f32)
    _, k_am, vt_am, xm_m, _ = _proj(h1m, *proj_w, ctab[:N_META], stab[:N_META], tm=N_META, dm=dm)

    prep_w = (cw, cb, wk_m, wv_m, wqt_m, wkt_m, wvt_m, wgc, wgm, bg)
    tm_prep = _pick(seq, (512, MLSTM_CHUNK))
    k_m, qt_m, kt_m, vt_m, xc, gc, gr = _prep(
        xm, xm_m, *prep_w, batch=batch, tm=tm_prep, chunk=MLSTM_CHUNK, meta=False)
    k_mm, v_mm, gc_m = _prep(xm, xm_m, *prep_w, batch=batch, tm=N_META, chunk=N_META, meta=True)
    hf, hb = _scan(k_m, qt_m, kt_m, vt_m, gc, gr, k_mm, v_mm, gc_m, batch=batch, chunk=MLSTM_CHUNK)

    o = _attn(q_a, k_a, vt_a, k_am, vt_am, batch=batch, tq=_pick(seq, (2048, 1024, 512, 256)), tk=_pick(seq, (512, 256)))

    h2 = _outproj(o, hf, hb, z, xc, h1, row(attn_out_g), row(mlstm_gn_g), row(mlstm_skip),
                  wo, row(ln2_g), row(ln2_b), tm=_pick(seq, (512, 256)), rows=_pick(seq, (256,)))
    out, = _ffn_ln(h2, *ffn2, row(ln3_g), row(ln3_b), tm=tm_ffn, tf=tf)
    return out.reshape(batch, seq, d)
```

```python
import functools

import jax
import jax.numpy as jnp
from jax import lax
from jax.experimental import pallas as pl
from jax.experimental.pallas import tpu as pltpu

F32 = jnp.float32
BF16 = jnp.bfloat16

N_META = 16
ATT_HEADS = 8
QK_NOPE = 128
QK_ROPE = 64
MLSTM_HEADS = 4
CONV_K = 5
ROPE_BASE = 10000.0
LN_EPS = 1e-5
RMS_EPS = 1e-6
DEPTH = 1
ALPHA = (2 * DEPTH) ** 0.25

LANES = 128
SUBLANES = 8
VMEM_LIMIT = 56 * 1024 * 1024

BF16_ROWS = 16
V_PAD = BF16_ROWS
MLSTM_CHUNK = 256
GATE_GROUP = 8
LOWEST = float(jnp.finfo(jnp.float32).min)
LARGEST = float(jnp.finfo(jnp.float32).max)
NEG = -1e30
LOG2E = 1.4426950408889634


def _dot(a, b):
    return jnp.dot(a, b, preferred_element_type=F32)


def _dot_nt(a, b):
    return lax.dot_general(a, b, (((1,), (1,)), ((), ())), preferred_element_type=F32)


def _dot_tn(a, b):
    return lax.dot_general(a, b, (((0,), (0,)), ((), ())), preferred_element_type=F32)


def _layer_norm(y, g, b):
    mu = jnp.mean(y, axis=-1, keepdims=True)
    yc = y - mu
    var = jnp.mean(yc * yc, axis=-1, keepdims=True)
    return yc * lax.rsqrt(var + LN_EPS) * g + b


def _rms_norm(y, g):
    return y * lax.rsqrt(jnp.mean(y * y, axis=-1, keepdims=True) + RMS_EPS) * g


def _params(*sem):
    return pltpu.CompilerParams(dimension_semantics=sem, vmem_limit_bytes=VMEM_LIMIT)


def _ffn_ln_body(x_ref, wg_ref, wu_ref, wd_ref, g_ref, b_ref, o_ref, *rest, cast_weights, n_sub):
    copies = rest[:-2]
    acc_ref, xb_ref = rest[-2:]
    i = pl.program_id(0)
    f = pl.program_id(1)
    nt = pl.num_programs(0) - 1
    cur = i % 2
    rows = o_ref.shape[0]

    @pl.when((i == 0) & (f == 0))
    def _():
        acc_ref[1] = jnp.zeros(acc_ref.shape[1:], F32)

    def norm_previous_tile():
        r0 = pl.multiple_of(jnp.minimum(f, n_sub - 1) * rows, rows)
        y = 0.5 * acc_ref[1 - cur, pl.ds(r0, rows), :]
        out = _layer_norm(y, g_ref[...], b_ref[...])
        o_ref[...] = out
        return jnp.max(out, axis=(0, 1), keepdims=True)

    @pl.when(i < nt)
    def _():
        @pl.when(f == 0)
        def _():
            x = x_ref[...]
            xb_ref[...] = x.astype(BF16)
            acc_ref[cur] = (2.0 * ALPHA) * x

        wg, wu, wd = wg_ref[...], wu_ref[...], wd_ref[...]
        if cast_weights:
            wg, wu, wd = wg.astype(BF16), wu.astype(BF16), wd.astype(BF16)
            for dst, w in zip(copies, (wg, wu, wd)):
                dst[...] = w

        norm_max = norm_previous_tile()
        xb = xb_ref[...]
        half = wg.shape[1] // 2
        for c in range(2):
            cs = slice(c * half, (c + 1) * half)
            gate = _dot(xb, wg[:, cs])
            up = _dot(xb, wu[:, cs])
            act = gate * jax.nn.sigmoid(gate) * up
            act = jnp.where(norm_max > LARGEST, 0.0, act).astype(BF16)
            acc_ref[cur] += _dot(act, wd[cs, :])

    @pl.when(i == nt)
    def _():
        norm_previous_tile()


def _ffn_ln(x, wg, wu, wd, g, b, *, tm, tf, cast_weights=False):
    n, d = x.shape
    dff = wg.shape[1]
    nf = dff // tf
    nt = n // tm
    assert n % tm == 0 and dff % tf == 0
    n_sub = 1
    while 2 * n_sub <= nf and tm % (2 * n_sub * SUBLANES) == 0:
        n_sub *= 2
    row_i = lambda i: jnp.minimum(i, nt - 1)
    col_f = lambda i, f: jnp.where(i < nt, f, nf - 1)
    in_specs = [
        pl.BlockSpec((tm, d), lambda i, f: (row_i(i), 0)),
        pl.BlockSpec((d, tf), lambda i, f: (0, col_f(i, f))),
        pl.BlockSpec((d, tf), lambda i, f: (0, col_f(i, f))),
        pl.BlockSpec((tf, d), lambda i, f: (col_f(i, f), 0)),
        pl.BlockSpec((1, d), lambda i, f: (0, 0)),
        pl.BlockSpec((1, d), lambda i, f: (0, 0)),
    ]
    out_specs = [pl.BlockSpec(
        (tm // n_sub, d),
        lambda i, f: (jnp.where(i == 0, 0, (i - 1) * n_sub + jnp.minimum(f, n_sub - 1)), 0))]
    out_shape = [jax.ShapeDtypeStruct((n, d), F32)]
    if cast_weights:
        out_specs += in_specs[1:4]
        out_shape += [jax.ShapeDtypeStruct(w.shape, BF16) for w in (wg, wu, wd)]
    return pl.pallas_call(
        functools.partial(_ffn_ln_body, cast_weights=cast_weights, n_sub=n_sub),
        grid=(nt + 1, nf),
        in_specs=in_specs,
        out_specs=out_specs,
        out_shape=out_shape,
        scratch_shapes=[pltpu.VMEM((2, tm, d), F32), pltpu.VMEM((tm, d), BF16)],
        compiler_params=_params("arbitrary", "arbitrary"),
        name="ffn_ln",
    )(x, wg, wu, wd, g, b)


def _proj_body(h_ref, win_ref, qg_ref, kvg_ref, wqn_ref, wqr_ref, wkn_ref, wvt_ref,
               rot_ref, *rest, ql, kvl, dm, scale, n_side):
    side_in = rest[:n_side]
    q_ref, k_ref, vt_ref, xm_ref, z_ref = rest[n_side:n_side + 5]
    for src, dst in zip(side_in, rest[n_side + 5:]):
        dst[...] = src[...].astype(BF16)
    hb = h_ref[...].astype(BF16)
    u = _dot(hb, win_ref[...])
    o1 = ql
    o2 = o1 + kvl
    o3 = o2 + LANES
    o4 = o3 + dm
    xm_ref[...] = u[:, o3:o4]
    z_ref[...] = u[:, o4:]
    rot = rot_ref[...]

    def rope(packed):
        t = packed * rot
        return t + pltpu.roll(t, LANES // 2, 1)

    lane = lax.broadcasted_iota(jnp.int32, (1, LANES), 1)
    q_rope_scale = jnp.where(lane < QK_ROPE, scale, 0.0)
    qn = _rms_norm(u[:, :o1], qg_ref[...]).astype(BF16)
    qa = _dot(qn, wqn_ref[...])
    qr = _dot(qn, wqr_ref[...])
    kvn = _rms_norm(u[:, o1:o2], kvg_ref[...]).astype(BF16)
    kn = _dot(kvn, wkn_ref[...])
    vt = _dot_nt(wvt_ref[...], kvn).astype(BF16)
    dv = vt.shape[0] // ATT_HEADS
    ones_rows = (lax.broadcasted_iota(jnp.int32, (V_PAD, vt.shape[1]), 0) == 0).astype(BF16)
    for h in range(ATT_HEADS):
        lo = h * (dv + V_PAD)
        vt_ref[lo:lo + dv, :] = vt[h * dv:(h + 1) * dv]
        vt_ref[lo + dv:lo + dv + V_PAD, :] = ones_rows
    k_rope = rope(u[:, o2:o3]).astype(BF16)
    for h in range(ATT_HEADS):
        sl = slice(h * LANES, (h + 1) * LANES)
        lo = 2 * h * LANES
        q_ref[:, lo:lo + LANES] = (qa[:, sl] * scale).astype(BF16)
        q_ref[:, lo + LANES:lo + 2 * LANES] = (rope(qr[:, sl]) * q_rope_scale).astype(BF16)
        k_ref[:, lo:lo + LANES] = kn[:, sl].astype(BF16)
        k_ref[:, lo + LANES:lo + 2 * LANES] = k_rope


def _slice_spec(shape, steps):
    rows, cols = shape
    for csplit in (1, 2, 4, 8):
        rblocks = steps // csplit
        if (steps % csplit == 0 and rows % (rblocks * BF16_ROWS) == 0 and cols % (csplit * LANES) == 0):
            return pl.BlockSpec((rows // rblocks, cols // csplit), lambda i: (i // csplit, i % csplit))
    raise ValueError(f"cannot slice {shape} over {steps} steps")


def _proj(h, win, qg, kvg, wqn, wqr, wkn, wvt, rot, *, tm, dm, side=()):
    n, d = h.shape
    ql = qg.shape[1]
    kvl = kvg.shape[1]
    da = wvt.shape[0] + ATT_HEADS * V_PAD
    npos = rot.shape[0] // tm
    hq = ATT_HEADS * 2 * LANES
    const = lambda i: (0, 0)
    scale = float((QK_NOPE + QK_ROPE) ** -0.5 * LOG2E)
    side_specs = [_slice_spec(w.shape, n // tm) for w in side]
    return pl.pallas_call(
        functools.partial(_proj_body, ql=ql, kvl=kvl, dm=dm, scale=scale, n_side=len(side)),
        grid=(n // tm,),
        in_specs=[
            pl.BlockSpec((tm, d), lambda i: (i, 0)),
            pl.BlockSpec(win.shape, const),
            pl.BlockSpec(qg.shape, const),
            pl.BlockSpec(kvg.shape, const),
            pl.BlockSpec(wqn.shape, const),
            pl.BlockSpec(wqr.shape, const),
            pl.BlockSpec(wkn.shape, const),
            pl.BlockSpec(wvt.shape, const),
            pl.BlockSpec((tm, LANES), lambda i: (i % npos, 0)),
        ] + side_specs,
        out_specs=[
            pl.BlockSpec((tm, hq), lambda i: (i, 0)),
            pl.BlockSpec((tm, hq), lambda i: (i, 0)),
            pl.BlockSpec((da, tm), lambda i: (0, i)),
            pl.BlockSpec((tm, dm), lambda i: (i, 0)),
            pl.BlockSpec((tm, dm), lambda i: (i, 0)),
        ] + side_specs,
        out_shape=[
            jax.ShapeDtypeStruct((n, hq), BF16),
            jax.ShapeDtypeStruct((n, hq), BF16),
            jax.ShapeDtypeStruct((da, n), BF16),
            jax.ShapeDtypeStruct((n, dm), F32),
            jax.ShapeDtypeStruct((n, dm), F32),
        ] + [jax.ShapeDtypeStruct(w.shape, BF16) for w in side],
        compiler_params=_params("parallel"),
        name="proj",
    )(h, win, qg, kvg, wqn, wqr, wkn, wvt, rot, *side)


def _log_sigmoid(x):
    return jnp.minimum(x, 0.0) - jnp.log1p(jnp.exp(-jnp.abs(x)))


def _gate_scans(gi, lf, chunk):
    row = lax.broadcasted_iota(jnp.int32, (chunk, LANES), 0)
    lane = lax.broadcasted_iota(jnp.int32, (chunk, LANES), 1)
    j = lane % GATE_GROUP
    fwd = j < 3

    def scan(val, combine, identity):
        pre = val
        suf = val
        k = 1
        while k < chunk:
            pre = combine(pre, jnp.where(row >= k, pltpu.roll(pre, k, 0), identity))
            suf = combine(suf, jnp.where(row < chunk - k, pltpu.roll(suf, chunk - k, 0), identity))
            k *= 2
        return jnp.where(fwd, pre, suf)

    b = scan(lf, jnp.add, 0.0)
    r = gi - b
    cm = scan(r, jnp.maximum, LOWEST)
    return jnp.where(j % 3 == 0, b, jnp.where(j % 3 == 1, cm, r))


def _prep_body(prev_ref, x_ref, next_ref, mtail_ref, cw_ref, cb_ref, wk_ref, wv_ref, wqt_ref, wkt_ref,
               wvt_ref, wgc_ref, wgm_ref, bg_ref, *rest, tm, chunk, dh, meta):
    if meta:
        k_ref, v_ref, gc_ref, xs_ref = rest
    else:
        k_ref, qt_ref, kt_ref, vt_ref, xc_ref, gc_ref, gr_ref, xs_ref = rest
    i = pl.program_id(1)
    nt = pl.num_programs(1)
    x = x_ref[...]
    if meta:
        prev = jnp.zeros_like(prev_ref[...])
        nxt = next_ref[...]
    else:
        prev = jnp.where(i == 0, mtail_ref[...], prev_ref[...])
        nxt = jnp.where(i == nt - 1, 0.0, next_ref[...])
    xs_ref[0:SUBLANES, :] = prev
    xs_ref[SUBLANES:SUBLANES + tm, :] = x
    xs_ref[SUBLANES + tm:2 * SUBLANES + tm, :] = nxt
    acc = jnp.broadcast_to(cb_ref[...], x.shape)
    for t in range(CONV_K):
        off = SUBLANES - CONV_K // 2 + t
        acc = acc + cw_ref[t:t + 1, :] * xs_ref[off:off + tm, :]
    xc = acc * jax.nn.sigmoid(acc)
    xcb = xc.astype(BF16)
    xmb = x.astype(BF16)
    k_scale = dh ** -0.5
    if not meta:
        xc_ref[...] = xcb
        ones_rows = (lax.broadcasted_iota(jnp.int32, (V_PAD, tm), 0) == 0).astype(BF16)
    for h in range(MLSTM_HEADS):
        sl = slice(h * dh, (h + 1) * dh)
        k_ref[:, sl] = (_dot(xcb[:, sl], wk_ref[h]) * k_scale).astype(BF16)
        if meta:
            v_ref[:, sl] = _dot(xmb[:, sl], wv_ref[h]).astype(BF16)
        else:
            qt_ref[sl, :] = _dot_nt(wqt_ref[h], xcb[:, sl]).astype(BF16)
            kt_ref[sl, :] = (_dot_nt(wkt_ref[h], xcb[:, sl]) * k_scale).astype(BF16)
            lo = h * (dh + V_PAD)
            vt_ref[lo:lo + dh, :] = _dot_nt(wvt_ref[h], xmb[:, sl]).astype(BF16)
            vt_ref[lo + dh:lo + dh + V_PAD, :] = ones_rows
    g = _dot(xcb, wgc_ref[...]) + _dot(xmb, wgm_ref[...]) + bg_ref[...]
    gi = g[:, :LANES]
    lf = _log_sigmoid(g[:, LANES:])
    for ci in range(tm // chunk):
        rows = slice(ci * chunk, (ci + 1) * chunk)
        out = _gate_scans(gi[rows], lf[rows], chunk)
        gc_ref[rows, :] = out
        if not meta:
            gr_ref[ci] = out.T


def _prep(xm, xm_meta, cw, cb, wk, wv, wqt, wkt, wvt, wgc, wgm, bg, *, batch, tm, chunk, meta):
    dm = xm.shape[1]
    dh = dm // MLSTM_HEADS
    seq = xm.shape[0] // batch
    if meta:
        nt = 1
        n_out = batch * N_META
        x_arr = xm_meta
        x_spec = pl.BlockSpec((tm, dm), lambda b, i: (0, 0))
        prev_spec = pl.BlockSpec((SUBLANES, dm), lambda b, i: (0, 0))
        next_spec = pl.BlockSpec((SUBLANES, dm), lambda b, i: (b * (seq // SUBLANES), 0))
    else:
        nt = seq // tm
        n_out = batch * seq
        x_arr = xm
        last_blk = batch * seq // SUBLANES - 1
        x_spec = pl.BlockSpec((tm, dm), lambda b, i: (b * nt + i, 0))
        prev_spec = pl.BlockSpec(
            (SUBLANES, dm), lambda b, i: (jnp.maximum((b * nt + i) * (tm // SUBLANES) - 1, 0), 0))
        next_spec = pl.BlockSpec(
            (SUBLANES, dm), lambda b, i: (jnp.minimum((b * nt + i + 1) * (tm // SUBLANES), last_blk), 0))
    const2 = lambda b, i: (0, 0)
    const3 = lambda b, i: (0, 0, 0)
    row_spec = lambda w: pl.BlockSpec((tm, w), lambda b, i: (b * nt + i, 0))
    col_spec = lambda r: pl.BlockSpec((r, tm), lambda b, i: (0, b * nt + i))
    bf = lambda *shape: jax.ShapeDtypeStruct(shape, BF16)
    gc_shape = jax.ShapeDtypeStruct((n_out, LANES), F32)
    if meta:
        out_specs = [row_spec(dm), row_spec(dm), row_spec(LANES)]
        out_shape = [bf(n_out, dm), bf(n_out, dm), gc_shape]
    else:
        dva = dm + MLSTM_HEADS * V_PAD
        out_specs = [row_spec(dm), col_spec(dm), col_spec(dm), col_spec(dva), row_spec(dm), row_spec(LANES),
                     pl.BlockSpec((tm // chunk, LANES, chunk), lambda b, i: (b * nt + i, 0, 0))]
        out_shape = [bf(n_out, dm), bf(dm, n_out), bf(dm, n_out), bf(dva, n_out), bf(n_out, dm), gc_shape,
                     jax.ShapeDtypeStruct((n_out // chunk, LANES, chunk), F32)]
    return pl.pallas_call(
        functools.partial(_prep_body, tm=tm, chunk=chunk, dh=dh, meta=meta),
        grid=(batch, nt),
        in_specs=[
            prev_spec, x_spec, next_spec,
            pl.BlockSpec((SUBLANES, dm), lambda b, i: (1, 0)),
            pl.BlockSpec(cw.shape, const2),
            pl.BlockSpec(cb.shape, const2),
            pl.BlockSpec(wk.shape, const3),
            pl.BlockSpec(wv.shape, const3),
            pl.BlockSpec(wqt.shape, const3),
            pl.BlockSpec(wkt.shape, const3),
            pl.BlockSpec(wvt.shape, const3),
            pl.BlockSpec(wgc.shape, const2),
            pl.BlockSpec(wgm.shape, const2),
            pl.BlockSpec(bg.shape, const2),
        ],
        out_specs=out_specs,
        out_shape=out_shape,
        scratch_shapes=[pltpu.VMEM((tm + 2 * SUBLANES, dm), F32)],
        compiler_params=_params("parallel", "parallel"),
        name="mlstm_prep_meta" if meta else "mlstm_prep",
    )(xm, x_arr, xm, xm_meta, cw, cb, wk, wv, wqt, wkt, wvt, wgc, wgm, bg)


def _scan_chain(k, qt, kt, vta, r_col, rows, c_ref, m_ref, mask, edge, dh):
    b_row, cm_row, r_row = rows
    m_old = m_ref[...]
    ct = c_ref[...]

    mt = jnp.maximum(m_old, cm_row)
    at = jnp.exp(jnp.where(mask, r_col - mt, NEG)) * _dot(k, qt)
    inter = _dot(ct.astype(BF16), qt)
    w = jnp.exp(m_old - mt)
    den = jnp.sum(at, axis=0, keepdims=True) + w * inter[dh:dh + 1, :]
    scale = 1.0 / jnp.maximum(jnp.abs(den), jnp.exp(-(b_row + mt)))
    ht = _dot(vta[:dh], (at * scale).astype(BF16)) + inter[:dh] * (w * scale)

    b_end = b_row[:, edge:edge + 1]
    m_new = jnp.maximum(b_end + m_old, b_end + cm_row[:, edge:edge + 1])
    decay = jnp.exp(b_end + m_old - m_new)
    kwt = (kt.astype(F32) * jnp.exp(b_end + r_row - m_new)).astype(BF16)
    c_ref[...] = decay * ct + _dot_nt(vta, kwt)
    m_ref[...] = m_new
    return ht


def _scan_body(kf_ref, qtf_ref, ktf_ref, vtf_ref, gcf_ref, grf_ref,
               kb_ref, qtb_ref, ktb_ref, vtb_ref, gcb_ref, grb_ref,
               km_ref, vm_ref, gcm_ref, hf_ref, hb_ref, c_ref, m_ref, *, chunk, dh):
    ci = pl.program_id(1)
    nh = MLSTM_HEADS
    dva = dh + V_PAD

    @pl.when(ci == 0)
    def _():
        gcm = gcm_ref[0]
        km = km_ref[0]
        vm = vm_ref[0]
        first = lax.broadcasted_iota(jnp.int32, (V_PAD, dh), 0) == 0
        for h in range(nh):
            sl = slice(h * dh, (h + 1) * dh)
            lane0 = h * GATE_GROUP
            b_end = gcm[N_META - 1:N_META, lane0:lane0 + 1]
            m_new = jnp.maximum(b_end, b_end + gcm[N_META - 1:N_META, lane0 + 1:lane0 + 2])
            kw = km[:, sl].astype(F32) * jnp.exp(b_end + gcm[:, lane0 + 2:lane0 + 3] - m_new)
            c_ref[h, 0:dh, :] = _dot_tn(vm[:, sl], kw.astype(BF16))
            c_ref[h, dh:dva, :] = jnp.where(first, jnp.sum(kw, axis=0, keepdims=True), 0.0)
            m_ref[h] = m_new
            c_ref[nh + h] = jnp.zeros((dva, dh), F32)
            m_ref[nh + h] = jnp.zeros((1, 1), F32)

    srow = lax.broadcasted_iota(jnp.int32, (chunk, chunk), 0)
    tcol = lax.broadcasted_iota(jnp.int32, (chunk, chunk), 1)
    for h in range(nh):
        sl = slice(h * dh, (h + 1) * dh)
        sla = slice(h * dva, (h + 1) * dva)
        lane0 = h * GATE_GROUP
        rows_f = tuple(grf_ref[0, lane0 + j:lane0 + j + 1, :] for j in range(3))
        rows_b = tuple(grb_ref[0, lane0 + j:lane0 + j + 1, :] for j in range(3, 6))
        hf_ref[sl, :] = _scan_chain(
            kf_ref[:, sl], qtf_ref[sl, :], ktf_ref[sl, :], vtf_ref[sla, :], gcf_ref[:, lane0 + 2:lane0 + 3],
            rows_f, c_ref.at[h], m_ref.at[h], srow <= tcol, chunk - 1, dh).astype(hf_ref.dtype)
        hb_ref[sl, :] = _scan_chain(
            kb_ref[:, sl], qtb_ref[sl, :], ktb_ref[sl, :], vtb_ref[sla, :], gcb_ref[:, lane0 + 5:lane0 + 6],
            rows_b, c_ref.at[nh + h], m_ref.at[nh + h], srow >= tcol, 0, dh).astype(hb_ref.dtype)


def _scan(k, qt, kt, vta, gc, gr, km, vm, gcm, *, batch, chunk):
    n, dm = k.shape
    dh = dm // MLSTM_HEADS
    dva = vta.shape[0]
    nc = n // batch // chunk
    fwd = lambda b, c: (b * nc + c, 0)
    bwd = lambda b, c: (b * nc + nc - 1 - c, 0)
    fwd_t = lambda b, c: (0, b * nc + c)
    bwd_t = lambda b, c: (0, b * nc + nc - 1 - c)
    fwd3 = lambda b, c: (b * nc + c, 0, 0)
    bwd3 = lambda b, c: (b * nc + nc - 1 - c, 0, 0)
    per_b = lambda b, c: (b, 0, 0)

    def specs(idx, idx_t, idx3):
        return [pl.BlockSpec((chunk, dm), idx), pl.BlockSpec((dm, chunk), idx_t), pl.BlockSpec((dm, chunk), idx_t),
                pl.BlockSpec((dva, chunk), idx_t), pl.BlockSpec((chunk, LANES), idx),
                pl.BlockSpec((1, LANES, chunk), idx3)]

    return pl.pallas_call(
        functools.partial(_scan_body, chunk=chunk, dh=dh),
        grid=(batch, nc),
        in_specs=specs(fwd, fwd_t, fwd3) + specs(bwd, bwd_t, bwd3) + [
            pl.BlockSpec((1, N_META, dm), per_b),
            pl.BlockSpec((1, N_META, dm), per_b),
            pl.BlockSpec((1, N_META, LANES), per_b),
        ],
        out_specs=[pl.BlockSpec((dm, chunk), fwd_t), pl.BlockSpec((dm, chunk), bwd_t)],
        out_shape=[jax.ShapeDtypeStruct((dm, n), BF16)] * 2,
        scratch_shapes=[
            pltpu.VMEM((2 * MLSTM_HEADS, dh + V_PAD, dh), F32),
            pltpu.VMEM((2 * MLSTM_HEADS, 1, 1), F32),
        ],
        compiler_params=_params("parallel", "arbitrary"),
        name="mlstm_scan",
    )(k, qt, kt, vta, gc, gr, k, qt, kt, vta, gc, gr,
      km.reshape(batch, N_META, dm), vm.reshape(batch, N_META, dm), gcm.reshape(batch, N_META, LANES))


def _attn_body(q_ref, k_ref, vt_ref, km_ref, vmt_ref, o_ref, *, tk, nk):
    q = q_ref[...]
    dv = o_ref.shape[1]

    def scores(j):
        s = _dot_nt(k_ref[j * tk:(j + 1) * tk, :], q)
        return s, jnp.max(s, axis=0, keepdims=True)

    s_meta = _dot_nt(km_ref[...], q)
    ahead = scores(0)
    m = jnp.maximum(jnp.max(s_meta, axis=0, keepdims=True), ahead[1])
    acc = _dot(vmt_ref[...], jnp.exp2(s_meta - m).astype(BF16))
    for j in range(nk):
        s, s_max = ahead
        if j + 1 < nk:
            ahead = scores(j + 1)
        m_new = jnp.maximum(m, s_max)
        p = jnp.exp2(s - m_new).astype(BF16)
        acc = jnp.exp2(m - m_new) * acc + _dot(vt_ref[:, j * tk:(j + 1) * tk], p)
        m = m_new
    o_ref[...] = (acc[:dv] * (1.0 / acc[dv:dv + 1])).T


def _attn(q, k, vt, km, vmt, *, batch, tq, tk):
    n = q.shape[0]
    seq = n // batch
    nq = seq // tq
    dva = vt.shape[0] // ATT_HEADS
    dv = dva - V_PAD
    dk = 2 * LANES
    return pl.pallas_call(
        functools.partial(_attn_body, tk=tk, nk=seq // tk),
        grid=(batch, ATT_HEADS, nq),
        in_specs=[
            pl.BlockSpec((tq, dk), lambda b, h, i: (b * nq + i, h)),
            pl.BlockSpec((seq, dk), lambda b, h, i: (b, h)),
            pl.BlockSpec((dva, seq), lambda b, h, i: (h, b)),
            pl.BlockSpec((N_META, dk), lambda b, h, i: (0, h)),
            pl.BlockSpec((dva, N_META), lambda b, h, i: (h, 0)),
        ],
        out_specs=pl.BlockSpec((tq, dv), lambda b, h, i: (b * nq + i, h)),
        out_shape=jax.ShapeDtypeStruct((n, ATT_HEADS * dv), F32),
        compiler_params=_params("parallel", "parallel", "arbitrary"),
        name="attn",
    )(q, k, vt, km, vmt)


def _outproj_body(o_ref, hf_ref, hb_ref, z_ref, xc_ref, h1_ref, og_ref, gn_ref, sk_ref,
                  wa_ref, wm_ref, g_ref, b_ref, out_ref, *, dh, rows):
    for r in range(out_ref.shape[0] // rows):
        rs = slice(r * rows, (r + 1) * rows)
        y_att = _rms_norm(o_ref[rs, :], og_ref[...]).astype(BF16)
        h_sum = (hf_ref[:, rs].astype(F32) + hb_ref[:, rs].astype(F32)).T
        hs = h_sum * jax.nn.sigmoid(z_ref[rs, :])
        parts = []
        for h in range(MLSTM_HEADS):
            seg = hs[:, h * dh:(h + 1) * dh]
            mu = jnp.mean(seg, axis=-1, keepdims=True)
            sc = seg - mu
            var = jnp.mean(sc * sc, axis=-1, keepdims=True)
            parts.append(sc * lax.rsqrt(var + LN_EPS))
        hn = jnp.concatenate(parts, axis=-1)
        y_ml = (hn * gn_ref[...] + sk_ref[...] * xc_ref[rs, :].astype(F32)).astype(BF16)
        y = _dot(y_att, wa_ref[...]) + _dot(y_ml, wm_ref[...])
        out_ref[rs, :] = _layer_norm(ALPHA * h1_ref[rs, :] + y, g_ref[...], b_ref[...])


def _outproj(o, hf, hb, z, xc, h1, og, gn, sk, wo, g, b, *, tm, rows):
    n, d = h1.shape
    da = o.shape[1]
    dm = hf.shape[0]
    assert da == dm and wo.shape[0] == da + dm
    const = lambda i: (0, 0)
    row = lambda w: pl.BlockSpec((tm, w), lambda i: (i, 0))
    col = pl.BlockSpec((dm, tm), lambda i: (0, i))
    return pl.pallas_call(
        functools.partial(_outproj_body, dh=dm // MLSTM_HEADS, rows=rows),
        grid=(n // tm,),
        in_specs=[row(da), col, col, row(dm), row(dm), row(d),
                  pl.BlockSpec(og.shape, const), pl.BlockSpec(gn.shape, const), pl.BlockSpec(sk.shape, const),
                  pl.BlockSpec((da, d), lambda i: (0, 0), pipeline_mode=pl.Buffered(1)),
                  pl.BlockSpec((dm, d), lambda i: (1, 0), pipeline_mode=pl.Buffered(1)),
                  pl.BlockSpec(g.shape, const), pl.BlockSpec(b.shape, const)],
        out_specs=row(d),
        out_shape=jax.ShapeDtypeStruct((n, d), F32),
        compiler_params=_params("parallel"),
        name="outproj",
    )(o, hf, hb, z, xc, h1, og, gn, sk, wo, wo, g, b)


def _pad_lanes(w):
    return jnp.pad(w, [(0, 0)] * (w.ndim - 1) + [(0, LANES - w.shape[-1])])


def _pick(n, pref):
    for t in pref:
        if n % t == 0:
            return t
    raise ValueError(f"no tile for {n}")


def kernel(x, meta_tokens, ffn1_w_gate, ffn1_w_up, ffn1_w_down, ln1_g, ln1_b, w_in, mla_q_norm_g, mla_w_uq, mla_kv_norm_g, mla_w_ukv, attn_out_g, mlstm_conv_w, mlstm_conv_b, mlstm_w_q, mlstm_w_k, mlstm_w_v, mlstm_w_gates, mlstm_b_gates, mlstm_gn_g, mlstm_skip, w_out, ln2_g, ln2_b, ffn2_w_gate, ffn2_w_up, ffn2_w_down, ln3_g, ln3_b):
    batch, seq, d = x.shape
    assert ffn1_w_gate.shape[0] == DEPTH and meta_tokens.shape[0] == N_META
    n = batch * seq
    ql = mla_q_norm_g.shape[-1]
    kvl = mla_kv_norm_g.shape[-1]
    da = attn_out_g.shape[-1]
    dm = mlstm_gn_g.shape[-1]
    dh = dm // MLSTM_HEADS
    dv = da // ATT_HEADS
    nh = MLSTM_HEADS
    half = QK_ROPE // 2
    assert dv == LANES and dh % LANES == 0 and ql % LANES == 0 and kvl % LANES == 0
    assert seq % MLSTM_CHUNK == 0

    wi = w_in[0]
    o1, o2, o3, o4 = ql, ql + kvl, ql + kvl + QK_ROPE, ql + kvl + QK_ROPE + dm
    swap = jnp.concatenate([jnp.arange(half, QK_ROPE), jnp.arange(half)])
    w_kr = wi[:, o2:o3]
    win = jnp.concatenate([wi[:, :o2], w_kr, w_kr[:, swap], wi[:, o3:]], axis=1).astype(BF16)
    wuq = mla_w_uq[0].reshape(ql, ATT_HEADS, QK_NOPE + QK_ROPE)
    wqn = wuq[:, :, :QK_NOPE].reshape(ql, -1).astype(BF16)
    w_qr = wuq[:, :, QK_NOPE:]
    wqr = jnp.concatenate([w_qr, w_qr[:, :, swap]], axis=-1).reshape(ql, -1).astype(BF16)
    wukv = mla_w_ukv[0].reshape(kvl, ATT_HEADS, QK_NOPE + dv)
    wkn = wukv[:, :, :QK_NOPE].reshape(kvl, -1).astype(BF16)
    wvt = wukv[:, :, QK_NOPE:].reshape(kvl, -1).T.astype(BF16)

    wg = mlstm_w_gates[0].reshape(2 * dm, 4, nh)
    bgr = mlstm_b_gates[0].reshape(4, nh)

    def gate_tile(w4, kind_f, kind_b):
        zeros = jnp.zeros_like(w4[..., 0, :])
        cols = [w4[..., kind_f, :]] * 3 + [w4[..., kind_b, :]] * 3 + [zeros] * (GATE_GROUP - 6)
        tile = jnp.stack(cols, axis=-1).reshape(*w4.shape[:-2], nh * GATE_GROUP)
        return _pad_lanes(tile)

    wgate = jnp.concatenate([gate_tile(wg, 0, 2), gate_tile(wg, 1, 3)], axis=-1)
    wgc = wgate[:dm].astype(BF16)
    wgm = wgate[dm:].astype(BF16)
    bg = jnp.concatenate([gate_tile(bgr, 0, 2), gate_tile(bgr, 1, 3)], axis=-1)[None, :]
    cw = jnp.pad(mlstm_conv_w[0], ((0, SUBLANES - CONV_K), (0, 0)))
    cb = mlstm_conv_b[0][None, :]
    wk_m, wv_m = (w[0].astype(BF16) for w in (mlstm_w_k, mlstm_w_v))
    wqt_m, wkt_m, wvt_m = (jnp.swapaxes(w[0], 1, 2).astype(BF16) for w in (mlstm_w_q, mlstm_w_k, mlstm_w_v))
    wo = w_out[0].astype(BF16)
    row = lambda p: p[0][None, :]

    pos = jnp.arange(N_META + seq, dtype=F32)
    inv = ROPE_BASE ** (-jnp.arange(0, QK_ROPE, 2, dtype=F32) / QK_ROPE)
    ang = pos[:, None] * inv[None, :]
    rot = jnp.concatenate([jnp.cos(ang), jnp.cos(ang), -jnp.sin(ang), jnp.sin(ang)], axis=-1)

    ffn1_f32 = [w[0] for w in (ffn1_w_gate, ffn1_w_up, ffn1_w_down)]
    ffn2_f32 = [w[0] for w in (ffn2_w_gate, ffn2_w_up, ffn2_w_down)]
    tf = _pick(ffn1_f32[0].shape[1], (512, 256, 128))
    tm_ffn = _pick(n, (1024, 512, 256))
    tm_proj = _pick(seq, (256,))

    xr = x.reshape(n, d)
    h1m, *ffn1 = _ffn_ln(meta_tokens.astype(x.dtype), *ffn1_f32, row(ln1_g), row(ln1_b),
                         tm=N_META, tf=tf, cast_weights=True)
    h1, = _ffn_ln(xr, *ffn1, row(ln1_g), row(ln1_b), tm=tm_ffn, tf=tf)
    proj_w = (win, row(mla_q_norm_g), row(mla_kv_norm_g), wqn, wqr, wkn, wvt)
    q_a, k_a, vt_a, xm, z, *ffn2 = _proj(h1, *proj_w, rot[N_META:], tm=tm_proj, dm=dm, side=ffn2_f32)
    _, k_am, vt_am, xm_m, _ = _proj(h1m, *proj_w, rot[:N_META], tm=N_META, dm=dm)

    prep_w = (cw, cb, wk_m, wv_m, wqt_m, wkt_m, wvt_m, wgc, wgm, bg)
    tm_prep = _pick(seq, (512, MLSTM_CHUNK))
    k_m, qt_m, kt_m, vt_m, xc, gc, gr = _prep(
        xm, xm_m, *prep_w, batch=batch, tm=tm_prep, chunk=MLSTM_CHUNK, meta=False)
    k_mm, v_mm, gc_m = _prep(xm, xm_m, *prep_w, batch=batch, tm=N_META, chunk=N_META, meta=True)
    hf, hb = _scan(k_m, qt_m, kt_m, vt_m, gc, gr, k_mm, v_mm, gc_m, batch=batch, chunk=MLSTM_CHUNK)

    o = _attn(q_a, k_a, vt_a, k_am, vt_am, batch=batch, tq=_pick(seq, (2048, 1024, 512, 256)), tk=_pick(seq, (512, 256)))

    h2 = _outproj(o, hf, hb, z, xc, h1, row(attn_out_g), row(mlstm_gn_g), row(mlstm_skip),
                  wo, row(ln2_g), row(ln2_b), tm=_pick(seq, (512, 256)), rows=_pick(seq, (256,)))
    out, = _ffn_ln(h2, *ffn2, row(ln3_g), row(ln3_b), tm=tm_ffn, tf=tf)
    return out.reshape(batch, seq, d)
```

```python
import functools

import jax
import jax.numpy as jnp
from jax import lax
from jax.experimental import pallas as pl
from jax.experimental.pallas import tpu as pltpu

F32 = jnp.float32
BF16 = jnp.bfloat16

N_META = 16
ATT_HEADS = 8
QK_NOPE = 128
QK_ROPE = 64
MLSTM_HEADS = 4
CONV_K = 5
ROPE_BASE = 10000.0
LN_EPS = 1e-5
RMS_EPS = 1e-6
DEPTH = 1
ALPHA = (2 * DEPTH) ** 0.25

LANES = 128
SUBLANES = 8
VMEM_LIMIT = 56 * 1024 * 1024

BF16_ROWS = 16
V_PAD = BF16_ROWS
MLSTM_CHUNK = 256
GATE_GROUP = 8
LOWEST = float(jnp.finfo(jnp.float32).min)
LARGEST = float(jnp.finfo(jnp.float32).max)
NEG = -1e30
LOG2E = 1.4426950408889634


def _dot(a, b):
    return jnp.dot(a, b, preferred_element_type=F32)


def _dot_nt(a, b):
    return lax.dot_general(a, b, (((1,), (1,)), ((), ())), preferred_element_type=F32)


def _dot_tn(a, b):
    return lax.dot_general(a, b, (((0,), (0,)), ((), ())), preferred_element_type=F32)


def _layer_norm(y, g, b):
    mu = jnp.mean(y, axis=-1, keepdims=True)
    yc = y - mu
    var = jnp.mean(yc * yc, axis=-1, keepdims=True)
    return yc * lax.rsqrt(var + LN_EPS) * g + b


def _rms_norm(y, g):
    return y * lax.rsqrt(jnp.mean(y * y, axis=-1, keepdims=True) + RMS_EPS) * g


def _params(*sem):
    return pltpu.CompilerParams(dimension_semantics=sem, vmem_limit_bytes=VMEM_LIMIT)


def _ffn_ln_body(x_ref, wg_ref, wu_ref, wd_ref, g_ref, b_ref, o_ref, *rest, cast_weights, n_sub):
    copies = rest[:-2]
    acc_ref, xb_ref = rest[-2:]
    i = pl.program_id(0)
    f = pl.program_id(1)
    nt = pl.num_programs(0) - 1
    cur = i % 2
    rows = o_ref.shape[0]

    @pl.when((i == 0) & (f == 0))
    def _():
        acc_ref[1] = jnp.zeros(acc_ref.shape[1:], F32)

    def norm_previous_tile():
        r0 = pl.multiple_of(jnp.minimum(f, n_sub - 1) * rows, rows)
        y = 0.5 * acc_ref[1 - cur, pl.ds(r0, rows), :]
        out = _layer_norm(y, g_ref[...], b_ref[...])
        o_ref[...] = out
        return jnp.max(out, axis=(0, 1), keepdims=True)

    @pl.when(i < nt)
    def _():
        @pl.when(f == 0)
        def _():
            x = x_ref[...]
            xb_ref[...] = x.astype(BF16)
            acc_ref[cur] = (2.0 * ALPHA) * x

        wg, wu, wd = wg_ref[...], wu_ref[...], wd_ref[...]
        if cast_weights:
            wg, wu, wd = wg.astype(BF16), wu.astype(BF16), wd.astype(BF16)
            for dst, w in zip(copies, (wg, wu, wd)):
                dst[...] = w

        norm_max = norm_previous_tile()
        xb = xb_ref[...]
        half = wg.shape[1] // 2
        for c in range(2):
            cs = slice(c * half, (c + 1) * half)
            gate = _dot(xb, wg[:, cs])
            up = _dot(xb, wu[:, cs])
            act = gate * jax.nn.sigmoid(gate) * up
            act = jnp.where(norm_max > LARGEST, 0.0, act).astype(BF16)
            acc_ref[cur] += _dot(act, wd[cs, :])

    @pl.when(i == nt)
    def _():
        norm_previous_tile()


def _ffn_ln(x, wg, wu, wd, g, b, *, tm, tf, cast_weights=False):
    n, d = x.shape
    dff = wg.shape[1]
    nf = dff // tf
    nt = n // tm
    assert n % tm == 0 and dff % tf == 0
    n_sub = 1
    while 2 * n_sub <= nf and tm % (2 * n_sub * SUBLANES) == 0:
        n_sub *= 2
    row_i = lambda i: jnp.minimum(i, nt - 1)
    col_f = lambda i, f: jnp.where(i < nt, f, nf - 1)
    in_specs = [
        pl.BlockSpec((tm, d), lambda i, f: (row_i(i), 0)),
        pl.BlockSpec((d, tf), lambda i, f: (0, col_f(i, f))),
        pl.BlockSpec((d, tf), lambda i, f: (0, col_f(i, f))),
        pl.BlockSpec((tf, d), lambda i, f: (col_f(i, f), 0)),
        pl.BlockSpec((1, d), lambda i, f: (0, 0)),
        pl.BlockSpec((1, d), lambda i, f: (0, 0)),
    ]
    out_specs = [pl.BlockSpec(
        (tm // n_sub, d),
        lambda i, f: (jnp.where(i == 0, 0, (i - 1) * n_sub + jnp.minimum(f, n_sub - 1)), 0))]
    out_shape = [jax.ShapeDtypeStruct((n, d), F32)]
    if cast_weights:
        out_specs += in_specs[1:4]
        out_shape += [jax.ShapeDtypeStruct(w.shape, BF16) for w in (wg, wu, wd)]
    return pl.pallas_call(
        functools.partial(_ffn_ln_body, cast_weights=cast_weights, n_sub=n_sub),
        grid=(nt + 1, nf),
        in_specs=in_specs,
        out_specs=out_specs,
        out_shape=out_shape,
        scratch_shapes=[pltpu.VMEM((2, tm, d), F32), pltpu.VMEM((tm, d), BF16)],
        compiler_params=_params("arbitrary", "arbitrary"),
        name="ffn_ln",
    )(x, wg, wu, wd, g, b)


def _proj_body(h_ref, win_ref, qg_ref, kvg_ref, wqnt_ref, wqrt_ref, wkn_ref, wvt_ref,
               rot_ref, rott_ref, *rest, ql, kvl, dm, scale, n_side):
    side_in = rest[:n_side]
    qt_ref, k_ref, vt_ref, xm_ref, z_ref = rest[n_side:n_side + 5]
    for src, dst in zip(side_in, rest[n_side + 5:]):
        dst[...] = src[...].astype(BF16)
    hb = h_ref[...].astype(BF16)
    u = _dot(hb, win_ref[...])
    o1 = ql
    o2 = o1 + kvl
    o3 = o2 + LANES
    o4 = o3 + dm
    xm_ref[...] = u[:, o3:o4]
    z_ref[...] = u[:, o4:]
    rott = rott_ref[...]
    row = lax.broadcasted_iota(jnp.int32, rott.shape, 0)
    q_rope_scale = jnp.where(row < QK_ROPE, scale, 0.0)
    qn = _rms_norm(u[:, :o1], qg_ref[...]).astype(BF16)
    qat = _dot_nt(wqnt_ref[...], qn)
    qrt = _dot_nt(wqrt_ref[...], qn)
    kvn = _rms_norm(u[:, o1:o2], kvg_ref[...]).astype(BF16)
    kn = _dot(kvn, wkn_ref[...])
    vt = _dot_nt(wvt_ref[...], kvn).astype(BF16)
    dv = vt.shape[0] // ATT_HEADS
    ones_rows = (lax.broadcasted_iota(jnp.int32, (V_PAD, vt.shape[1]), 0) == 0).astype(BF16)
    for h in range(ATT_HEADS):
        lo = h * (dv + V_PAD)
        vt_ref[lo:lo + dv, :] = vt[h * dv:(h + 1) * dv]
        vt_ref[lo + dv:lo + dv + V_PAD, :] = ones_rows
    t = u[:, o2:o3] * rot_ref[...]
    k_rope = (t + pltpu.roll(t, LANES // 2, 1)).astype(BF16)
    for h in range(ATT_HEADS):
        sl = slice(h * LANES, (h + 1) * LANES)
        lo = 2 * h * LANES
        qt_ref[lo:lo + LANES, :] = (qat[sl] * scale).astype(BF16)
        t = qrt[sl] * rott
        qt_ref[lo + LANES:lo + 2 * LANES, :] = ((t + pltpu.roll(t, LANES // 2, 0)) * q_rope_scale).astype(BF16)
        k_ref[:, lo:lo + LANES] = kn[:, sl].astype(BF16)
        k_ref[:, lo + LANES:lo + 2 * LANES] = k_rope


def _slice_spec(shape, steps):
    rows, cols = shape
    for csplit in (1, 2, 4, 8):
        rblocks = steps // csplit
        if (steps % csplit == 0 and rows % (rblocks * BF16_ROWS) == 0 and cols % (csplit * LANES) == 0):
            return pl.BlockSpec((rows // rblocks, cols // csplit), lambda i: (i // csplit, i % csplit))
    raise ValueError(f"cannot slice {shape} over {steps} steps")


def _proj(h, win, qg, kvg, wqnt, wqrt, wkn, wvt, rot, *, tm, dm, side=()):
    n, d = h.shape
    ql = qg.shape[1]
    kvl = kvg.shape[1]
    da = wvt.shape[0] + ATT_HEADS * V_PAD
    npos = rot.shape[0] // tm
    hq = ATT_HEADS * 2 * LANES
    const = lambda i: (0, 0)
    scale = float((QK_NOPE + QK_ROPE) ** -0.5 * LOG2E)
    side_specs = [_slice_spec(w.shape, n // tm) for w in side]
    return pl.pallas_call(
        functools.partial(_proj_body, ql=ql, kvl=kvl, dm=dm, scale=scale, n_side=len(side)),
        grid=(n // tm,),
        in_specs=[
            pl.BlockSpec((tm, d), lambda i: (i, 0)),
            pl.BlockSpec(win.shape, const),
            pl.BlockSpec(qg.shape, const),
            pl.BlockSpec(kvg.shape, const),
            pl.BlockSpec(wqnt.shape, const),
            pl.BlockSpec(wqrt.shape, const),
            pl.BlockSpec(wkn.shape, const),
            pl.BlockSpec(wvt.shape, const),
            pl.BlockSpec((tm, LANES), lambda i: (i % npos, 0)),
            pl.BlockSpec((LANES, tm), lambda i: (0, i % npos)),
        ] + side_specs,
        out_specs=[
            pl.BlockSpec((hq, tm), lambda i: (0, i)),
            pl.BlockSpec((tm, hq), lambda i: (i, 0)),
            pl.BlockSpec((da, tm), lambda i: (0, i)),
            pl.BlockSpec((tm, dm), lambda i: (i, 0)),
            pl.BlockSpec((tm, dm), lambda i: (i, 0)),
        ] + side_specs,
        out_shape=[
            jax.ShapeDtypeStruct((hq, n), BF16),
            jax.ShapeDtypeStruct((n, hq), BF16),
            jax.ShapeDtypeStruct((da, n), BF16),
            jax.ShapeDtypeStruct((n, dm), F32),
            jax.ShapeDtypeStruct((n, dm), F32),
        ] + [jax.ShapeDtypeStruct(w.shape, BF16) for w in side],
        compiler_params=_params("parallel"),
        name="proj",
    )(h, win, qg, kvg, wqnt, wqrt, wkn, wvt, rot, rot.T, *side)


def _log_sigmoid(x):
    return jnp.minimum(x, 0.0) - jnp.log1p(jnp.exp(-jnp.abs(x)))


def _gate_scans(gi, lf, chunk):
    row = lax.broadcasted_iota(jnp.int32, (chunk, LANES), 0)
    lane = lax.broadcasted_iota(jnp.int32, (chunk, LANES), 1)
    j = lane % GATE_GROUP
    fwd = j < 3

    def scan(val, combine, identity):
        pre = val
        suf = val
        k = 1
        while k < chunk:
            pre = combine(pre, jnp.where(row >= k, pltpu.roll(pre, k, 0), identity))
            suf = combine(suf, jnp.where(row < chunk - k, pltpu.roll(suf, chunk - k, 0), identity))
            k *= 2
        return jnp.where(fwd, pre, suf)

    b = scan(lf, jnp.add, 0.0)
    r = gi - b
    cm = scan(r, jnp.maximum, LOWEST)
    return jnp.where(j % 3 == 0, b, jnp.where(j % 3 == 1, cm, r))


def _prep_body(prev_ref, x_ref, next_ref, mtail_ref, cw_ref, cb_ref, wk_ref, wv_ref, wqt_ref,
               wvt_ref, wgc_ref, wgm_ref, bg_ref, *rest, tm, chunk, dh, meta):
    if meta:
        k_ref, v_ref, gc_ref, xs_ref = rest
    else:
        k_ref, qt_ref, vt_ref, xc_ref, gc_ref, gr_ref, xs_ref = rest
    i = pl.program_id(1)
    nt = pl.num_programs(1)
    x = x_ref[...]
    if meta:
        prev = jnp.zeros_like(prev_ref[...])
        nxt = next_ref[...]
    else:
        prev = jnp.where(i == 0, mtail_ref[...], prev_ref[...])
        nxt = jnp.where(i == nt - 1, 0.0, next_ref[...])
    xs_ref[0:SUBLANES, :] = prev
    xs_ref[SUBLANES:SUBLANES + tm, :] = x
    xs_ref[SUBLANES + tm:2 * SUBLANES + tm, :] = nxt
    acc = jnp.broadcast_to(cb_ref[...], x.shape)
    for t in range(CONV_K):
        off = SUBLANES - CONV_K // 2 + t
        acc = acc + cw_ref[t:t + 1, :] * xs_ref[off:off + tm, :]
    xc = acc * jax.nn.sigmoid(acc)
    xcb = xc.astype(BF16)
    xmb = x.astype(BF16)
    k_scale = dh ** -0.5
    if not meta:
        xc_ref[...] = xcb
        ones_rows = (lax.broadcasted_iota(jnp.int32, (V_PAD, tm), 0) == 0).astype(BF16)
    for h in range(MLSTM_HEADS):
        sl = slice(h * dh, (h + 1) * dh)
        k_ref[:, sl] = (_dot(xcb[:, sl], wk_ref[h]) * k_scale).astype(BF16)
        if meta:
            v_ref[:, sl] = _dot(xmb[:, sl], wv_ref[h]).astype(BF16)
        else:
            qt_ref[sl, :] = _dot_nt(wqt_ref[h], xcb[:, sl]).astype(BF16)
            lo = h * (dh + V_PAD)
            vt_ref[lo:lo + dh, :] = _dot_nt(wvt_ref[h], xmb[:, sl]).astype(BF16)
            vt_ref[lo + dh:lo + dh + V_PAD, :] = ones_rows
    g = _dot(xcb, wgc_ref[...]) + _dot(xmb, wgm_ref[...]) + bg_ref[...]
    gi = g[:, :LANES]
    lf = _log_sigmoid(g[:, LANES:])
    for ci in range(tm // chunk):
        rows = slice(ci * chunk, (ci + 1) * chunk)
        out = _gate_scans(gi[rows], lf[rows], chunk)
        gc_ref[rows, :] = out
        if not meta:
            gr_ref[ci] = out.T


def _prep(xm, xm_meta, cw, cb, wk, wv, wqt, wvt, wgc, wgm, bg, *, batch, tm, chunk, meta):
    dm = xm.shape[1]
    dh = dm // MLSTM_HEADS
    seq = xm.shape[0] // batch
    if meta:
        nt = 1
        n_out = batch * N_META
        x_arr = xm_meta
        x_spec = pl.BlockSpec((tm, dm), lambda b, i: (0, 0))
        prev_spec = pl.BlockSpec((SUBLANES, dm), lambda b, i: (0, 0))
        next_spec = pl.BlockSpec((SUBLANES, dm), lambda b, i: (b * (seq // SUBLANES), 0))
    else:
        nt = seq // tm
        n_out = batch * seq
        x_arr = xm
        last_blk = batch * seq // SUBLANES - 1
        x_spec = pl.BlockSpec((tm, dm), lambda b, i: (b * nt + i, 0))
        prev_spec = pl.BlockSpec(
            (SUBLANES, dm), lambda b, i: (jnp.maximum((b * nt + i) * (tm // SUBLANES) - 1, 0), 0))
        next_spec = pl.BlockSpec(
            (SUBLANES, dm), lambda b, i: (jnp.minimum((b * nt + i + 1) * (tm // SUBLANES), last_blk), 0))
    const2 = lambda b, i: (0, 0)
    const3 = lambda b, i: (0, 0, 0)
    row_spec = lambda w: pl.BlockSpec((tm, w), lambda b, i: (b * nt + i, 0))
    col_spec = lambda r: pl.BlockSpec((r, tm), lambda b, i: (0, b * nt + i))
    bf = lambda *shape: jax.ShapeDtypeStruct(shape, BF16)
    gc_shape = jax.ShapeDtypeStruct((n_out, LANES), F32)
    if meta:
        out_specs = [row_spec(dm), row_spec(dm), row_spec(LANES)]
        out_shape = [bf(n_out, dm), bf(n_out, dm), gc_shape]
    else:
        dva = dm + MLSTM_HEADS * V_PAD
        out_specs = [row_spec(dm), col_spec(dm), col_spec(dva), row_spec(dm), row_spec(LANES),
                     pl.BlockSpec((tm // chunk, LANES, chunk), lambda b, i: (b * nt + i, 0, 0))]
        out_shape = [bf(n_out, dm), bf(dm, n_out), bf(dva, n_out), bf(n_out, dm), gc_shape,
                     jax.ShapeDtypeStruct((n_out // chunk, LANES, chunk), F32)]
    return pl.pallas_call(
        functools.partial(_prep_body, tm=tm, chunk=chunk, dh=dh, meta=meta),
        grid=(batch, nt),
        in_specs=[
            prev_spec, x_spec, next_spec,
            pl.BlockSpec((SUBLANES, dm), lambda b, i: (1, 0)),
            pl.BlockSpec(cw.shape, const2),
            pl.BlockSpec(cb.shape, const2),
            pl.BlockSpec(wk.shape, const3),
            pl.BlockSpec(wv.shape, const3),
            pl.BlockSpec(wqt.shape, const3),
            pl.BlockSpec(wvt.shape, const3),
            pl.BlockSpec(wgc.shape, const2),
            pl.BlockSpec(wgm.shape, const2),
            pl.BlockSpec(bg.shape, const2),
        ],
        out_specs=out_specs,
        out_shape=out_shape,
        scratch_shapes=[pltpu.VMEM((tm + 2 * SUBLANES, dm), F32)],
        compiler_params=_params("parallel", "parallel"),
        name="mlstm_prep_meta" if meta else "mlstm_prep",
    )(xm, x_arr, xm, xm_meta, cw, cb, wk, wv, wqt, wvt, wgc, wgm, bg)


def _scan_chain(k, qt, vta, r_col, rows, c_ref, m_ref, mask, edge, dh):
    b_row, cm_row = rows
    m_old = m_ref[...]
    ct = c_ref[...]

    mt = jnp.maximum(m_old, cm_row)
    at = jnp.exp(jnp.where(mask, r_col - mt, NEG)) * _dot(k, qt)
    inter = _dot(ct.astype(BF16), qt)
    w = jnp.exp(m_old - mt)
    den = jnp.sum(at, axis=0, keepdims=True) + w * inter[dh:dh + 1, :]
    scale = 1.0 / jnp.maximum(jnp.abs(den), jnp.exp(-(b_row + mt)))
    ht = _dot(vta[:dh], (at * scale).astype(BF16)) + inter[:dh] * (w * scale)

    b_end = b_row[:, edge:edge + 1]
    m_new = jnp.maximum(b_end + m_old, b_end + cm_row[:, edge:edge + 1])
    decay = jnp.exp(b_end + m_old - m_new)
    kw = (k.astype(F32) * jnp.exp(b_end + r_col - m_new)).astype(BF16)
    c_ref[...] = decay * ct + _dot(vta, kw)
    m_ref[...] = m_new
    return ht


def _scan_body(kf_ref, qtf_ref, vtf_ref, gcf_ref, grf_ref,
               kb_ref, qtb_ref, vtb_ref, gcb_ref, grb_ref,
               km_ref, vm_ref, gcm_ref, hf_ref, hb_ref, c_ref, m_ref, *, chunk, dh):
    ci = pl.program_id(1)
    nh = MLSTM_HEADS
    dva = dh + V_PAD

    @pl.when(ci == 0)
    def _():
        gcm = gcm_ref[0]
        km = km_ref[0]
        vm = vm_ref[0]
        first = lax.broadcasted_iota(jnp.int32, (V_PAD, dh), 0) == 0
        for h in range(nh):
            sl = slice(h * dh, (h + 1) * dh)
            lane0 = h * GATE_GROUP
            b_end = gcm[N_META - 1:N_META, lane0:lane0 + 1]
            m_new = jnp.maximum(b_end, b_end + gcm[N_META - 1:N_META, lane0 + 1:lane0 + 2])
            kw = km[:, sl].astype(F32) * jnp.exp(b_end + gcm[:, lane0 + 2:lane0 + 3] - m_new)
            c_ref[h, 0:dh, :] = _dot_tn(vm[:, sl], kw.astype(BF16))
            c_ref[h, dh:dva, :] = jnp.where(first, jnp.sum(kw, axis=0, keepdims=True), 0.0)
            m_ref[h] = m_new
            c_ref[nh + h] = jnp.zeros((dva, dh), F32)
            m_ref[nh + h] = jnp.zeros((1, 1), F32)

    srow = lax.broadcasted_iota(jnp.int32, (chunk, chunk), 0)
    tcol = lax.broadcasted_iota(jnp.int32, (chunk, chunk), 1)
    for h in range(nh):
        sl = slice(h * dh, (h + 1) * dh)
        sla = slice(h * dva, (h + 1) * dva)
        lane0 = h * GATE_GROUP
        rows_f = tuple(grf_ref[0, lane0 + j:lane0 + j + 1, :] for j in (0, 1))
        rows_b = tuple(grb_ref[0, lane0 + j:lane0 + j + 1, :] for j in (3, 4))
        hf_ref[sl, :] = _scan_chain(
            kf_ref[:, sl], qtf_ref[sl, :], vtf_ref[sla, :], gcf_ref[:, lane0 + 2:lane0 + 3],
            rows_f, c_ref.at[h], m_ref.at[h], srow <= tcol, chunk - 1, dh).astype(hf_ref.dtype)
        hb_ref[sl, :] = _scan_chain(
            kb_ref[:, sl], qtb_ref[sl, :], vtb_ref[sla, :], gcb_ref[:, lane0 + 5:lane0 + 6],
            rows_b, c_ref.at[nh + h], m_ref.at[nh + h], srow >= tcol, 0, dh).astype(hb_ref.dtype)


def _scan(k, qt, vta, gc, gr, km, vm, gcm, *, batch, chunk):
    n, dm = k.shape
    dh = dm // MLSTM_HEADS
    dva = vta.shape[0]
    nc = n // batch // chunk
    fwd = lambda b, c: (b * nc + c, 0)
    bwd = lambda b, c: (b * nc + nc - 1 - c, 0)
    fwd_t = lambda b, c: (0, b * nc + c)
    bwd_t = lambda b, c: (0, b * nc + nc - 1 - c)
    fwd3 = lambda b, c: (b * nc + c, 0, 0)
    bwd3 = lambda b, c: (b * nc + nc - 1 - c, 0, 0)
    per_b = lambda b, c: (b, 0, 0)

    def specs(idx, idx_t, idx3):
        return [pl.BlockSpec((chunk, dm), idx), pl.BlockSpec((dm, chunk), idx_t),
                pl.BlockSpec((dva, chunk), idx_t), pl.BlockSpec((chunk, LANES), idx),
                pl.BlockSpec((1, LANES, chunk), idx3)]

    return pl.pallas_call(
        functools.partial(_scan_body, chunk=chunk, dh=dh),
        grid=(batch, nc),
        in_specs=specs(fwd, fwd_t, fwd3) + specs(bwd, bwd_t, bwd3) + [
            pl.BlockSpec((1, N_META, dm), per_b),
            pl.BlockSpec((1, N_META, dm), per_b),
            pl.BlockSpec((1, N_META, LANES), per_b),
        ],
        out_specs=[pl.BlockSpec((dm, chunk), fwd_t), pl.BlockSpec((dm, chunk), bwd_t)],
        out_shape=[jax.ShapeDtypeStruct((dm, n), BF16)] * 2,
        scratch_shapes=[
            pltpu.VMEM((2 * MLSTM_HEADS, dh + V_PAD, dh), F32),
            pltpu.VMEM((2 * MLSTM_HEADS, 1, 1), F32),
        ],
        compiler_params=_params("parallel", "arbitrary"),
        name="mlstm_scan",
    )(k, qt, vta, gc, gr, k, qt, vta, gc, gr,
      km.reshape(batch, N_META, dm), vm.reshape(batch, N_META, dm), gcm.reshape(batch, N_META, LANES))


def _attn_body(qt_ref, k_ref, vt_ref, km_ref, vmt_ref, o_ref, *, tk, nk):
    qt = qt_ref[...]
    dv = o_ref.shape[1]

    def scores(j):
        s = _dot(k_ref[j * tk:(j + 1) * tk, :], qt)
        return s, jnp.max(s, axis=0, keepdims=True)

    s_meta = _dot(km_ref[...], qt)
    ahead = scores(0)
    m = jnp.maximum(jnp.max(s_meta, axis=0, keepdims=True), ahead[1])
    acc = _dot(vmt_ref[...], jnp.exp2(s_meta - m).astype(BF16))
    for j in range(nk):
        s, s_max = ahead
        if j + 1 < nk:
            ahead = scores(j + 1)
        m_new = jnp.maximum(m, s_max)
        p = jnp.exp2(s - m_new).astype(BF16)
        acc = jnp.exp2(m - m_new) * acc + _dot(vt_ref[:, j * tk:(j + 1) * tk], p)
        m = m_new
    o_ref[...] = (acc[:dv] * (1.0 / acc[dv:dv + 1])).T


def _attn(qt, k, vt, km, vmt, *, batch, tq, tk):
    n = k.shape[0]
    seq = n // batch
    nq = seq // tq
    dva = vt.shape[0] // ATT_HEADS
    dv = dva - V_PAD
    dk = 2 * LANES
    return pl.pallas_call(
        functools.partial(_attn_body, tk=tk, nk=seq // tk),
        grid=(batch, ATT_HEADS, nq),
        in_specs=[
            pl.BlockSpec((dk, tq), lambda b, h, i: (h, b * nq + i)),
            pl.BlockSpec((seq, dk), lambda b, h, i: (b, h)),
            pl.BlockSpec((dva, seq), lambda b, h, i: (h, b)),
            pl.BlockSpec((N_META, dk), lambda b, h, i: (0, h)),
            pl.BlockSpec((dva, N_META), lambda b, h, i: (h, 0)),
        ],
        out_specs=pl.BlockSpec((tq, dv), lambda b, h, i: (b * nq + i, h)),
        out_shape=jax.ShapeDtypeStruct((n, ATT_HEADS * dv), F32),
        compiler_params=_params("parallel", "parallel", "arbitrary"),
        name="attn",
    )(qt, k, vt, km, vmt)


def _outproj_body(o_ref, hf_ref, hb_ref, z_ref, xc_ref, h1_ref, og_ref, gn_ref, sk_ref,
                  wa_ref, wm_ref, g_ref, b_ref, out_ref, *, dh, rows):
    for r in range(out_ref.shape[0] // rows):
        rs = slice(r * rows, (r + 1) * rows)
        y_att = _rms_norm(o_ref[rs, :], og_ref[...]).astype(BF16)
        h_sum = (hf_ref[:, rs].astype(F32) + hb_ref[:, rs].astype(F32)).T
        hs = h_sum * jax.nn.sigmoid(z_ref[rs, :])
        parts = []
        for h in range(MLSTM_HEADS):
            seg = hs[:, h * dh:(h + 1) * dh]
            mu = jnp.mean(seg, axis=-1, keepdims=True)
            sc = seg - mu
            var = jnp.mean(sc * sc, axis=-1, keepdims=True)
            parts.append(sc * lax.rsqrt(var + LN_EPS))
        hn = jnp.concatenate(parts, axis=-1)
        y_ml = (hn * gn_ref[...] + sk_ref[...] * xc_ref[rs, :].astype(F32)).astype(BF16)
        y = _dot(y_att, wa_ref[...]) + _dot(y_ml, wm_ref[...])
        out_ref[rs, :] = _layer_norm(ALPHA * h1_ref[rs, :] + y, g_ref[...], b_ref[...])


def _outproj(o, hf, hb, z, xc, h1, og, gn, sk, wo, g, b, *, tm, rows):
    n, d = h1.shape
    da = o.shape[1]
    dm = hf.shape[0]
    assert da == dm and wo.shape[0] == da + dm
    const = lambda i: (0, 0)
    row = lambda w: pl.BlockSpec((tm, w), lambda i: (i, 0))
    col = pl.BlockSpec((dm, tm), lambda i: (0, i))
    return pl.pallas_call(
        functools.partial(_outproj_body, dh=dm // MLSTM_HEADS, rows=rows),
        grid=(n // tm,),
        in_specs=[row(da), col, col, row(dm), row(dm), row(d),
                  pl.BlockSpec(og.shape, const), pl.BlockSpec(gn.shape, const), pl.BlockSpec(sk.shape, const),
                  pl.BlockSpec((da, d), lambda i: (0, 0), pipeline_mode=pl.Buffered(1)),
                  pl.BlockSpec((dm, d), lambda i: (1, 0), pipeline_mode=pl.Buffered(1)),
                  pl.BlockSpec(g.shape, const), pl.BlockSpec(b.shape, const)],
        out_specs=row(d),
        out_shape=jax.ShapeDtypeStruct((n, d), F32),
        compiler_params=_params("parallel"),
        name="outproj",
    )(o, hf, hb, z, xc, h1, og, gn, sk, wo, wo, g, b)


def _pad_lanes(w):
    return jnp.pad(w, [(0, 0)] * (w.ndim - 1) + [(0, LANES - w.shape[-1])])


def _pick(n, pref):
    for t in pref:
        if n % t == 0:
            return t
    raise ValueError(f"no tile for {n}")


def kernel(x, meta_tokens, ffn1_w_gate, ffn1_w_up, ffn1_w_down, ln1_g, ln1_b, w_in, mla_q_norm_g, mla_w_uq, mla_kv_norm_g, mla_w_ukv, attn_out_g, mlstm_conv_w, mlstm_conv_b, mlstm_w_q, mlstm_w_k, mlstm_w_v, mlstm_w_gates, mlstm_b_gates, mlstm_gn_g, mlstm_skip, w_out, ln2_g, ln2_b, ffn2_w_gate, ffn2_w_up, ffn2_w_down, ln3_g, ln3_b):
    batch, seq, d = x.shape
    assert ffn1_w_gate.shape[0] == DEPTH and meta_tokens.shape[0] == N_META
    n = batch * seq
    ql = mla_q_norm_g.shape[-1]
    kvl = mla_kv_norm_g.shape[-1]
    da = attn_out_g.shape[-1]
    dm = mlstm_gn_g.shape[-1]
    dh = dm // MLSTM_HEADS
    dv = da // ATT_HEADS
    nh = MLSTM_HEADS
    half = QK_ROPE // 2
    assert dv == LANES and dh % LANES == 0 and ql % LANES == 0 and kvl % LANES == 0
    assert seq % MLSTM_CHUNK == 0

    wi = w_in[0]
    o1, o2, o3, o4 = ql, ql + kvl, ql + kvl + QK_ROPE, ql + kvl + QK_ROPE + dm
    swap = jnp.concatenate([jnp.arange(half, QK_ROPE), jnp.arange(half)])
    w_kr = wi[:, o2:o3]
    win = jnp.concatenate([wi[:, :o2], w_kr, w_kr[:, swap], wi[:, o3:]], axis=1).astype(BF16)
    wuq = mla_w_uq[0].reshape(ql, ATT_HEADS, QK_NOPE + QK_ROPE)
    wqnt = wuq[:, :, :QK_NOPE].reshape(ql, -1).T.astype(BF16)
    w_qr = wuq[:, :, QK_NOPE:]
    wqrt = jnp.concatenate([w_qr, w_qr[:, :, swap]], axis=-1).reshape(ql, -1).T.astype(BF16)
    wukv = mla_w_ukv[0].reshape(kvl, ATT_HEADS, QK_NOPE + dv)
    wkn = wukv[:, :, :QK_NOPE].reshape(kvl, -1).astype(BF16)
    wvt = wukv[:, :, QK_NOPE:].reshape(kvl, -1).T.astype(BF16)

    wg = mlstm_w_gates[0].reshape(2 * dm, 4, nh)
    bgr = mlstm_b_gates[0].reshape(4, nh)

    def gate_tile(w4, kind_f, kind_b):
        zeros = jnp.zeros_like(w4[..., 0, :])
        cols = [w4[..., kind_f, :]] * 3 + [w4[..., kind_b, :]] * 3 + [zeros] * (GATE_GROUP - 6)
        tile = jnp.stack(cols, axis=-1).reshape(*w4.shape[:-2], nh * GATE_GROUP)
        return _pad_lanes(tile)

    wgate = jnp.concatenate([gate_tile(wg, 0, 2), gate_tile(wg, 1, 3)], axis=-1)
    wgc = wgate[:dm].astype(BF16)
    wgm = wgate[dm:].astype(BF16)
    bg = jnp.concatenate([gate_tile(bgr, 0, 2), gate_tile(bgr, 1, 3)], axis=-1)[None, :]
    cw = jnp.pad(mlstm_conv_w[0], ((0, SUBLANES - CONV_K), (0, 0)))
    cb = mlstm_conv_b[0][None, :]
    wk_m, wv_m = (w[0].astype(BF16) for w in (mlstm_w_k, mlstm_w_v))
    wqt_m, wvt_m = (jnp.swapaxes(w[0], 1, 2).astype(BF16) for w in (mlstm_w_q, mlstm_w_v))
    wo = w_out[0].astype(BF16)
    row = lambda p: p[0][None, :]

    pos = jnp.arange(N_META + seq, dtype=F32)
    inv = ROPE_BASE ** (-jnp.arange(0, QK_ROPE, 2, dtype=F32) / QK_ROPE)
    ang = pos[:, None] * inv[None, :]
    rot = jnp.concatenate([jnp.cos(ang), jnp.cos(ang), -jnp.sin(ang), jnp.sin(ang)], axis=-1)

    ffn1_f32 = [w[0] for w in (ffn1_w_gate, ffn1_w_up, ffn1_w_down)]
    ffn2_f32 = [w[0] for w in (ffn2_w_gate, ffn2_w_up, ffn2_w_down)]
    tf = _pick(ffn1_f32[0].shape[1], (512, 256, 128))
    tm_ffn = _pick(n, (1024, 512, 256))
    tm_proj = _pick(seq, (256,))

    xr = x.reshape(n, d)
    h1m, *ffn1 = _ffn_ln(meta_tokens.astype(x.dtype), *ffn1_f32, row(ln1_g), row(ln1_b),
                         tm=N_META, tf=tf, cast_weights=True)
    h1, = _ffn_ln(xr, *ffn1, row(ln1_g), row(ln1_b), tm=tm_ffn, tf=tf)
    proj_w = (win, row(mla_q_norm_g), row(mla_kv_norm_g), wqnt, wqrt, wkn, wvt)
    q_a, k_a, vt_a, xm, z, *ffn2 = _proj(h1, *proj_w, rot[N_META:], tm=tm_proj, dm=dm, side=ffn2_f32)
    _, k_am, vt_am, xm_m, _ = _proj(h1m, *proj_w, rot[:N_META], tm=N_META, dm=dm)

    prep_w = (cw, cb, wk_m, wv_m, wqt_m, wvt_m, wgc, wgm, bg)
    tm_prep = _pick(seq, (512, MLSTM_CHUNK))
    k_m, qt_m, vt_m, xc, gc, gr = _prep(
        xm, xm_m, *prep_w, batch=batch, tm=tm_prep, chunk=MLSTM_CHUNK, meta=False)
    k_mm, v_mm, gc_m = _prep(xm, xm_m, *prep_w, batch=batch, tm=N_META, chunk=N_META, meta=True)
    hf, hb = _scan(k_m, qt_m, vt_m, gc, gr, k_mm, v_mm, gc_m, batch=batch, chunk=MLSTM_CHUNK)

    o = _attn(q_a, k_a, vt_a, k_am, vt_am, batch=batch, tq=_pick(seq, (2048, 1024, 512, 256)), tk=_pick(seq, (512, 256)))

    h2 = _outproj(o, hf, hb, z, xc, h1, row(attn_out_g), row(mlstm_gn_g), row(mlstm_skip),
                  wo, row(ln2_g), row(ln2_b), tm=_pick(seq, (512, 256)), rows=_pick(seq, (256,)))
    out, = _ffn_ln(h2, *ffn2, row(ln3_g), row(ln3_b), tm=tm_ffn, tf=tf)
    return out.reshape(batch, seq, d)
```

```python
import functools

import jax
import jax.numpy as jnp
from jax import lax
from jax.experimental import pallas as pl
from jax.experimental.pallas import tpu as pltpu

F32 = jnp.float32
BF16 = jnp.bfloat16

N_META = 16
ATT_HEADS = 8
QK_NOPE = 128
QK_ROPE = 64
MLSTM_HEADS = 4
CONV_K = 5
ROPE_BASE = 10000.0
LN_EPS = 1e-5
RMS_EPS = 1e-6
DEPTH = 1
ALPHA = (2 * DEPTH) ** 0.25

LANES = 128
SUBLANES = 8
VMEM_LIMIT = 56 * 1024 * 1024

BF16_ROWS = 16
V_PAD = BF16_ROWS
MLSTM_CHUNK = 256
GATE_GROUP = 8
LOWEST = float(jnp.finfo(jnp.float32).min)
LARGEST = float(jnp.finfo(jnp.float32).max)
NEG = -1e30
LOG2E = 1.4426950408889634


def _dot(a, b):
    return jnp.dot(a, b, preferred_element_type=F32)


def _dot_nt(a, b):
    return lax.dot_general(a, b, (((1,), (1,)), ((), ())), preferred_element_type=F32)


def _dot_tn(a, b):
    return lax.dot_general(a, b, (((0,), (0,)), ((), ())), preferred_element_type=F32)


def _layer_norm(y, g, b):
    mu = jnp.mean(y, axis=-1, keepdims=True)
    yc = y - mu
    var = jnp.mean(yc * yc, axis=-1, keepdims=True)
    return yc * lax.rsqrt(var + LN_EPS) * g + b


def _rms_norm(y, g):
    return y * lax.rsqrt(jnp.mean(y * y, axis=-1, keepdims=True) + RMS_EPS) * g


def _params(*sem):
    return pltpu.CompilerParams(dimension_semantics=sem, vmem_limit_bytes=VMEM_LIMIT)


def _ffn_ln_body(x_ref, wg_ref, wu_ref, wd_ref, g_ref, b_ref, o_ref, *rest, cast_weights, n_sub):
    copies = rest[:-2]
    acc_ref, xb_ref = rest[-2:]
    i = pl.program_id(0)
    f = pl.program_id(1)
    nt = pl.num_programs(0) - 1
    cur = i % 2
    rows = o_ref.shape[0]

    @pl.when((i == 0) & (f == 0))
    def _():
        acc_ref[1] = jnp.zeros(acc_ref.shape[1:], F32)

    def norm_previous_tile():
        r0 = pl.multiple_of(jnp.minimum(f, n_sub - 1) * rows, rows)
        y = 0.5 * acc_ref[1 - cur, pl.ds(r0, rows), :]
        out = _layer_norm(y, g_ref[...], b_ref[...])
        o_ref[...] = out
        return jnp.max(out, axis=(0, 1), keepdims=True)

    @pl.when(i < nt)
    def _():
        @pl.when(f == 0)
        def _():
            x = x_ref[...]
            xb_ref[...] = x.astype(BF16)
            acc_ref[cur] = (2.0 * ALPHA) * x

        wg, wu, wd = wg_ref[...], wu_ref[...], wd_ref[...]
        if cast_weights:
            wg, wu, wd = wg.astype(BF16), wu.astype(BF16), wd.astype(BF16)
            for dst, w in zip(copies, (wg, wu, wd)):
                dst[...] = w

        norm_max = norm_previous_tile()
        xb = xb_ref[...]
        half = wg.shape[1] // 2
        for c in range(2):
            cs = slice(c * half, (c + 1) * half)
            gate = _dot(xb, wg[:, cs])
            up = _dot(xb, wu[:, cs])
            act = gate * jax.nn.sigmoid(gate) * up
            act = jnp.where(norm_max > LARGEST, 0.0, act).astype(BF16)
            acc_ref[cur] += _dot(act, wd[cs, :])

    @pl.when(i == nt)
    def _():
        norm_previous_tile()


def _ffn_ln(x, wg, wu, wd, g, b, *, tm, tf, cast_weights=False):
    n, d = x.shape
    dff = wg.shape[1]
    nf = dff // tf
    nt = n // tm
    assert n % tm == 0 and dff % tf == 0
    n_sub = 1
    while 2 * n_sub <= nf and tm % (2 * n_sub * SUBLANES) == 0:
        n_sub *= 2
    row_i = lambda i: jnp.minimum(i, nt - 1)
    col_f = lambda i, f: jnp.where(i < nt, f, nf - 1)
    in_specs = [
        pl.BlockSpec((tm, d), lambda i, f: (row_i(i), 0)),
        pl.BlockSpec((d, tf), lambda i, f: (0, col_f(i, f))),
        pl.BlockSpec((d, tf), lambda i, f: (0, col_f(i, f))),
        pl.BlockSpec((tf, d), lambda i, f: (col_f(i, f), 0)),
        pl.BlockSpec((1, d), lambda i, f: (0, 0)),
        pl.BlockSpec((1, d), lambda i, f: (0, 0)),
    ]
    out_specs = [pl.BlockSpec(
        (tm // n_sub, d),
        lambda i, f: (jnp.where(i == 0, 0, (i - 1) * n_sub + jnp.minimum(f, n_sub - 1)), 0))]
    out_shape = [jax.ShapeDtypeStruct((n, d), F32)]
    if cast_weights:
        out_specs += in_specs[1:4]
        out_shape += [jax.ShapeDtypeStruct(w.shape, BF16) for w in (wg, wu, wd)]
    return pl.pallas_call(
        functools.partial(_ffn_ln_body, cast_weights=cast_weights, n_sub=n_sub),
        grid=(nt + 1, nf),
        in_specs=in_specs,
        out_specs=out_specs,
        out_shape=out_shape,
        scratch_shapes=[pltpu.VMEM((2, tm, d), F32), pltpu.VMEM((tm, d), BF16)],
        compiler_params=_params("arbitrary", "arbitrary"),
        name="ffn_ln",
    )(x, wg, wu, wd, g, b)


def _proj_body(h_ref, win_ref, qg_ref, kvg_ref, wqnt_ref, wqrt_ref, wkn_ref, wvt_ref,
               rot_ref, rott_ref, *rest, ql, kvl, dm, scale, n_side):
    side_in = rest[:n_side]
    qt_ref, k_ref, vt_ref, xm_ref, z_ref = rest[n_side:n_side + 5]
    for src, dst in zip(side_in, rest[n_side + 5:]):
        dst[...] = src[...].astype(BF16)
    hb = h_ref[...].astype(BF16)
    u = _dot(hb, win_ref[...])
    o1 = ql
    o2 = o1 + kvl
    o3 = o2 + LANES
    o4 = o3 + dm
    xm_ref[...] = u[:, o3:o4]
    z_ref[...] = u[:, o4:]
    rott = rott_ref[...]
    row = lax.broadcasted_iota(jnp.int32, rott.shape, 0)
    q_rope_scale = jnp.where(row < QK_ROPE, scale, 0.0)
    qn = _rms_norm(u[:, :o1], qg_ref[...]).astype(BF16)
    qat = _dot_nt(wqnt_ref[...], qn)
    qrt = _dot_nt(wqrt_ref[...], qn)
    kvn = _rms_norm(u[:, o1:o2], kvg_ref[...]).astype(BF16)
    kn = _dot(kvn, wkn_ref[...])
    vt = _dot_nt(wvt_ref[...], kvn).astype(BF16)
    dv = vt.shape[0] // ATT_HEADS
    ones_rows = (lax.broadcasted_iota(jnp.int32, (V_PAD, vt.shape[1]), 0) == 0).astype(BF16)
    for h in range(ATT_HEADS):
        lo = h * (dv + V_PAD)
        vt_ref[lo:lo + dv, :] = vt[h * dv:(h + 1) * dv]
        vt_ref[lo + dv:lo + dv + V_PAD, :] = ones_rows
    t = u[:, o2:o3] * rot_ref[...]
    k_rope = (t + pltpu.roll(t, LANES // 2, 1)).astype(BF16)
    for h in range(ATT_HEADS):
        sl = slice(h * LANES, (h + 1) * LANES)
        lo = 2 * h * LANES
        qt_ref[lo:lo + LANES, :] = (qat[sl] * scale).astype(BF16)
        t = qrt[sl] * rott
        qt_ref[lo + LANES:lo + 2 * LANES, :] = ((t + pltpu.roll(t, LANES // 2, 0)) * q_rope_scale).astype(BF16)
        k_ref[:, lo:lo + LANES] = kn[:, sl].astype(BF16)
        k_ref[:, lo + LANES:lo + 2 * LANES] = k_rope


def _slice_spec(shape, steps):
    rows, cols = shape
    for csplit in (1, 2, 4, 8):
        rblocks = steps // csplit
        if (steps % csplit == 0 and rows % (rblocks * BF16_ROWS) == 0 and cols % (csplit * LANES) == 0):
            return pl.BlockSpec((rows // rblocks, cols // csplit), lambda i: (i // csplit, i % csplit))
    raise ValueError(f"cannot slice {shape} over {steps} steps")


def _proj(h, win, qg, kvg, wqnt, wqrt, wkn, wvt, rot, *, tm, dm, side=()):
    n, d = h.shape
    ql = qg.shape[1]
    kvl = kvg.shape[1]
    da = wvt.shape[0] + ATT_HEADS * V_PAD
    npos = rot.shape[0] // tm
    hq = ATT_HEADS * 2 * LANES
    const = lambda i: (0, 0)
    scale = float((QK_NOPE + QK_ROPE) ** -0.5 * LOG2E)
    side_specs = [_slice_spec(w.shape, n // tm) for w in side]
    return pl.pallas_call(
        functools.partial(_proj_body, ql=ql, kvl=kvl, dm=dm, scale=scale, n_side=len(side)),
        grid=(n // tm,),
        in_specs=[
            pl.BlockSpec((tm, d), lambda i: (i, 0)),
            pl.BlockSpec(win.shape, const),
            pl.BlockSpec(qg.shape, const),
            pl.BlockSpec(kvg.shape, const),
            pl.BlockSpec(wqnt.shape, const),
            pl.BlockSpec(wqrt.shape, const),
            pl.BlockSpec(wkn.shape, const),
            pl.BlockSpec(wvt.shape, const),
            pl.BlockSpec((tm, LANES), lambda i: (i % npos, 0)),
            pl.BlockSpec((LANES, tm), lambda i: (0, i % npos)),
        ] + side_specs,
        out_specs=[
            pl.BlockSpec((hq, tm), lambda i: (0, i)),
            pl.BlockSpec((tm, hq), lambda i: (i, 0)),
            pl.BlockSpec((da, tm), lambda i: (0, i)),
            pl.BlockSpec((tm, dm), lambda i: (i, 0)),
            pl.BlockSpec((tm, dm), lambda i: (i, 0)),
        ] + side_specs,
        out_shape=[
            jax.ShapeDtypeStruct((hq, n), BF16),
            jax.ShapeDtypeStruct((n, hq), BF16),
            jax.ShapeDtypeStruct((da, n), BF16),
            jax.ShapeDtypeStruct((n, dm), F32),
            jax.ShapeDtypeStruct((n, dm), F32),
        ] + [jax.ShapeDtypeStruct(w.shape, BF16) for w in side],
        compiler_params=_params("parallel"),
        name="proj",
    )(h, win, qg, kvg, wqnt, wqrt, wkn, wvt, rot, rot.T, *side)


def _log_sigmoid(x):
    return jnp.minimum(x, 0.0) - jnp.log1p(jnp.exp(-jnp.abs(x)))


def _gate_scans(gi, lf, chunk):
    row = lax.broadcasted_iota(jnp.int32, (chunk, LANES), 0)
    lane = lax.broadcasted_iota(jnp.int32, (chunk, LANES), 1)
    j = lane % GATE_GROUP
    fwd = j < 3

    def scan(val, combine, identity):
        pre = val
        suf = val
        k = 1
        while k < chunk:
            pre = combine(pre, jnp.where(row >= k, pltpu.roll(pre, k, 0), identity))
            suf = combine(suf, jnp.where(row < chunk - k, pltpu.roll(suf, chunk - k, 0), identity))
            k *= 2
        return jnp.where(fwd, pre, suf)

    b = scan(lf, jnp.add, 0.0)
    r = gi - b
    cm = scan(r, jnp.maximum, LOWEST)
    return jnp.where(j % 3 == 0, b, jnp.where(j % 3 == 1, cm, r))


def _prep_body(prev_ref, x_ref, next_ref, mtail_ref, cw_ref, cb_ref, wk_ref, wv_ref, wqt_ref,
               wvt_ref, wgc_ref, wgm_ref, bg_ref, *rest, tm, chunk, dh, meta):
    if meta:
        k_ref, v_ref, gc_ref, xs_ref = rest
    else:
        k_ref, qt_ref, vt_ref, xc_ref, gc_ref, gr_ref, xs_ref = rest
    i = pl.program_id(1)
    nt = pl.num_programs(1)
    x = x_ref[...]
    if meta:
        prev = jnp.zeros_like(prev_ref[...])
        nxt = next_ref[...]
    else:
        prev = jnp.where(i == 0, mtail_ref[...], prev_ref[...])
        nxt = jnp.where(i == nt - 1, 0.0, next_ref[...])
    xs_ref[0:SUBLANES, :] = prev
    xs_ref[SUBLANES:SUBLANES + tm, :] = x
    xs_ref[SUBLANES + tm:2 * SUBLANES + tm, :] = nxt
    acc = jnp.broadcast_to(cb_ref[...], x.shape)
    for t in range(CONV_K):
        off = SUBLANES - CONV_K // 2 + t
        acc = acc + cw_ref[t:t + 1, :] * xs_ref[off:off + tm, :]
    xc = acc * jax.nn.sigmoid(acc)
    xcb = xc.astype(BF16)
    xmb = x.astype(BF16)
    k_scale = dh ** -0.5
    if not meta:
        xc_ref[...] = xcb
        ones_rows = (lax.broadcasted_iota(jnp.int32, (V_PAD, tm), 0) == 0).astype(BF16)
    for h in range(MLSTM_HEADS):
        sl = slice(h * dh, (h + 1) * dh)
        k_ref[:, sl] = (_dot(xcb[:, sl], wk_ref[h]) * k_scale).astype(BF16)
        if meta:
            v_ref[:, sl] = _dot(xmb[:, sl], wv_ref[h]).astype(BF16)
        else:
            qt_ref[sl, :] = _dot_nt(wqt_ref[h], xcb[:, sl]).astype(BF16)
            lo = h * (dh + V_PAD)
            vt_ref[lo:lo + dh, :] = _dot_nt(wvt_ref[h], xmb[:, sl]).astype(BF16)
            vt_ref[lo + dh:lo + dh + V_PAD, :] = ones_rows
    g = _dot(xcb, wgc_ref[...]) + _dot(xmb, wgm_ref[...]) + bg_ref[...]
    gi = g[:, :LANES]
    lf = _log_sigmoid(g[:, LANES:])
    for ci in range(tm // chunk):
        rows = slice(ci * chunk, (ci + 1) * chunk)
        out = _gate_scans(gi[rows], lf[rows], chunk)
        gc_ref[rows, :] = out
        if not meta:
            gr_ref[ci] = out.T


def _prep(xm, xm_meta, cw, cb, wk, wv, wqt, wvt, wgc, wgm, bg, *, batch, tm, chunk, meta):
    dm = xm.shape[1]
    dh = dm // MLSTM_HEADS
    seq = xm.shape[0] // batch
    if meta:
        nt = 1
        n_out = batch * N_META
        x_arr = xm_meta
        x_spec = pl.BlockSpec((tm, dm), lambda b, i: (0, 0))
        prev_spec = pl.BlockSpec((SUBLANES, dm), lambda b, i: (0, 0))
        next_spec = pl.BlockSpec((SUBLANES, dm), lambda b, i: (b * (seq // SUBLANES), 0))
    else:
        nt = seq // tm
        n_out = batch * seq
        x_arr = xm
        last_blk = batch * seq // SUBLANES - 1
        x_spec = pl.BlockSpec((tm, dm), lambda b, i: (b * nt + i, 0))
        prev_spec = pl.BlockSpec(
            (SUBLANES, dm), lambda b, i: (jnp.maximum((b * nt + i) * (tm // SUBLANES) - 1, 0), 0))
        next_spec = pl.BlockSpec(
            (SUBLANES, dm), lambda b, i: (jnp.minimum((b * nt + i + 1) * (tm // SUBLANES), last_blk), 0))
    const2 = lambda b, i: (0, 0)
    const3 = lambda b, i: (0, 0, 0)
    row_spec = lambda w: pl.BlockSpec((tm, w), lambda b, i: (b * nt + i, 0))
    col_spec = lambda r: pl.BlockSpec((r, tm), lambda b, i: (0, b * nt + i))
    bf = lambda *shape: jax.ShapeDtypeStruct(shape, BF16)
    gc_shape = jax.ShapeDtypeStruct((n_out, LANES), F32)
    if meta:
        out_specs = [row_spec(dm), row_spec(dm), row_spec(LANES)]
        out_shape = [bf(n_out, dm), bf(n_out, dm), gc_shape]
    else:
        dva = dm + MLSTM_HEADS * V_PAD
        out_specs = [row_spec(dm), col_spec(dm), col_spec(dva), row_spec(dm), row_spec(LANES),
                     pl.BlockSpec((tm // chunk, LANES, chunk), lambda b, i: (b * nt + i, 0, 0))]
        out_shape = [bf(n_out, dm), bf(dm, n_out), bf(dva, n_out), bf(n_out, dm), gc_shape,
                     jax.ShapeDtypeStruct((n_out // chunk, LANES, chunk), F32)]
    return pl.pallas_call(
        functools.partial(_prep_body, tm=tm, chunk=chunk, dh=dh, meta=meta),
        grid=(batch, nt),
        in_specs=[
            prev_spec, x_spec, next_spec,
            pl.BlockSpec((SUBLANES, dm), lambda b, i: (1, 0)),
            pl.BlockSpec(cw.shape, const2),
            pl.BlockSpec(cb.shape, const2),
            pl.BlockSpec(wk.shape, const3),
            pl.BlockSpec(wv.shape, const3),
            pl.BlockSpec(wqt.shape, const3),
            pl.BlockSpec(wvt.shape, const3),
            pl.BlockSpec(wgc.shape, const2),
            pl.BlockSpec(wgm.shape, const2),
            pl.BlockSpec(bg.shape, const2),
        ],
        out_specs=out_specs,
        out_shape=out_shape,
        scratch_shapes=[pltpu.VMEM((tm + 2 * SUBLANES, dm), F32)],
        compiler_params=_params("parallel", "parallel"),
        name="mlstm_prep_meta" if meta else "mlstm_prep",
    )(xm, x_arr, xm, xm_meta, cw, cb, wk, wv, wqt, wvt, wgc, wgm, bg)


def _scan_chain(k, qt, vta, r_col, rows, c_ref, m_ref, mask, edge, dh):
    b_row, cm_row = rows
    m_old = m_ref[...]
    ct = c_ref[...]

    mt = jnp.maximum(m_old, cm_row)
    at = jnp.exp(jnp.where(mask, r_col - mt, NEG)) * _dot(k, qt)
    inter = _dot(ct.astype(BF16), qt)
    w = jnp.exp(m_old - mt)
    den = jnp.sum(at, axis=0, keepdims=True) + w * inter[dh:dh + 1, :]
    scale = 1.0 / jnp.maximum(jnp.abs(den), jnp.exp(-(b_row + mt)))
    ht = _dot(vta[:dh], (at * scale).astype(BF16)) + inter[:dh] * (w * scale)

    b_end = b_row[:, edge:edge + 1]
    m_new = jnp.maximum(b_end + m_old, b_end + cm_row[:, edge:edge + 1])
    decay = jnp.exp(b_end + m_old - m_new)
    kw = (k.astype(F32) * jnp.exp(b_end + r_col - m_new)).astype(BF16)
    c_ref[...] = decay * ct + _dot(vta, kw)
    m_ref[...] = m_new
    return ht


def _scan_body(kf_ref, qtf_ref, vtf_ref, gcf_ref, grf_ref,
               kb_ref, qtb_ref, vtb_ref, gcb_ref, grb_ref,
               km_ref, vm_ref, gcm_ref, hf_ref, hb_ref, c_ref, m_ref, *, chunk, dh):
    ci = pl.program_id(1)
    nh = MLSTM_HEADS
    dva = dh + V_PAD

    @pl.when(ci == 0)
    def _():
        gcm = gcm_ref[0]
        km = km_ref[0]
        vm = vm_ref[0]
        first = lax.broadcasted_iota(jnp.int32, (V_PAD, dh), 0) == 0
        for h in range(nh):
            sl = slice(h * dh, (h + 1) * dh)
            lane0 = h * GATE_GROUP
            b_end = gcm[N_META - 1:N_META, lane0:lane0 + 1]
            m_new = jnp.maximum(b_end, b_end + gcm[N_META - 1:N_META, lane0 + 1:lane0 + 2])
            kw = km[:, sl].astype(F32) * jnp.exp(b_end + gcm[:, lane0 + 2:lane0 + 3] - m_new)
            c_ref[h, 0:dh, :] = _dot_tn(vm[:, sl], kw.astype(BF16))
            c_ref[h, dh:dva, :] = jnp.where(first, jnp.sum(kw, axis=0, keepdims=True), 0.0)
            m_ref[h] = m_new
            c_ref[nh + h] = jnp.zeros((dva, dh), F32)
            m_ref[nh + h] = jnp.zeros((1, 1), F32)

    srow = lax.broadcasted_iota(jnp.int32, (chunk, chunk), 0)
    tcol = lax.broadcasted_iota(jnp.int32, (chunk, chunk), 1)
    for h in range(nh):
        sl = slice(h * dh, (h + 1) * dh)
        sla = slice(h * dva, (h + 1) * dva)
        lane0 = h * GATE_GROUP
        rows_f = tuple(grf_ref[0, lane0 + j:lane0 + j + 1, :] for j in (0, 1))
        rows_b = tuple(grb_ref[0, lane0 + j:lane0 + j + 1, :] for j in (3, 4))
        hf_ref[sl, :] = _scan_chain(
            kf_ref[:, sl], qtf_ref[sl, :], vtf_ref[sla, :], gcf_ref[:, lane0 + 2:lane0 + 3],
            rows_f, c_ref.at[h], m_ref.at[h], srow <= tcol, chunk - 1, dh).astype(hf_ref.dtype)
        hb_ref[sl, :] = _scan_chain(
            kb_ref[:, sl], qtb_ref[sl, :], vtb_ref[sla, :], gcb_ref[:, lane0 + 5:lane0 + 6],
            rows_b, c_ref.at[nh + h], m_ref.at[nh + h], srow >= tcol, 0, dh).astype(hb_ref.dtype)


def _scan(k, qt, vta, gc, gr, km, vm, gcm, *, batch, chunk):
    n, dm = k.shape
    dh = dm // MLSTM_HEADS
    dva = vta.shape[0]
    nc = n // batch // chunk
    fwd = lambda b, c: (b * nc + c, 0)
    bwd = lambda b, c: (b * nc + nc - 1 - c, 0)
    fwd_t = lambda b, c: (0, b * nc + c)
    bwd_t = lambda b, c: (0, b * nc + nc - 1 - c)
    fwd3 = lambda b, c: (b * nc + c, 0, 0)
    bwd3 = lambda b, c: (b * nc + nc - 1 - c, 0, 0)
    per_b = lambda b, c: (b, 0, 0)

    def specs(idx, idx_t, idx3):
        return [pl.BlockSpec((chunk, dm), idx), pl.BlockSpec((dm, chunk), idx_t),
                pl.BlockSpec((dva, chunk), idx_t), pl.BlockSpec((chunk, LANES), idx),
                pl.BlockSpec((1, LANES, chunk), idx3)]

    return pl.pallas_call(
        functools.partial(_scan_body, chunk=chunk, dh=dh),
        grid=(batch, nc),
        in_specs=specs(fwd, fwd_t, fwd3) + specs(bwd, bwd_t, bwd3) + [
            pl.BlockSpec((1, N_META, dm), per_b),
            pl.BlockSpec((1, N_META, dm), per_b),
            pl.BlockSpec((1, N_META, LANES), per_b),
        ],
        out_specs=[pl.BlockSpec((dm, chunk), fwd_t), pl.BlockSpec((dm, chunk), bwd_t)],
        out_shape=[jax.ShapeDtypeStruct((dm, n), BF16)] * 2,
        scratch_shapes=[
            pltpu.VMEM((2 * MLSTM_HEADS, dh + V_PAD, dh), F32),
            pltpu.VMEM((2 * MLSTM_HEADS, 1, 1), F32),
        ],
        compiler_params=_params("parallel", "arbitrary"),
        name="mlstm_scan",
    )(k, qt, vta, gc, gr, k, qt, vta, gc, gr,
      km.reshape(batch, N_META, dm), vm.reshape(batch, N_META, dm), gcm.reshape(batch, N_META, LANES))


def _attn_body(qt_ref, k_ref, vt_ref, km_ref, vmt_ref, o_ref, *, tk, nk):
    qt = qt_ref[...]
    dv = o_ref.shape[1]

    def scores(j):
        s = _dot(k_ref[j * tk:(j + 1) * tk, :], qt)
        return s, jnp.max(s, axis=0, keepdims=True)

    s_meta = _dot(km_ref[...], qt)
    ahead = scores(0)
    m = jnp.maximum(jnp.max(s_meta, axis=0, keepdims=True), ahead[1])
    acc = _dot(vmt_ref[...], jnp.exp2(s_meta - m).astype(BF16))
    for j in range(nk):
        s, s_max = ahead
        if j + 1 < nk:
            ahead = scores(j + 1)
        m_new = jnp.maximum(m, s_max)
        p = jnp.exp2(s - m_new).astype(BF16)
        acc = jnp.exp2(m - m_new) * acc + _dot(vt_ref[:, j * tk:(j + 1) * tk], p)
        m = m_new
    o_ref[...] = (acc[:dv] * (1.0 / acc[dv:dv + 1])).T


def _attn(qt, k, vt, km, vmt, *, batch, tq, tk):
    n = k.shape[0]
    seq = n // batch
    nq = seq // tq
    dva = vt.shape[0] // ATT_HEADS
    dv = dva - V_PAD
    dk = 2 * LANES
    return pl.pallas_call(
        functools.partial(_attn_body, tk=tk, nk=seq // tk),
        grid=(batch, ATT_HEADS, nq),
        in_specs=[
            pl.BlockSpec((dk, tq), lambda b, h, i: (h, b * nq + i)),
            pl.BlockSpec((seq, dk), lambda b, h, i: (b, h)),
            pl.BlockSpec((dva, seq), lambda b, h, i: (h, b)),
            pl.BlockSpec((N_META, dk), lambda b, h, i: (0, h)),
            pl.BlockSpec((dva, N_META), lambda b, h, i: (h, 0)),
        ],
        out_specs=pl.BlockSpec((tq, dv), lambda b, h, i: (b * nq + i, h)),
        out_shape=jax.ShapeDtypeStruct((n, ATT_HEADS * dv), F32),
        compiler_params=_params("parallel", "parallel", "arbitrary"),
        name="attn",
    )(qt, k, vt, km, vmt)


def _outproj_body(o_ref, hf_ref, hb_ref, z_ref, xc_ref, h1_ref, og_ref, gn_ref, sk_ref,
                  wa_ref, wm_ref, g_ref, b_ref, out_ref, *, dh, rows):
    for r in range(out_ref.shape[0] // rows):
        rs = slice(r * rows, (r + 1) * rows)
        y_att = _rms_norm(o_ref[rs, :], og_ref[...]).astype(BF16)
        h_sum = (hf_ref[:, rs].astype(F32) + hb_ref[:, rs].astype(F32)).T
        hs = h_sum * jax.nn.sigmoid(z_ref[rs, :])
        parts = []
        for h in range(MLSTM_HEADS):
            seg = hs[:, h * dh:(h + 1) * dh]
            mu = jnp.mean(seg, axis=-1, keepdims=True)
            sc = seg - mu
            var = jnp.mean(sc * sc, axis=-1, keepdims=True)
            parts.append(sc * lax.rsqrt(var + LN_EPS))
        hn = jnp.concatenate(parts, axis=-1)
        y_ml = (hn * gn_ref[...] + sk_ref[...] * xc_ref[rs, :].astype(F32)).astype(BF16)
        y = _dot(y_att, wa_ref[...]) + _dot(y_ml, wm_ref[...])
        out_ref[rs, :] = _layer_norm(ALPHA * h1_ref[rs, :] + y, g_ref[...], b_ref[...])


def _outproj(o, hf, hb, z, xc, h1, og, gn, sk, wo, g, b, *, tm, rows):
    n, d = h1.shape
    da = o.shape[1]
    dm = hf.shape[0]
    assert da == dm and wo.shape[0] == da + dm
    const = lambda i: (0, 0)
    row = lambda w: pl.BlockSpec((tm, w), lambda i: (i, 0))
    col = pl.BlockSpec((dm, tm), lambda i: (0, i))
    return pl.pallas_call(
        functools.partial(_outproj_body, dh=dm // MLSTM_HEADS, rows=rows),
        grid=(n // tm,),
        in_specs=[row(da), col, col, row(dm), row(dm), row(d),
                  pl.BlockSpec(og.shape, const), pl.BlockSpec(gn.shape, const), pl.BlockSpec(sk.shape, const),
                  pl.BlockSpec((da, d), lambda i: (0, 0), pipeline_mode=pl.Buffered(1)),
                  pl.BlockSpec((dm, d), lambda i: (1, 0), pipeline_mode=pl.Buffered(1)),
                  pl.BlockSpec(g.shape, const), pl.BlockSpec(b.shape, const)],
        out_specs=row(d),
        out_shape=jax.ShapeDtypeStruct((n, d), F32),
        compiler_params=_params("parallel"),
        name="outproj",
    )(o, hf, hb, z, xc, h1, og, gn, sk, wo, wo, g, b)


def _pad_lanes(w):
    return jnp.pad(w, [(0, 0)] * (w.ndim - 1) + [(0, LANES - w.shape[-1])])


def _pick(n, pref):
    for t in pref:
        if n % t == 0:
            return t
    raise ValueError(f"no tile for {n}")


def kernel(x, meta_tokens, ffn1_w_gate, ffn1_w_up, ffn1_w_down, ln1_g, ln1_b, w_in, mla_q_norm_g, mla_w_uq, mla_kv_norm_g, mla_w_ukv, attn_out_g, mlstm_conv_w, mlstm_conv_b, mlstm_w_q, mlstm_w_k, mlstm_w_v, mlstm_w_gates, mlstm_b_gates, mlstm_gn_g, mlstm_skip, w_out, ln2_g, ln2_b, ffn2_w_gate, ffn2_w_up, ffn2_w_down, ln3_g, ln3_b):
    batch, seq, d = x.shape
    assert ffn1_w_gate.shape[0] == DEPTH and meta_tokens.shape[0] == N_META
    n = batch * seq
    ql = mla_q_norm_g.shape[-1]
    kvl = mla_kv_norm_g.shape[-1]
    da = attn_out_g.shape[-1]
    dm = mlstm_gn_g.shape[-1]
    dh = dm // MLSTM_HEADS
    dv = da // ATT_HEADS
    nh = MLSTM_HEADS
    half = QK_ROPE // 2
    assert dv == LANES and dh % LANES == 0 and ql % LANES == 0 and kvl % LANES == 0
    assert seq % MLSTM_CHUNK == 0

    wi = w_in[0]
    o1, o2, o3, o4 = ql, ql + kvl, ql + kvl + QK_ROPE, ql + kvl + QK_ROPE + dm
    swap = jnp.concatenate([jnp.arange(half, QK_ROPE), jnp.arange(half)])
    w_kr = wi[:, o2:o3]
    win = jnp.concatenate([wi[:, :o2], w_kr, w_kr[:, swap], wi[:, o3:]], axis=1).astype(BF16)
    wuq = mla_w_uq[0].reshape(ql, ATT_HEADS, QK_NOPE + QK_ROPE)
    wqnt = wuq[:, :, :QK_NOPE].reshape(ql, -1).T.astype(BF16)
    w_qr = wuq[:, :, QK_NOPE:]
    wqrt = jnp.concatenate([w_qr, w_qr[:, :, swap]], axis=-1).reshape(ql, -1).T.astype(BF16)
    wukv = mla_w_ukv[0].reshape(kvl, ATT_HEADS, QK_NOPE + dv)
    wkn = wukv[:, :, :QK_NOPE].reshape(kvl, -1).astype(BF16)
    wvt = wukv[:, :, QK_NOPE:].reshape(kvl, -1).T.astype(BF16)

    wg = mlstm_w_gates[0].reshape(2 * dm, 4, nh)
    bgr = mlstm_b_gates[0].reshape(4, nh)

    def gate_tile(w4, kind_f, kind_b):
        zeros = jnp.zeros_like(w4[..., 0, :])
        cols = [w4[..., kind_f, :]] * 3 + [w4[..., kind_b, :]] * 3 + [zeros] * (GATE_GROUP - 6)
        tile = jnp.stack(cols, axis=-1).reshape(*w4.shape[:-2], nh * GATE_GROUP)
        return _pad_lanes(tile)

    wgate = jnp.concatenate([gate_tile(wg, 0, 2), gate_tile(wg, 1, 3)], axis=-1)
    wgc = wgate[:dm].astype(BF16)
    wgm = wgate[dm:].astype(BF16)
    bg = jnp.concatenate([gate_tile(bgr, 0, 2), gate_tile(bgr, 1, 3)], axis=-1)[None, :]
    cw = jnp.pad(mlstm_conv_w[0], ((0, SUBLANES - CONV_K), (0, 0)))
    cb = mlstm_conv_b[0][None, :]
    wk_m, wv_m = (w[0].astype(BF16) for w in (mlstm_w_k, mlstm_w_v))
    wqt_m, wvt_m = (jnp.swapaxes(w[0], 1, 2).astype(BF16) for w in (mlstm_w_q, mlstm_w_v))
    wo = w_out[0].astype(BF16)
    row = lambda p: p[0][None, :]

    pos = jnp.arange(N_META + seq, dtype=F32)
    inv = ROPE_BASE ** (-jnp.arange(0, QK_ROPE, 2, dtype=F32) / QK_ROPE)
    ang = pos[:, None] * inv[None, :]
    rot = jnp.concatenate([jnp.cos(ang), jnp.cos(ang), -jnp.sin(ang), jnp.sin(ang)], axis=-1)

    ffn1_f32 = [w[0] for w in (ffn1_w_gate, ffn1_w_up, ffn1_w_down)]
    ffn2_f32 = [w[0] for w in (ffn2_w_gate, ffn2_w_up, ffn2_w_down)]
    tf = _pick(ffn1_f32[0].shape[1], (512, 256, 128))
    tm_ffn = _pick(n, (1024, 512, 256))
    tm_proj = _pick(seq, (256,))

    xr = x.reshape(n, d)
    h1m, *ffn1 = _ffn_ln(meta_tokens.astype(x.dtype), *ffn1_f32, row(ln1_g), row(ln1_b),
                         tm=N_META, tf=tf, cast_weights=True)
    h1, = _ffn_ln(xr, *ffn1, row(ln1_g), row(ln1_b), tm=tm_ffn, tf=tf)
    proj_w = (win, row(mla_q_norm_g), row(mla_kv_norm_g), wqnt, wqrt, wkn, wvt)
    q_a, k_a, vt_a, xm, z, *ffn2 = _proj(h1, *proj_w, rot[N_META:], tm=tm_proj, dm=dm, side=ffn2_f32)
    _, k_am, vt_am, xm_m, _ = _proj(h1m, *proj_w, rot[:N_META], tm=N_META, dm=dm)

    prep_w = (cw, cb, wk_m, wv_m, wqt_m, wvt_m, wgc, wgm, bg)
    tm_prep = _pick(seq, (512, MLSTM_CHUNK))
    k_m, qt_m, vt_m, xc, gc, gr = _prep(
        xm, xm_m, *prep_w, batch=batch, tm=tm_prep, chunk=MLSTM_CHUNK, meta=False)
    k_mm, v_mm, gc_m = _prep(xm, xm_m, *prep_w, batch=batch, tm=N_META, chunk=N_META, meta=True)
    hf, hb = _scan(k_m, qt_m, vt_m, gc, gr, k_mm, v_mm, gc_m, batch=batch, chunk=MLSTM_CHUNK)

    o = _attn(q_a, k_a, vt_a, k_am, vt_am, batch=batch, tq=_pick(seq, (4096, 2048, 1024, 512, 256)), tk=_pick(seq, (512, 256)))

    h2 = _outproj(o, hf, hb, z, xc, h1, row(attn_out_g), row(mlstm_gn_g), row(mlstm_skip),
                  wo, row(ln2_g), row(ln2_b), tm=_pick(seq, (512, 256)), rows=_pick(seq, (256,)))
    out, = _ffn_ln(h2, *ffn2, row(ln3_g), row(ln3_b), tm=tm_ffn, tf=tf)
    return out.reshape(batch, seq, d)
```

```python
import functools

import jax
import jax.numpy as jnp
from jax import lax
from jax.experimental import pallas as pl
from jax.experimental.pallas import tpu as pltpu

F32 = jnp.float32
BF16 = jnp.bfloat16

N_META = 16
ATT_HEADS = 8
QK_NOPE = 128
QK_ROPE = 64
MLSTM_HEADS = 4
CONV_K = 5
ROPE_BASE = 10000.0
LN_EPS = 1e-5
RMS_EPS = 1e-6
DEPTH = 1
ALPHA = (2 * DEPTH) ** 0.25

LANES = 128
SUBLANES = 8
VMEM_LIMIT = 56 * 1024 * 1024

BF16_ROWS = 16
V_PAD = BF16_ROWS
MLSTM_CHUNK = 256
GATE_GROUP = 8
LOWEST = float(jnp.finfo(jnp.float32).min)
LARGEST = float(jnp.finfo(jnp.float32).max)
NEG = -1e30
LOG2E = 1.4426950408889634


def _dot(a, b):
    return jnp.dot(a, b, preferred_element_type=F32)


def _dot_nt(a, b):
    return lax.dot_general(a, b, (((1,), (1,)), ((), ())), preferred_element_type=F32)


def _dot_tn(a, b):
    return lax.dot_general(a, b, (((0,), (0,)), ((), ())), preferred_element_type=F32)


def _layer_norm(y, g, b):
    mu = jnp.mean(y, axis=-1, keepdims=True)
    yc = y - mu
    var = jnp.mean(yc * yc, axis=-1, keepdims=True)
    return yc * lax.rsqrt(var + LN_EPS) * g + b


def _rms_norm(y, g):
    return y * lax.rsqrt(jnp.mean(y * y, axis=-1, keepdims=True) + RMS_EPS) * g


def _params(*sem):
    return pltpu.CompilerParams(dimension_semantics=sem, vmem_limit_bytes=VMEM_LIMIT)


def _ffn_ln_body(x_ref, wg_ref, wu_ref, wd_ref, g_ref, b_ref, o_ref, *rest, cast_weights, n_sub):
    copies = rest[:-2]
    acc_ref, xb_ref = rest[-2:]
    i = pl.program_id(0)
    f = pl.program_id(1)
    nt = pl.num_programs(0) - 1
    cur = i % 2
    rows = o_ref.shape[0]

    @pl.when((i == 0) & (f == 0))
    def _():
        acc_ref[1] = jnp.zeros(acc_ref.shape[1:], F32)

    def norm_previous_tile():
        r0 = pl.multiple_of(jnp.minimum(f, n_sub - 1) * rows, rows)
        y = 0.5 * acc_ref[1 - cur, pl.ds(r0, rows), :]
        out = _layer_norm(y, g_ref[...], b_ref[...])
        o_ref[...] = out
        return jnp.max(out, axis=(0, 1), keepdims=True)

    @pl.when(i < nt)
    def _():
        @pl.when(f == 0)
        def _():
            x = x_ref[...]
            xb_ref[...] = x.astype(BF16)
            acc_ref[cur] = (2.0 * ALPHA) * x

        wg, wu, wd = wg_ref[...], wu_ref[...], wd_ref[...]
        if cast_weights:
            wg, wu, wd = wg.astype(BF16), wu.astype(BF16), wd.astype(BF16)
            for dst, w in zip(copies, (wg, wu, wd)):
                dst[...] = w

        norm_max = norm_previous_tile()
        xb = xb_ref[...]
        half = wg.shape[1] // 2
        for c in range(2):
            cs = slice(c * half, (c + 1) * half)
            gate = _dot(xb, wg[:, cs])
            up = _dot(xb, wu[:, cs])
            act = gate * jax.nn.sigmoid(gate) * up
            act = jnp.where(norm_max > LARGEST, 0.0, act).astype(BF16)
            acc_ref[cur] += _dot(act, wd[cs, :])

    @pl.when(i == nt)
    def _():
        norm_previous_tile()


def _ffn_ln(x, wg, wu, wd, g, b, *, tm, tf, cast_weights=False):
    n, d = x.shape
    dff = wg.shape[1]
    nf = dff // tf
    nt = n // tm
    assert n % tm == 0 and dff % tf == 0
    n_sub = 1
    while 2 * n_sub <= nf and tm % (2 * n_sub * SUBLANES) == 0:
        n_sub *= 2
    row_i = lambda i: jnp.minimum(i, nt - 1)
    col_f = lambda i, f: jnp.where(i < nt, f, nf - 1)
    in_specs = [
        pl.BlockSpec((tm, d), lambda i, f: (row_i(i), 0)),
        pl.BlockSpec((d, tf), lambda i, f: (0, col_f(i, f))),
        pl.BlockSpec((d, tf), lambda i, f: (0, col_f(i, f))),
        pl.BlockSpec((tf, d), lambda i, f: (col_f(i, f), 0)),
        pl.BlockSpec((1, d), lambda i, f: (0, 0)),
        pl.BlockSpec((1, d), lambda i, f: (0, 0)),
    ]
    out_specs = [pl.BlockSpec(
        (tm // n_sub, d),
        lambda i, f: (jnp.where(i == 0, 0, (i - 1) * n_sub + jnp.minimum(f, n_sub - 1)), 0))]
    out_shape = [jax.ShapeDtypeStruct((n, d), F32)]
    if cast_weights:
        out_specs += in_specs[1:4]
        out_shape += [jax.ShapeDtypeStruct(w.shape, BF16) for w in (wg, wu, wd)]
    return pl.pallas_call(
        functools.partial(_ffn_ln_body, cast_weights=cast_weights, n_sub=n_sub),
        grid=(nt + 1, nf),
        in_specs=in_specs,
        out_specs=out_specs,
        out_shape=out_shape,
        scratch_shapes=[pltpu.VMEM((2, tm, d), F32), pltpu.VMEM((tm, d), BF16)],
        compiler_params=_params("arbitrary", "arbitrary"),
        name="ffn_ln",
    )(x, wg, wu, wd, g, b)


def _proj_body(h_ref, win_ref, qg_ref, kvg_ref, wqnt_ref, wqrt_ref, wkn_ref, wvt_ref,
               rot_ref, rott_ref, *rest, ql, kvl, dm, scale, n_side):
    side_in = rest[:n_side]
    qt_ref, k_ref, vt_ref, xm_ref, z_ref = rest[n_side:n_side + 5]
    for src, dst in zip(side_in, rest[n_side + 5:]):
        dst[...] = src[...].astype(BF16)
    hb = h_ref[...].astype(BF16)
    u = _dot(hb, win_ref[...])
    o1 = ql
    o2 = o1 + kvl
    o3 = o2 + LANES
    o4 = o3 + dm
    xm_ref[...] = u[:, o3:o4]
    z_ref[...] = u[:, o4:]
    rott = rott_ref[...]
    row = lax.broadcasted_iota(jnp.int32, rott.shape, 0)
    q_rope_scale = jnp.where(row < QK_ROPE, scale, 0.0)
    qn = _rms_norm(u[:, :o1], qg_ref[...]).astype(BF16)
    qat = _dot_nt(wqnt_ref[...], qn)
    qrt = _dot_nt(wqrt_ref[...], qn)
    kvn = _rms_norm(u[:, o1:o2], kvg_ref[...]).astype(BF16)
    kn = _dot(kvn, wkn_ref[...])
    vt = _dot_nt(wvt_ref[...], kvn).astype(BF16)
    dv = vt.shape[0] // ATT_HEADS
    ones_rows = (lax.broadcasted_iota(jnp.int32, (V_PAD, vt.shape[1]), 0) == 0).astype(BF16)
    for h in range(ATT_HEADS):
        lo = h * (dv + V_PAD)
        vt_ref[lo:lo + dv, :] = vt[h * dv:(h + 1) * dv]
        vt_ref[lo + dv:lo + dv + V_PAD, :] = ones_rows
    t = u[:, o2:o3] * rot_ref[...]
    k_rope = (t + pltpu.roll(t, LANES // 2, 1)).astype(BF16)
    for h in range(ATT_HEADS):
        sl = slice(h * LANES, (h + 1) * LANES)
        lo = 2 * h * LANES
        qt_ref[lo:lo + LANES, :] = (qat[sl] * scale).astype(BF16)
        t = qrt[sl] * rott
        qt_ref[lo + LANES:lo + 2 * LANES, :] = ((t + pltpu.roll(t, LANES // 2, 0)) * q_rope_scale).astype(BF16)
        k_ref[:, lo:lo + LANES] = kn[:, sl].astype(BF16)
        k_ref[:, lo + LANES:lo + 2 * LANES] = k_rope


def _win_body(a_ref, b_ref, o_ref, *, swap_tile):
    b = b_ref[...]
    half = b.shape[0] // 2
    b = jnp.where(pl.program_id(0) == swap_tile, jnp.concatenate([b[half:], b[:half]], axis=0), b)
    o_ref[...] = jnp.concatenate([a_ref[...], b], axis=0).T.astype(BF16)


def _win(w_in_t, rope_row):
    cols, d = w_in_t.shape
    blk = QK_ROPE
    kr = rope_row // blk
    assert rope_row % blk == 0 and cols % blk == 0 and kr % 2 == 0 and 2 * blk == LANES
    n_tiles = (cols + blk) // LANES
    src = lambda j: jnp.where(j <= kr, j, j - 1)
    return pl.pallas_call(
        functools.partial(_win_body, swap_tile=kr // 2),
        grid=(n_tiles,),
        in_specs=[pl.BlockSpec((blk, d), lambda t: (src(2 * t), 0)),
                  pl.BlockSpec((blk, d), lambda t: (src(2 * t + 1), 0))],
        out_specs=pl.BlockSpec((d, LANES), lambda t: (0, t)),
        out_shape=jax.ShapeDtypeStruct((d, cols + blk), BF16),
        compiler_params=_params("parallel"),
        name="win_prep",
    )(w_in_t, w_in_t)


def _slice_spec(shape, steps):
    rows, cols = shape
    for csplit in (1, 2, 4, 8):
        rblocks = steps // csplit
        if (steps % csplit == 0 and rows % (rblocks * BF16_ROWS) == 0 and cols % (csplit * LANES) == 0):
            return pl.BlockSpec((rows // rblocks, cols // csplit), lambda i: (i // csplit, i % csplit))
    raise ValueError(f"cannot slice {shape} over {steps} steps")


def _proj(h, win, qg, kvg, wqnt, wqrt, wkn, wvt, rot, *, tm, dm, side=()):
    n, d = h.shape
    ql = qg.shape[1]
    kvl = kvg.shape[1]
    da = wvt.shape[0] + ATT_HEADS * V_PAD
    npos = rot.shape[0] // tm
    hq = ATT_HEADS * 2 * LANES
    const = lambda i: (0, 0)
    scale = float((QK_NOPE + QK_ROPE) ** -0.5 * LOG2E)
    side_specs = [_slice_spec(w.shape, n // tm) for w in side]
    return pl.pallas_call(
        functools.partial(_proj_body, ql=ql, kvl=kvl, dm=dm, scale=scale, n_side=len(side)),
        grid=(n // tm,),
        in_specs=[
            pl.BlockSpec((tm, d), lambda i: (i, 0)),
            pl.BlockSpec(win.shape, const),
            pl.BlockSpec(qg.shape, const),
            pl.BlockSpec(kvg.shape, const),
            pl.BlockSpec(wqnt.shape, const),
            pl.BlockSpec(wqrt.shape, const),
            pl.BlockSpec(wkn.shape, const),
            pl.BlockSpec(wvt.shape, const),
            pl.BlockSpec((tm, LANES), lambda i: (i % npos, 0)),
            pl.BlockSpec((LANES, tm), lambda i: (0, i % npos)),
        ] + side_specs,
        out_specs=[
            pl.BlockSpec((hq, tm), lambda i: (0, i)),
            pl.BlockSpec((tm, hq), lambda i: (i, 0)),
            pl.BlockSpec((da, tm), lambda i: (0, i)),
            pl.BlockSpec((tm, dm), lambda i: (i, 0)),
            pl.BlockSpec((tm, dm), lambda i: (i, 0)),
        ] + side_specs,
        out_shape=[
            jax.ShapeDtypeStruct((hq, n), BF16),
            jax.ShapeDtypeStruct((n, hq), BF16),
            jax.ShapeDtypeStruct((da, n), BF16),
            jax.ShapeDtypeStruct((n, dm), F32),
            jax.ShapeDtypeStruct((n, dm), F32),
        ] + [jax.ShapeDtypeStruct(w.shape, BF16) for w in side],
        compiler_params=_params("parallel"),
        name="proj",
    )(h, win, qg, kvg, wqnt, wqrt, wkn, wvt, rot, rot.T, *side)


def _log_sigmoid(x):
    return jnp.minimum(x, 0.0) - jnp.log1p(jnp.exp(-jnp.abs(x)))


def _gate_scans(gi, lf, chunk):
    row = lax.broadcasted_iota(jnp.int32, (chunk, LANES), 0)
    lane = lax.broadcasted_iota(jnp.int32, (chunk, LANES), 1)
    j = lane % GATE_GROUP
    fwd = j < 3

    def scan(val, combine, identity):
        pre = val
        suf = val
        k = 1
        while k < chunk:
            pre = combine(pre, jnp.where(row >= k, pltpu.roll(pre, k, 0), identity))
            suf = combine(suf, jnp.where(row < chunk - k, pltpu.roll(suf, chunk - k, 0), identity))
            k *= 2
        return jnp.where(fwd, pre, suf)

    b = scan(lf, jnp.add, 0.0)
    r = gi - b
    cm = scan(r, jnp.maximum, LOWEST)
    return jnp.where(j % 3 == 0, b, jnp.where(j % 3 == 1, cm, r))


def _prep_body(prev_ref, x_ref, next_ref, mtail_ref, cw_ref, cb_ref, wk_ref, wv_ref, wqt_ref,
               wvt_ref, wgc_ref, wgm_ref, bg_ref, *rest, tm, chunk, dh, meta):
    if meta:
        k_ref, v_ref, gc_ref, xs_ref = rest
    else:
        k_ref, qt_ref, vt_ref, xc_ref, gc_ref, gr_ref, xs_ref = rest
    i = pl.program_id(1)
    nt = pl.num_programs(1)
    x = x_ref[...]
    if meta:
        prev = jnp.zeros_like(prev_ref[...])
        nxt = next_ref[...]
    else:
        prev = jnp.where(i == 0, mtail_ref[...], prev_ref[...])
        nxt = jnp.where(i == nt - 1, 0.0, next_ref[...])
    xs_ref[0:SUBLANES, :] = prev
    xs_ref[SUBLANES:SUBLANES + tm, :] = x
    xs_ref[SUBLANES + tm:2 * SUBLANES + tm, :] = nxt
    acc = jnp.broadcast_to(cb_ref[...], x.shape)
    for t in range(CONV_K):
        off = SUBLANES - CONV_K // 2 + t
        acc = acc + cw_ref[t:t + 1, :] * xs_ref[off:off + tm, :]
    xc = acc * jax.nn.sigmoid(acc)
    xcb = xc.astype(BF16)
    xmb = x.astype(BF16)
    k_scale = dh ** -0.5
    if not meta:
        xc_ref[...] = xcb
        ones_rows = (lax.broadcasted_iota(jnp.int32, (V_PAD, tm), 0) == 0).astype(BF16)
    for h in range(MLSTM_HEADS):
        sl = slice(h * dh, (h + 1) * dh)
        k_ref[:, sl] = (_dot(xcb[:, sl], wk_ref[h]) * k_scale).astype(BF16)
        if meta:
            v_ref[:, sl] = _dot(xmb[:, sl], wv_ref[h]).astype(BF16)
        else:
            qt_ref[sl, :] = _dot_nt(wqt_ref[h], xcb[:, sl]).astype(BF16)
            lo = h * (dh + V_PAD)
            vt_ref[lo:lo + dh, :] = _dot_nt(wvt_ref[h], xmb[:, sl]).astype(BF16)
            vt_ref[lo + dh:lo + dh + V_PAD, :] = ones_rows
    g = _dot(xcb, wgc_ref[...]) + _dot(xmb, wgm_ref[...]) + bg_ref[...]
    gi = g[:, :LANES]
    lf = _log_sigmoid(g[:, LANES:])
    for ci in range(tm // chunk):
        rows = slice(ci * chunk, (ci + 1) * chunk)
        out = _gate_scans(gi[rows], lf[rows], chunk)
        gc_ref[rows, :] = out
        if not meta:
            gr_ref[ci] = out.T


def _prep(xm, xm_meta, cw, cb, wk, wv, wqt, wvt, wgc, wgm, bg, *, batch, tm, chunk, meta):
    dm = xm.shape[1]
    dh = dm // MLSTM_HEADS
    seq = xm.shape[0] // batch
    if meta:
        nt = 1
        n_out = batch * N_META
        x_arr = xm_meta
        x_spec = pl.BlockSpec((tm, dm), lambda b, i: (0, 0))
        prev_spec = pl.BlockSpec((SUBLANES, dm), lambda b, i: (0, 0))
        next_spec = pl.BlockSpec((SUBLANES, dm), lambda b, i: (b * (seq // SUBLANES), 0))
    else:
        nt = seq // tm
        n_out = batch * seq
        x_arr = xm
        last_blk = batch * seq // SUBLANES - 1
        x_spec = pl.BlockSpec((tm, dm), lambda b, i: (b * nt + i, 0))
        prev_spec = pl.BlockSpec(
            (SUBLANES, dm), lambda b, i: (jnp.maximum((b * nt + i) * (tm // SUBLANES) - 1, 0), 0))
        next_spec = pl.BlockSpec(
            (SUBLANES, dm), lambda b, i: (jnp.minimum((b * nt + i + 1) * (tm // SUBLANES), last_blk), 0))
    const2 = lambda b, i: (0, 0)
    const3 = lambda b, i: (0, 0, 0)
    row_spec = lambda w: pl.BlockSpec((tm, w), lambda b, i: (b * nt + i, 0))
    col_spec = lambda r: pl.BlockSpec((r, tm), lambda b, i: (0, b * nt + i))
    bf = lambda *shape: jax.ShapeDtypeStruct(shape, BF16)
    gc_shape = jax.ShapeDtypeStruct((n_out, LANES), F32)
    if meta:
        out_specs = [row_spec(dm), row_spec(dm), row_spec(LANES)]
        out_shape = [bf(n_out, dm), bf(n_out, dm), gc_shape]
    else:
        dva = dm + MLSTM_HEADS * V_PAD
        out_specs = [row_spec(dm), col_spec(dm), col_spec(dva), row_spec(dm), row_spec(LANES),
                     pl.BlockSpec((tm // chunk, LANES, chunk), lambda b, i: (b * nt + i, 0, 0))]
        out_shape = [bf(n_out, dm), bf(dm, n_out), bf(dva, n_out), bf(n_out, dm), gc_shape,
                     jax.ShapeDtypeStruct((n_out // chunk, LANES, chunk), F32)]
    return pl.pallas_call(
        functools.partial(_prep_body, tm=tm, chunk=chunk, dh=dh, meta=meta),
        grid=(batch, nt),
        in_specs=[
            prev_spec, x_spec, next_spec,
            pl.BlockSpec((SUBLANES, dm), lambda b, i: (1, 0)),
            pl.BlockSpec(cw.shape, const2),
            pl.BlockSpec(cb.shape, const2),
            pl.BlockSpec(wk.shape, const3),
            pl.BlockSpec(wv.shape, const3),
            pl.BlockSpec(wqt.shape, const3),
            pl.BlockSpec(wvt.shape, const3),
            pl.BlockSpec(wgc.shape, const2),
            pl.BlockSpec(wgm.shape, const2),
            pl.BlockSpec(bg.shape, const2),
        ],
        out_specs=out_specs,
        out_shape=out_shape,
        scratch_shapes=[pltpu.VMEM((tm + 2 * SUBLANES, dm), F32)],
        compiler_params=_params("parallel", "parallel"),
        name="mlstm_prep_meta" if meta else "mlstm_prep",
    )(xm, x_arr, xm, xm_meta, cw, cb, wk, wv, wqt, wvt, wgc, wgm, bg)


def _scan_chain(k, qt, vta, r_col, rows, c_ref, m_ref, mask, edge, dh):
    b_row, cm_row = rows
    m_old = m_ref[...]
    ct = c_ref[...]

    mt = jnp.maximum(m_old, cm_row)
    at = jnp.exp(jnp.where(mask, r_col - mt, NEG)) * _dot(k, qt)
    inter = _dot(ct.astype(BF16), qt)
    w = jnp.exp(m_old - mt)
    den = jnp.sum(at, axis=0, keepdims=True) + w * inter[dh:dh + 1, :]
    scale = 1.0 / jnp.maximum(jnp.abs(den), jnp.exp(-(b_row + mt)))
    ht = _dot(vta[:dh], (at * scale).astype(BF16)) + inter[:dh] * (w * scale)

    b_end = b_row[:, edge:edge + 1]
    m_new = jnp.maximum(b_end + m_old, b_end + cm_row[:, edge:edge + 1])
    decay = jnp.exp(b_end + m_old - m_new)
    kw = (k.astype(F32) * jnp.exp(b_end + r_col - m_new)).astype(BF16)
    c_ref[...] = decay * ct + _dot(vta, kw)
    m_ref[...] = m_new
    return ht


def _scan_body(kf_ref, qtf_ref, vtf_ref, gcf_ref, grf_ref,
               kb_ref, qtb_ref, vtb_ref, gcb_ref, grb_ref,
               km_ref, vm_ref, gcm_ref, hf_ref, hb_ref, c_ref, m_ref, *, chunk, dh):
    ci = pl.program_id(1)
    nh = MLSTM_HEADS
    dva = dh + V_PAD

    @pl.when(ci == 0)
    def _():
        gcm = gcm_ref[0]
        km = km_ref[0]
        vm = vm_ref[0]
        first = lax.broadcasted_iota(jnp.int32, (V_PAD, dh), 0) == 0
        for h in range(nh):
            sl = slice(h * dh, (h + 1) * dh)
            lane0 = h * GATE_GROUP
            b_end = gcm[N_META - 1:N_META, lane0:lane0 + 1]
            m_new = jnp.maximum(b_end, b_end + gcm[N_META - 1:N_META, lane0 + 1:lane0 + 2])
            kw = km[:, sl].astype(F32) * jnp.exp(b_end + gcm[:, lane0 + 2:lane0 + 3] - m_new)
            c_ref[h, 0:dh, :] = _dot_tn(vm[:, sl], kw.astype(BF16))
            c_ref[h, dh:dva, :] = jnp.where(first, jnp.sum(kw, axis=0, keepdims=True), 0.0)
            m_ref[h] = m_new
            c_ref[nh + h] = jnp.zeros((dva, dh), F32)
            m_ref[nh + h] = jnp.zeros((1, 1), F32)

    srow = lax.broadcasted_iota(jnp.int32, (chunk, chunk), 0)
    tcol = lax.broadcasted_iota(jnp.int32, (chunk, chunk), 1)
    for h in range(nh):
        sl = slice(h * dh, (h + 1) * dh)
        sla = slice(h * dva, (h + 1) * dva)
        lane0 = h * GATE_GROUP
        rows_f = tuple(grf_ref[0, lane0 + j:lane0 + j + 1, :] for j in (0, 1))
        rows_b = tuple(grb_ref[0, lane0 + j:lane0 + j + 1, :] for j in (3, 4))
        hf_ref[sl, :] = _scan_chain(
            kf_ref[:, sl], qtf_ref[sl, :], vtf_ref[sla, :], gcf_ref[:, lane0 + 2:lane0 + 3],
            rows_f, c_ref.at[h], m_ref.at[h], srow <= tcol, chunk - 1, dh).astype(hf_ref.dtype)
        hb_ref[sl, :] = _scan_chain(
            kb_ref[:, sl], qtb_ref[sl, :], vtb_ref[sla, :], gcb_ref[:, lane0 + 5:lane0 + 6],
            rows_b, c_ref.at[nh + h], m_ref.at[nh + h], srow >= tcol, 0, dh).astype(hb_ref.dtype)


def _scan(k, qt, vta, gc, gr, km, vm, gcm, *, batch, chunk):
    n, dm = k.shape
    dh = dm // MLSTM_HEADS
    dva = vta.shape[0]
    nc = n // batch // chunk
    fwd = lambda b, c: (b * nc + c, 0)
    bwd = lambda b, c: (b * nc + nc - 1 - c, 0)
    fwd_t = lambda b, c: (0, b * nc + c)
    bwd_t = lambda b, c: (0, b * nc + nc - 1 - c)
    fwd3 = lambda b, c: (b * nc + c, 0, 0)
    bwd3 = lambda b, c: (b * nc + nc - 1 - c, 0, 0)
    per_b = lambda b, c: (b, 0, 0)

    def specs(idx, idx_t, idx3):
        return [pl.BlockSpec((chunk, dm), idx), pl.BlockSpec((dm, chunk), idx_t),
                pl.BlockSpec((dva, chunk), idx_t), pl.BlockSpec((chunk, LANES), idx),
                pl.BlockSpec((1, LANES, chunk), idx3)]

    return pl.pallas_call(
        functools.partial(_scan_body, chunk=chunk, dh=dh),
        grid=(batch, nc),
        in_specs=specs(fwd, fwd_t, fwd3) + specs(bwd, bwd_t, bwd3) + [
            pl.BlockSpec((1, N_META, dm), per_b),
            pl.BlockSpec((1, N_META, dm), per_b),
            pl.BlockSpec((1, N_META, LANES), per_b),
        ],
        out_specs=[pl.BlockSpec((dm, chunk), fwd_t), pl.BlockSpec((dm, chunk), bwd_t)],
        out_shape=[jax.ShapeDtypeStruct((dm, n), BF16)] * 2,
        scratch_shapes=[
            pltpu.VMEM((2 * MLSTM_HEADS, dh + V_PAD, dh), F32),
            pltpu.VMEM((2 * MLSTM_HEADS, 1, 1), F32),
        ],
        compiler_params=_params("parallel", "arbitrary"),
        name="mlstm_scan",
    )(k, qt, vta, gc, gr, k, qt, vta, gc, gr,
      km.reshape(batch, N_META, dm), vm.reshape(batch, N_META, dm), gcm.reshape(batch, N_META, LANES))


def _attn_body(qt_ref, k_ref, vt_ref, km_ref, vmt_ref, o_ref, *, tk, nk):
    qt = qt_ref[...]
    dv = o_ref.shape[1]

    def scores(j):
        s = _dot(k_ref[j * tk:(j + 1) * tk, :], qt)
        return s, jnp.max(s, axis=0, keepdims=True)

    s_meta = _dot(km_ref[...], qt)
    ahead = scores(0)
    m = jnp.maximum(jnp.max(s_meta, axis=0, keepdims=True), ahead[1])
    acc = _dot(vmt_ref[...], jnp.exp2(s_meta - m).astype(BF16))
    for j in range(nk):
        s, s_max = ahead
        if j + 1 < nk:
            ahead = scores(j + 1)
        m_new = jnp.maximum(m, s_max)
        p = jnp.exp2(s - m_new).astype(BF16)
        acc = jnp.exp2(m - m_new) * acc + _dot(vt_ref[:, j * tk:(j + 1) * tk], p)
        m = m_new
    o_ref[...] = (acc[:dv] * (1.0 / acc[dv:dv + 1])).T


def _attn(qt, k, vt, km, vmt, *, batch, tq, tk):
    n = k.shape[0]
    seq = n // batch
    nq = seq // tq
    dva = vt.shape[0] // ATT_HEADS
    dv = dva - V_PAD
    dk = 2 * LANES
    return pl.pallas_call(
        functools.partial(_attn_body, tk=tk, nk=seq // tk),
        grid=(batch, ATT_HEADS, nq),
        in_specs=[
            pl.BlockSpec((dk, tq), lambda b, h, i: (h, b * nq + i)),
            pl.BlockSpec((seq, dk), lambda b, h, i: (b, h)),
            pl.BlockSpec((dva, seq), lambda b, h, i: (h, b)),
            pl.BlockSpec((N_META, dk), lambda b, h, i: (0, h)),
            pl.BlockSpec((dva, N_META), lambda b, h, i: (h, 0)),
        ],
        out_specs=pl.BlockSpec((tq, dv), lambda b, h, i: (b * nq + i, h)),
        out_shape=jax.ShapeDtypeStruct((n, ATT_HEADS * dv), F32),
        compiler_params=_params("parallel", "parallel", "arbitrary"),
        name="attn",
    )(qt, k, vt, km, vmt)


def _outproj_body(o_ref, hf_ref, hb_ref, z_ref, xc_ref, h1_ref, og_ref, gn_ref, sk_ref,
                  wa_ref, wm_ref, g_ref, b_ref, out_ref, *, dh, rows):
    for r in range(out_ref.shape[0] // rows):
        rs = slice(r * rows, (r + 1) * rows)
        y_att = _rms_norm(o_ref[rs, :], og_ref[...]).astype(BF16)
        h_sum = (hf_ref[:, rs].astype(F32) + hb_ref[:, rs].astype(F32)).T
        hs = h_sum * jax.nn.sigmoid(z_ref[rs, :])
        parts = []
        for h in range(MLSTM_HEADS):
            seg = hs[:, h * dh:(h + 1) * dh]
            mu = jnp.mean(seg, axis=-1, keepdims=True)
            sc = seg - mu
            var = jnp.mean(sc * sc, axis=-1, keepdims=True)
            parts.append(sc * lax.rsqrt(var + LN_EPS))
        hn = jnp.concatenate(parts, axis=-1)
        y_ml = (hn * gn_ref[...] + sk_ref[...] * xc_ref[rs, :].astype(F32)).astype(BF16)
        y = _dot(y_att, wa_ref[...]) + _dot(y_ml, wm_ref[...])
        out_ref[rs, :] = _layer_norm(ALPHA * h1_ref[rs, :] + y, g_ref[...], b_ref[...])


def _outproj(o, hf, hb, z, xc, h1, og, gn, sk, wo, g, b, *, tm, rows):
    n, d = h1.shape
    da = o.shape[1]
    dm = hf.shape[0]
    assert da == dm and wo.shape[0] == da + dm
    const = lambda i: (0, 0)
    row = lambda w: pl.BlockSpec((tm, w), lambda i: (i, 0))
    col = pl.BlockSpec((dm, tm), lambda i: (0, i))
    return pl.pallas_call(
        functools.partial(_outproj_body, dh=dm // MLSTM_HEADS, rows=rows),
        grid=(n // tm,),
        in_specs=[row(da), col, col, row(dm), row(dm), row(d),
                  pl.BlockSpec(og.shape, const), pl.BlockSpec(gn.shape, const), pl.BlockSpec(sk.shape, const),
                  pl.BlockSpec((da, d), lambda i: (0, 0), pipeline_mode=pl.Buffered(1)),
                  pl.BlockSpec((dm, d), lambda i: (1, 0), pipeline_mode=pl.Buffered(1)),
                  pl.BlockSpec(g.shape, const), pl.BlockSpec(b.shape, const)],
        out_specs=row(d),
        out_shape=jax.ShapeDtypeStruct((n, d), F32),
        compiler_params=_params("parallel"),
        name="outproj",
    )(o, hf, hb, z, xc, h1, og, gn, sk, wo, wo, g, b)


def _pad_lanes(w):
    return jnp.pad(w, [(0, 0)] * (w.ndim - 1) + [(0, LANES - w.shape[-1])])


def _pick(n, pref):
    for t in pref:
        if n % t == 0:
            return t
    raise ValueError(f"no tile for {n}")


def kernel(x, meta_tokens, ffn1_w_gate, ffn1_w_up, ffn1_w_down, ln1_g, ln1_b, w_in, mla_q_norm_g, mla_w_uq, mla_kv_norm_g, mla_w_ukv, attn_out_g, mlstm_conv_w, mlstm_conv_b, mlstm_w_q, mlstm_w_k, mlstm_w_v, mlstm_w_gates, mlstm_b_gates, mlstm_gn_g, mlstm_skip, w_out, ln2_g, ln2_b, ffn2_w_gate, ffn2_w_up, ffn2_w_down, ln3_g, ln3_b):
    batch, seq, d = x.shape
    assert ffn1_w_gate.shape[0] == DEPTH and meta_tokens.shape[0] == N_META
    n = batch * seq
    ql = mla_q_norm_g.shape[-1]
    kvl = mla_kv_norm_g.shape[-1]
    da = attn_out_g.shape[-1]
    dm = mlstm_gn_g.shape[-1]
    dh = dm // MLSTM_HEADS
    dv = da // ATT_HEADS
    nh = MLSTM_HEADS
    half = QK_ROPE // 2
    assert dv == LANES and dh % LANES == 0 and ql % LANES == 0 and kvl % LANES == 0
    assert seq % MLSTM_CHUNK == 0

    swap = jnp.concatenate([jnp.arange(half, QK_ROPE), jnp.arange(half)])
    win = _win(jnp.swapaxes(w_in[0], 0, 1), ql + kvl)
    wuq = mla_w_uq[0].reshape(ql, ATT_HEADS, QK_NOPE + QK_ROPE)
    wqnt = wuq[:, :, :QK_NOPE].reshape(ql, -1).T.astype(BF16)
    w_qr = wuq[:, :, QK_NOPE:]
    wqrt = jnp.concatenate([w_qr, w_qr[:, :, swap]], axis=-1).reshape(ql, -1).T.astype(BF16)
    wukv = mla_w_ukv[0].reshape(kvl, ATT_HEADS, QK_NOPE + dv)
    wkn = wukv[:, :, :QK_NOPE].reshape(kvl, -1).astype(BF16)
    wvt = wukv[:, :, QK_NOPE:].reshape(kvl, -1).T.astype(BF16)

    wg = mlstm_w_gates[0].reshape(2 * dm, 4, nh)
    bgr = mlstm_b_gates[0].reshape(4, nh)

    def gate_tile(w4, kind_f, kind_b):
        zeros = jnp.zeros_like(w4[..., 0, :])
        cols = [w4[..., kind_f, :]] * 3 + [w4[..., kind_b, :]] * 3 + [zeros] * (GATE_GROUP - 6)
        tile = jnp.stack(cols, axis=-1).reshape(*w4.shape[:-2], nh * GATE_GROUP)
        return _pad_lanes(tile)

    wgate = jnp.concatenate([gate_tile(wg, 0, 2), gate_tile(wg, 1, 3)], axis=-1)
    wgc = wgate[:dm].astype(BF16)
    wgm = wgate[dm:].astype(BF16)
    bg = jnp.concatenate([gate_tile(bgr, 0, 2), gate_tile(bgr, 1, 3)], axis=-1)[None, :]
    cw = jnp.pad(mlstm_conv_w[0], ((0, SUBLANES - CONV_K), (0, 0)))
    cb = mlstm_conv_b[0][None, :]
    wk_m, wv_m = (w[0].astype(BF16) for w in (mlstm_w_k, mlstm_w_v))
    wqt_m, wvt_m = (jnp.swapaxes(w[0], 1, 2).astype(BF16) for w in (mlstm_w_q, mlstm_w_v))
    row = lambda p: p[0][None, :]

    pos = jnp.arange(N_META + seq, dtype=F32)
    inv = ROPE_BASE ** (-jnp.arange(0, QK_ROPE, 2, dtype=F32) / QK_ROPE)
    ang = pos[:, None] * inv[None, :]
    rot = jnp.concatenate([jnp.cos(ang), jnp.cos(ang), -jnp.sin(ang), jnp.sin(ang)], axis=-1)

    ffn1_f32 = [w[0] for w in (ffn1_w_gate, ffn1_w_up, ffn1_w_down)]
    ffn2_f32 = [w[0] for w in (ffn2_w_gate, ffn2_w_up, ffn2_w_down)]
    tf = _pick(ffn1_f32[0].shape[1], (512, 256, 128))
    tm_ffn = _pick(n, (1024, 512, 256))
    tm_proj = _pick(seq, (256,))

    xr = x.reshape(n, d)
    h1m, *ffn1 = _ffn_ln(meta_tokens.astype(x.dtype), *ffn1_f32, row(ln1_g), row(ln1_b),
                         tm=N_META, tf=tf, cast_weights=True)
    h1, = _ffn_ln(xr, *ffn1, row(ln1_g), row(ln1_b), tm=tm_ffn, tf=tf)
    proj_w = (win, row(mla_q_norm_g), row(mla_kv_norm_g), wqnt, wqrt, wkn, wvt)
    q_a, k_a, vt_a, xm, z, *ffn2, wo = _proj(
        h1, *proj_w, rot[N_META:], tm=tm_proj, dm=dm, side=ffn2_f32 + [w_out[0]])
    _, k_am, vt_am, xm_m, _ = _proj(h1m, *proj_w, rot[:N_META], tm=N_META, dm=dm)

    prep_w = (cw, cb, wk_m, wv_m, wqt_m, wvt_m, wgc, wgm, bg)
    tm_prep = _pick(seq, (512, MLSTM_CHUNK))
    k_m, qt_m, vt_m, xc, gc, gr = _prep(
        xm, xm_m, *prep_w, batch=batch, tm=tm_prep, chunk=MLSTM_CHUNK, meta=False)
    k_mm, v_mm, gc_m = _prep(xm, xm_m, *prep_w, batch=batch, tm=N_META, chunk=N_META, meta=True)
    hf, hb = _scan(k_m, qt_m, vt_m, gc, gr, k_mm, v_mm, gc_m, batch=batch, chunk=MLSTM_CHUNK)

    o = _attn(q_a, k_a, vt_a, k_am, vt_am, batch=batch, tq=_pick(seq, (4096, 2048, 1024, 512, 256)), tk=_pick(seq, (512, 256)))

    h2 = _outproj(o, hf, hb, z, xc, h1, row(attn_out_g), row(mlstm_gn_g), row(mlstm_skip),
                  wo, row(ln2_g), row(ln2_b), tm=_pick(seq, (512, 256)), rows=_pick(seq, (256,)))
    out, = _ffn_ln(h2, *ffn2, row(ln3_g), row(ln3_b), tm=tm_ffn, tf=tf)
    return out.reshape(batch, seq, d)
```

```python
import functools

import jax
import jax.numpy as jnp
from jax import lax
from jax.experimental import pallas as pl
from jax.experimental.pallas import tpu as pltpu

F32 = jnp.float32
BF16 = jnp.bfloat16

N_META = 16
ATT_HEADS = 8
QK_NOPE = 128
QK_ROPE = 64
MLSTM_HEADS = 4
CONV_K = 5
ROPE_BASE = 10000.0
LN_EPS = 1e-5
RMS_EPS = 1e-6
DEPTH = 1
ALPHA = (2 * DEPTH) ** 0.25

LANES = 128
SUBLANES = 8
VMEM_LIMIT = 56 * 1024 * 1024

BF16_ROWS = 16
V_PAD = BF16_ROWS
MLSTM_CHUNK = 256
GATE_GROUP = 8
LOWEST = float(jnp.finfo(jnp.float32).min)
LARGEST = float(jnp.finfo(jnp.float32).max)
NEG = -1e30
LOG2E = 1.4426950408889634


def _dot(a, b):
    return jnp.dot(a, b, preferred_element_type=F32)


def _dot_nt(a, b):
    return lax.dot_general(a, b, (((1,), (1,)), ((), ())), preferred_element_type=F32)


def _dot_tn(a, b):
    return lax.dot_general(a, b, (((0,), (0,)), ((), ())), preferred_element_type=F32)


def _layer_norm(y, g, b):
    mu = jnp.mean(y, axis=-1, keepdims=True)
    yc = y - mu
    var = jnp.mean(yc * yc, axis=-1, keepdims=True)
    return yc * lax.rsqrt(var + LN_EPS) * g + b


def _rms_norm(y, g):
    return y * lax.rsqrt(jnp.mean(y * y, axis=-1, keepdims=True) + RMS_EPS) * g


def _params(*sem):
    return pltpu.CompilerParams(dimension_semantics=sem, vmem_limit_bytes=VMEM_LIMIT)


def _ffn_ln_body(x_ref, wg_ref, wu_ref, wd_ref, g_ref, b_ref, o_ref, *rest, cast_weights, n_sub):
    copies = rest[:-2]
    acc_ref, xb_ref = rest[-2:]
    i = pl.program_id(0)
    f = pl.program_id(1)
    nt = pl.num_programs(0) - 1
    cur = i % 2
    rows = o_ref.shape[0]

    @pl.when((i == 0) & (f == 0))
    def _():
        acc_ref[1] = jnp.zeros(acc_ref.shape[1:], F32)

    def norm_previous_tile():
        r0 = pl.multiple_of(jnp.minimum(f, n_sub - 1) * rows, rows)
        y = 0.5 * acc_ref[1 - cur, pl.ds(r0, rows), :]
        out = _layer_norm(y, g_ref[...], b_ref[...])
        o_ref[...] = out
        return jnp.max(out, axis=(0, 1), keepdims=True)

    @pl.when(i < nt)
    def _():
        @pl.when(f == 0)
        def _():
            x = x_ref[...]
            xb_ref[...] = x.astype(BF16)
            acc_ref[cur] = (2.0 * ALPHA) * x

        wg, wu, wd = wg_ref[...], wu_ref[...], wd_ref[...]
        if cast_weights:
            wg, wu, wd = wg.astype(BF16), wu.astype(BF16), wd.astype(BF16)
            for dst, w in zip(copies, (wg, wu, wd)):
                dst[...] = w

        norm_max = norm_previous_tile()
        xb = xb_ref[...]
        half = wg.shape[1] // 2
        for c in range(2):
            cs = slice(c * half, (c + 1) * half)
            gate = _dot(xb, wg[:, cs])
            up = _dot(xb, wu[:, cs])
            act = gate * jax.nn.sigmoid(gate) * up
            act = jnp.where(norm_max > LARGEST, 0.0, act).astype(BF16)
            acc_ref[cur] += _dot(act, wd[cs, :])

    @pl.when(i == nt)
    def _():
        norm_previous_tile()


def _ffn_ln(x, wg, wu, wd, g, b, *, tm, tf, cast_weights=False):
    n, d = x.shape
    dff = wg.shape[1]
    nf = dff // tf
    nt = n // tm
    assert n % tm == 0 and dff % tf == 0
    n_sub = 1
    while 2 * n_sub <= nf and tm % (2 * n_sub * SUBLANES) == 0:
        n_sub *= 2
    row_i = lambda i: jnp.minimum(i, nt - 1)
    col_f = lambda i, f: jnp.where(i < nt, f, nf - 1)
    in_specs = [
        pl.BlockSpec((tm, d), lambda i, f: (row_i(i), 0)),
        pl.BlockSpec((d, tf), lambda i, f: (0, col_f(i, f))),
        pl.BlockSpec((d, tf), lambda i, f: (0, col_f(i, f))),
        pl.BlockSpec((tf, d), lambda i, f: (col_f(i, f), 0)),
        pl.BlockSpec((1, d), lambda i, f: (0, 0)),
        pl.BlockSpec((1, d), lambda i, f: (0, 0)),
    ]
    out_specs = [pl.BlockSpec(
        (tm // n_sub, d),
        lambda i, f: (jnp.where(i == 0, 0, (i - 1) * n_sub + jnp.minimum(f, n_sub - 1)), 0))]
    out_shape = [jax.ShapeDtypeStruct((n, d), F32)]
    if cast_weights:
        out_specs += in_specs[1:4]
        out_shape += [jax.ShapeDtypeStruct(w.shape, BF16) for w in (wg, wu, wd)]
    return pl.pallas_call(
        functools.partial(_ffn_ln_body, cast_weights=cast_weights, n_sub=n_sub),
        grid=(nt + 1, nf),
        in_specs=in_specs,
        out_specs=out_specs,
        out_shape=out_shape,
        scratch_shapes=[pltpu.VMEM((2, tm, d), F32), pltpu.VMEM((tm, d), BF16)],
        compiler_params=_params("arbitrary", "arbitrary"),
        name="ffn_ln",
    )(x, wg, wu, wd, g, b)


def _proj_body(h_ref, win_ref, qg_ref, kvg_ref, wqnt_ref, wqrt_ref, wkn_ref, wvt_ref,
               rot_ref, rott_ref, *rest, ql, kvl, dm, scale, n_side):
    side_in = rest[:n_side]
    qt_ref, k_ref, vt_ref, xm_ref, z_ref = rest[n_side:n_side + 5]
    for src, dst in zip(side_in, rest[n_side + 5:]):
        dst[...] = src[...].astype(BF16)
    hb = h_ref[...].astype(BF16)
    u = _dot(hb, win_ref[...])
    o1 = ql
    o2 = o1 + kvl
    o3 = o2 + LANES
    o4 = o3 + dm
    xm_ref[...] = u[:, o3:o4]
    z_ref[...] = u[:, o4:]
    rott = rott_ref[...]
    row = lax.broadcasted_iota(jnp.int32, rott.shape, 0)
    q_rope_scale = jnp.where(row < QK_ROPE, scale, 0.0)
    qn = _rms_norm(u[:, :o1], qg_ref[...]).astype(BF16)
    qat = _dot_nt(wqnt_ref[...], qn)
    qrt = _dot_nt(wqrt_ref[...], qn)
    kvn = _rms_norm(u[:, o1:o2], kvg_ref[...]).astype(BF16)
    kn = _dot(kvn, wkn_ref[...])
    vt = _dot_nt(wvt_ref[...], kvn).astype(BF16)
    dv = vt.shape[0] // ATT_HEADS
    ones_rows = (lax.broadcasted_iota(jnp.int32, (V_PAD, vt.shape[1]), 0) == 0).astype(BF16)
    for h in range(ATT_HEADS):
        lo = h * (dv + V_PAD)
        vt_ref[lo:lo + dv, :] = vt[h * dv:(h + 1) * dv]
        vt_ref[lo + dv:lo + dv + V_PAD, :] = ones_rows
    t = u[:, o2:o3] * rot_ref[...]
    k_rope = (t + pltpu.roll(t, LANES // 2, 1)).astype(BF16)
    for h in range(ATT_HEADS):
        sl = slice(h * LANES, (h + 1) * LANES)
        lo = 2 * h * LANES
        qt_ref[lo:lo + LANES, :] = (qat[sl] * scale).astype(BF16)
        t = qrt[sl] * rott
        qt_ref[lo + LANES:lo + 2 * LANES, :] = ((t + pltpu.roll(t, LANES // 2, 0)) * q_rope_scale).astype(BF16)
        k_ref[:, lo:lo + LANES] = kn[:, sl].astype(BF16)
        k_ref[:, lo + LANES:lo + 2 * LANES] = k_rope


def _win_body(a_ref, b_ref, o_ref, *, swap_tile):
    b = b_ref[...]
    half = b.shape[0] // 2
    b = jnp.where(pl.program_id(0) == swap_tile, jnp.concatenate([b[half:], b[:half]], axis=0), b)
    o_ref[...] = jnp.concatenate([a_ref[...], b], axis=0).T.astype(BF16)


def _win(w_in_t, rope_row):
    cols, d = w_in_t.shape
    blk = QK_ROPE
    kr = rope_row // blk
    assert rope_row % blk == 0 and cols % blk == 0 and kr % 2 == 0 and 2 * blk == LANES
    n_tiles = (cols + blk) // LANES
    src = lambda j: jnp.where(j <= kr, j, j - 1)
    return pl.pallas_call(
        functools.partial(_win_body, swap_tile=kr // 2),
        grid=(n_tiles,),
        in_specs=[pl.BlockSpec((blk, d), lambda t: (src(2 * t), 0)),
                  pl.BlockSpec((blk, d), lambda t: (src(2 * t + 1), 0))],
        out_specs=pl.BlockSpec((d, LANES), lambda t: (0, t)),
        out_shape=jax.ShapeDtypeStruct((d, cols + blk), BF16),
        compiler_params=_params("parallel"),
        name="win_prep",
    )(w_in_t, w_in_t)


def _slice_spec(shape, steps):
    rows, cols = shape
    for csplit in (1, 2, 4, 8):
        rblocks = steps // csplit
        if (steps % csplit == 0 and rows % (rblocks * BF16_ROWS) == 0 and cols % (csplit * LANES) == 0):
            return pl.BlockSpec((rows // rblocks, cols // csplit), lambda i: (i // csplit, i % csplit))
    raise ValueError(f"cannot slice {shape} over {steps} steps")


def _proj(h, win, qg, kvg, wqnt, wqrt, wkn, wvt, rot, *, tm, dm, side=()):
    n, d = h.shape
    ql = qg.shape[1]
    kvl = kvg.shape[1]
    da = wvt.shape[0] + ATT_HEADS * V_PAD
    npos = rot.shape[0] // tm
    hq = ATT_HEADS * 2 * LANES
    const = lambda i: (0, 0)
    scale = float((QK_NOPE + QK_ROPE) ** -0.5 * LOG2E)
    side_specs = [_slice_spec(w.shape, n // tm) for w in side]
    return pl.pallas_call(
        functools.partial(_proj_body, ql=ql, kvl=kvl, dm=dm, scale=scale, n_side=len(side)),
        grid=(n // tm,),
        in_specs=[
            pl.BlockSpec((tm, d), lambda i: (i, 0)),
            pl.BlockSpec(win.shape, const),
            pl.BlockSpec(qg.shape, const),
            pl.BlockSpec(kvg.shape, const),
            pl.BlockSpec(wqnt.shape, const),
            pl.BlockSpec(wqrt.shape, const),
            pl.BlockSpec(wkn.shape, const),
            pl.BlockSpec(wvt.shape, const),
            pl.BlockSpec((tm, LANES), lambda i: (i % npos, 0)),
            pl.BlockSpec((LANES, tm), lambda i: (0, i % npos)),
        ] + side_specs,
        out_specs=[
            pl.BlockSpec((hq, tm), lambda i: (0, i)),
            pl.BlockSpec((tm, hq), lambda i: (i, 0)),
            pl.BlockSpec((da, tm), lambda i: (0, i)),
            pl.BlockSpec((tm, dm), lambda i: (i, 0)),
            pl.BlockSpec((tm, dm), lambda i: (i, 0)),
        ] + side_specs,
        out_shape=[
            jax.ShapeDtypeStruct((hq, n), BF16),
            jax.ShapeDtypeStruct((n, hq), BF16),
            jax.ShapeDtypeStruct((da, n), BF16),
            jax.ShapeDtypeStruct((n, dm), F32),
            jax.ShapeDtypeStruct((n, dm), F32),
        ] + [jax.ShapeDtypeStruct(w.shape, BF16) for w in side],
        compiler_params=_params("parallel"),
        name="proj",
    )(h, win, qg, kvg, wqnt, wqrt, wkn, wvt, rot, rot.T, *side)


def _log_sigmoid(x):
    return jnp.minimum(x, 0.0) - jnp.log1p(jnp.exp(-jnp.abs(x)))


def _gate_scans(gi, lf, chunk):
    row = lax.broadcasted_iota(jnp.int32, (chunk, LANES), 0)
    lane = lax.broadcasted_iota(jnp.int32, (chunk, LANES), 1)
    j = lane % GATE_GROUP
    fwd = j < 3

    def scan(val, combine, identity):
        pre = val
        suf = val
        k = 1
        while k < chunk:
            pre = combine(pre, jnp.where(row >= k, pltpu.roll(pre, k, 0), identity))
            suf = combine(suf, jnp.where(row < chunk - k, pltpu.roll(suf, chunk - k, 0), identity))
            k *= 2
        return jnp.where(fwd, pre, suf)

    b = scan(lf, jnp.add, 0.0)
    r = gi - b
    cm = scan(r, jnp.maximum, LOWEST)
    return jnp.where(j % 3 == 0, b, jnp.where(j % 3 == 1, cm, r))


def _prep_body(prev_ref, x_ref, next_ref, mtail_ref, cw_ref, cb_ref, wk_ref, wv_ref, wqt_ref,
               wvt_ref, wgc_ref, wgm_ref, bg_ref, *rest, tm, chunk, dh, meta):
    if meta:
        k_ref, v_ref, gc_ref, xs_ref = rest
    else:
        k_ref, qt_ref, vt_ref, xc_ref, gc_ref, gr_ref, xs_ref = rest
    i = pl.program_id(1)
    nt = pl.num_programs(1)
    x = x_ref[...]
    if meta:
        prev = jnp.zeros_like(prev_ref[...])
        nxt = next_ref[...]
    else:
        prev = jnp.where(i == 0, mtail_ref[...], prev_ref[...])
        nxt = jnp.where(i == nt - 1, 0.0, next_ref[...])
    xs_ref[0:SUBLANES, :] = prev
    xs_ref[SUBLANES:SUBLANES + tm, :] = x
    xs_ref[SUBLANES + tm:2 * SUBLANES + tm, :] = nxt
    acc = jnp.broadcast_to(cb_ref[...], x.shape)
    for t in range(CONV_K):
        off = SUBLANES - CONV_K // 2 + t
        acc = acc + cw_ref[t:t + 1, :] * xs_ref[off:off + tm, :]
    xc = acc * jax.nn.sigmoid(acc)
    xcb = xc.astype(BF16)
    xmb = x.astype(BF16)
    k_scale = dh ** -0.5
    if not meta:
        xc_ref[...] = xcb
        ones_rows = (lax.broadcasted_iota(jnp.int32, (V_PAD, tm), 0) == 0).astype(BF16)
    for h in range(MLSTM_HEADS):
        sl = slice(h * dh, (h + 1) * dh)
        k_ref[:, sl] = (_dot(xcb[:, sl], wk_ref[h]) * k_scale).astype(BF16)
        if meta:
            v_ref[:, sl] = _dot(xmb[:, sl], wv_ref[h]).astype(BF16)
        else:
            qt_ref[sl, :] = _dot_nt(wqt_ref[h], xcb[:, sl]).astype(BF16)
            lo = h * (dh + V_PAD)
            vt_ref[lo:lo + dh, :] = _dot_nt(wvt_ref[h], xmb[:, sl]).astype(BF16)
            vt_ref[lo + dh:lo + dh + V_PAD, :] = ones_rows
    g = _dot(xcb, wgc_ref[...]) + _dot(xmb, wgm_ref[...]) + bg_ref[...]
    gi = g[:, :LANES]
    lf = _log_sigmoid(g[:, LANES:])
    for ci in range(tm // chunk):
        rows = slice(ci * chunk, (ci + 1) * chunk)
        out = _gate_scans(gi[rows], lf[rows], chunk)
        gc_ref[rows, :] = out
        if not meta:
            gr_ref[ci] = out.T


def _prep(xm, xm_meta, cw, cb, wk, wv, wqt, wvt, wgc, wgm, bg, *, batch, tm, chunk, meta):
    dm = xm.shape[1]
    dh = dm // MLSTM_HEADS
    seq = xm.shape[0] // batch
    if meta:
        nt = 1
        n_out = batch * N_META
        x_arr = xm_meta
        x_spec = pl.BlockSpec((tm, dm), lambda b, i: (0, 0))
        prev_spec = pl.BlockSpec((SUBLANES, dm), lambda b, i: (0, 0))
        next_spec = pl.BlockSpec((SUBLANES, dm), lambda b, i: (b * (seq // SUBLANES), 0))
    else:
        nt = seq // tm
        n_out = batch * seq
        x_arr = xm
        last_blk = batch * seq // SUBLANES - 1
        x_spec = pl.BlockSpec((tm, dm), lambda b, i: (b * nt + i, 0))
        prev_spec = pl.BlockSpec(
            (SUBLANES, dm), lambda b, i: (jnp.maximum((b * nt + i) * (tm // SUBLANES) - 1, 0), 0))
        next_spec = pl.BlockSpec(
            (SUBLANES, dm), lambda b, i: (jnp.minimum((b * nt + i + 1) * (tm // SUBLANES), last_blk), 0))
    const2 = lambda b, i: (0, 0)
    const3 = lambda b, i: (0, 0, 0)
    row_spec = lambda w: pl.BlockSpec((tm, w), lambda b, i: (b * nt + i, 0))
    col_spec = lambda r: pl.BlockSpec((r, tm), lambda b, i: (0, b * nt + i))
    bf = lambda *shape: jax.ShapeDtypeStruct(shape, BF16)
    gc_shape = jax.ShapeDtypeStruct((n_out, LANES), F32)
    if meta:
        out_specs = [row_spec(dm), row_spec(dm), row_spec(LANES)]
        out_shape = [bf(n_out, dm), bf(n_out, dm), gc_shape]
    else:
        dva = dm + MLSTM_HEADS * V_PAD
        out_specs = [row_spec(dm), col_spec(dm), col_spec(dva), row_spec(dm), row_spec(LANES),
                     pl.BlockSpec((tm // chunk, LANES, chunk), lambda b, i: (b * nt + i, 0, 0))]
        out_shape = [bf(n_out, dm), bf(dm, n_out), bf(dva, n_out), bf(n_out, dm), gc_shape,
                     jax.ShapeDtypeStruct((n_out // chunk, LANES, chunk), F32)]
    return pl.pallas_call(
        functools.partial(_prep_body, tm=tm, chunk=chunk, dh=dh, meta=meta),
        grid=(batch, nt),
        in_specs=[
            prev_spec, x_spec, next_spec,
            pl.BlockSpec((SUBLANES, dm), lambda b, i: (1, 0)),
            pl.BlockSpec(cw.shape, const2),
            pl.BlockSpec(cb.shape, const2),
            pl.BlockSpec(wk.shape, const3),
            pl.BlockSpec(wv.shape, const3),
            pl.BlockSpec(wqt.shape, const3),
            pl.BlockSpec(wvt.shape, const3),
            pl.BlockSpec(wgc.shape, const2),
            pl.BlockSpec(wgm.shape, const2),
            pl.BlockSpec(bg.shape, const2),
        ],
        out_specs=out_specs,
        out_shape=out_shape,
        scratch_shapes=[pltpu.VMEM((tm + 2 * SUBLANES, dm), F32)],
        compiler_params=_params("parallel", "parallel"),
        name="mlstm_prep_meta" if meta else "mlstm_prep",
    )(xm, x_arr, xm, xm_meta, cw, cb, wk, wv, wqt, wvt, wgc, wgm, bg)


def _scan_chain(k, qt, vta, r_col, rows, c_ref, m_ref, mask, edge, dh):
    b_row, cm_row = rows
    m_old = m_ref[...]
    ct = c_ref[...]

    mt = jnp.maximum(m_old, cm_row)
    at = jnp.exp(jnp.where(mask, r_col - mt, NEG)) * _dot(k, qt)
    inter = _dot(ct.astype(BF16), qt)
    w = jnp.exp(m_old - mt)
    den = jnp.sum(at, axis=0, keepdims=True) + w * inter[dh:dh + 1, :]
    scale = 1.0 / jnp.maximum(jnp.abs(den), jnp.exp(-(b_row + mt)))
    ht = _dot(vta[:dh], (at * scale).astype(BF16)) + inter[:dh] * (w * scale)

    b_end = b_row[:, edge:edge + 1]
    m_new = jnp.maximum(b_end + m_old, b_end + cm_row[:, edge:edge + 1])
    decay = jnp.exp(b_end + m_old - m_new)
    kw = (k.astype(F32) * jnp.exp(b_end + r_col - m_new)).astype(BF16)
    c_ref[...] = decay * ct + _dot(vta, kw)
    m_ref[...] = m_new
    return ht


def _scan_body(kf_ref, qtf_ref, vtf_ref, gcf_ref, grf_ref,
               kb_ref, qtb_ref, vtb_ref, gcb_ref, grb_ref,
               km_ref, vm_ref, gcm_ref, hf_ref, hb_ref, c_ref, m_ref, *, chunk, dh):
    ci = pl.program_id(1)
    nh = MLSTM_HEADS
    dva = dh + V_PAD

    @pl.when(ci == 0)
    def _():
        gcm = gcm_ref[0]
        km = km_ref[0]
        vm = vm_ref[0]
        first = lax.broadcasted_iota(jnp.int32, (V_PAD, dh), 0) == 0
        for h in range(nh):
            sl = slice(h * dh, (h + 1) * dh)
            lane0 = h * GATE_GROUP
            b_end = gcm[N_META - 1:N_META, lane0:lane0 + 1]
            m_new = jnp.maximum(b_end, b_end + gcm[N_META - 1:N_META, lane0 + 1:lane0 + 2])
            kw = km[:, sl].astype(F32) * jnp.exp(b_end + gcm[:, lane0 + 2:lane0 + 3] - m_new)
            c_ref[h, 0:dh, :] = _dot_tn(vm[:, sl], kw.astype(BF16))
            c_ref[h, dh:dva, :] = jnp.where(first, jnp.sum(kw, axis=0, keepdims=True), 0.0)
            m_ref[h] = m_new
            c_ref[nh + h] = jnp.zeros((dva, dh), F32)
            m_ref[nh + h] = jnp.zeros((1, 1), F32)

    srow = lax.broadcasted_iota(jnp.int32, (chunk, chunk), 0)
    tcol = lax.broadcasted_iota(jnp.int32, (chunk, chunk), 1)
    for h in range(nh):
        sl = slice(h * dh, (h + 1) * dh)
        sla = slice(h * dva, (h + 1) * dva)
        lane0 = h * GATE_GROUP
        rows_f = tuple(grf_ref[0, lane0 + j:lane0 + j + 1, :] for j in (0, 1))
        rows_b = tuple(grb_ref[0, lane0 + j:lane0 + j + 1, :] for j in (3, 4))
        hf_ref[sl, :] = _scan_chain(
            kf_ref[:, sl], qtf_ref[sl, :], vtf_ref[sla, :], gcf_ref[:, lane0 + 2:lane0 + 3],
            rows_f, c_ref.at[h], m_ref.at[h], srow <= tcol, chunk - 1, dh).astype(hf_ref.dtype)
        hb_ref[sl, :] = _scan_chain(
            kb_ref[:, sl], qtb_ref[sl, :], vtb_ref[sla, :], gcb_ref[:, lane0 + 5:lane0 + 6],
            rows_b, c_ref.at[nh + h], m_ref.at[nh + h], srow >= tcol, 0, dh).astype(hb_ref.dtype)


def _scan(k, qt, vta, gc, gr, km, vm, gcm, *, batch, chunk):
    n, dm = k.shape
    dh = dm // MLSTM_HEADS
    dva = vta.shape[0]
    nc = n // batch // chunk
    fwd = lambda b, c: (b * nc + c, 0)
    bwd = lambda b, c: (b * nc + nc - 1 - c, 0)
    fwd_t = lambda b, c: (0, b * nc + c)
    bwd_t = lambda b, c: (0, b * nc + nc - 1 - c)
    fwd3 = lambda b, c: (b * nc + c, 0, 0)
    bwd3 = lambda b, c: (b * nc + nc - 1 - c, 0, 0)
    per_b = lambda b, c: (b, 0, 0)

    def specs(idx, idx_t, idx3):
        return [pl.BlockSpec((chunk, dm), idx), pl.BlockSpec((dm, chunk), idx_t),
                pl.BlockSpec((dva, chunk), idx_t), pl.BlockSpec((chunk, LANES), idx),
                pl.BlockSpec((1, LANES, chunk), idx3)]

    return pl.pallas_call(
        functools.partial(_scan_body, chunk=chunk, dh=dh),
        grid=(batch, nc),
        in_specs=specs(fwd, fwd_t, fwd3) + specs(bwd, bwd_t, bwd3) + [
            pl.BlockSpec((1, N_META, dm), per_b),
            pl.BlockSpec((1, N_META, dm), per_b),
            pl.BlockSpec((1, N_META, LANES), per_b),
        ],
        out_specs=[pl.BlockSpec((dm, chunk), fwd_t), pl.BlockSpec((dm, chunk), bwd_t)],
        out_shape=[jax.ShapeDtypeStruct((dm, n), BF16)] * 2,
        scratch_shapes=[
            pltpu.VMEM((2 * MLSTM_HEADS, dh + V_PAD, dh), F32),
            pltpu.VMEM((2 * MLSTM_HEADS, 1, 1), F32),
        ],
        compiler_params=_params("parallel", "arbitrary"),
        name="mlstm_scan",
    )(k, qt, vta, gc, gr, k, qt, vta, gc, gr,
      km.reshape(batch, N_META, dm), vm.reshape(batch, N_META, dm), gcm.reshape(batch, N_META, LANES))


def _attn_body(qt_ref, k_ref, vt_ref, km_ref, vmt_ref, o_ref, *, tk, nk, hps):
    dk = 2 * LANES
    dva = vt_ref.shape[0] // hps
    dv = dva - V_PAD
    qts = [qt_ref[h * dk:(h + 1) * dk, :] for h in range(hps)]

    def scores(h, j):
        s = _dot(k_ref[j * tk:(j + 1) * tk, h * dk:(h + 1) * dk], qts[h])
        return s, jnp.max(s, axis=0, keepdims=True)

    ahead, ms, accs = [], [], []
    for h in range(hps):
        s_meta = _dot(km_ref[:, h * dk:(h + 1) * dk], qts[h])
        ahead.append(scores(h, 0))
        m = jnp.maximum(jnp.max(s_meta, axis=0, keepdims=True), ahead[h][1])
        ms.append(m)
        accs.append(_dot(vmt_ref[h * dva:(h + 1) * dva, :], jnp.exp2(s_meta - m).astype(BF16)))
    for j in range(nk):
        for h in range(hps):
            s, s_max = ahead[h]
            if j + 1 < nk:
                ahead[h] = scores(h, j + 1)
            m_new = jnp.maximum(ms[h], s_max)
            p = jnp.exp2(s - m_new).astype(BF16)
            accs[h] = jnp.exp2(ms[h] - m_new) * accs[h] + _dot(vt_ref[h * dva:(h + 1) * dva, j * tk:(j + 1) * tk], p)
            ms[h] = m_new
    for h in range(hps):
        o_ref[:, h * dv:(h + 1) * dv] = (accs[h][:dv] * (1.0 / accs[h][dv:dv + 1])).T


def _attn(qt, k, vt, km, vmt, *, batch, tq, tk):
    n = k.shape[0]
    seq = n // batch
    nq = seq // tq
    dva = vt.shape[0] // ATT_HEADS
    dv = dva - V_PAD
    dk = 2 * LANES
    hps = 2
    assert ATT_HEADS % hps == 0
    return pl.pallas_call(
        functools.partial(_attn_body, tk=tk, nk=seq // tk, hps=hps),
        grid=(batch, ATT_HEADS // hps, nq),
        in_specs=[
            pl.BlockSpec((hps * dk, tq), lambda b, h, i: (h, b * nq + i)),
            pl.BlockSpec((seq, hps * dk), lambda b, h, i: (b, h)),
            pl.BlockSpec((hps * dva, seq), lambda b, h, i: (h, b)),
            pl.BlockSpec((N_META, hps * dk), lambda b, h, i: (0, h)),
            pl.BlockSpec((hps * dva, N_META), lambda b, h, i: (h, 0)),
        ],
        out_specs=pl.BlockSpec((tq, hps * dv), lambda b, h, i: (b * nq + i, h)),
        out_shape=jax.ShapeDtypeStruct((n, ATT_HEADS * dv), F32),
        compiler_params=_params("parallel", "parallel", "arbitrary"),
        name="attn",
    )(qt, k, vt, km, vmt)


def _outproj_body(o_ref, hf_ref, hb_ref, z_ref, xc_ref, h1_ref, og_ref, gn_ref, sk_ref,
                  wa_ref, wm_ref, g_ref, b_ref, out_ref, *, dh, rows):
    for r in range(out_ref.shape[0] // rows):
        rs = slice(r * rows, (r + 1) * rows)
        y_att = _rms_norm(o_ref[rs, :], og_ref[...]).astype(BF16)
        h_sum = (hf_ref[:, rs].astype(F32) + hb_ref[:, rs].astype(F32)).T
        hs = h_sum * jax.nn.sigmoid(z_ref[rs, :])
        parts = []
        for h in range(MLSTM_HEADS):
            seg = hs[:, h * dh:(h + 1) * dh]
            mu = jnp.mean(seg, axis=-1, keepdims=True)
            sc = seg - mu
            var = jnp.mean(sc * sc, axis=-1, keepdims=True)
            parts.append(sc * lax.rsqrt(var + LN_EPS))
        hn = jnp.concatenate(parts, axis=-1)
        y_ml = (hn * gn_ref[...] + sk_ref[...] * xc_ref[rs, :].astype(F32)).astype(BF16)
        y = _dot(y_att, wa_ref[...]) + _dot(y_ml, wm_ref[...])
        out_ref[rs, :] = _layer_norm(ALPHA * h1_ref[rs, :] + y, g_ref[...], b_ref[...])


def _outproj(o, hf, hb, z, xc, h1, og, gn, sk, wo, g, b, *, tm, rows):
    n, d = h1.shape
    da = o.shape[1]
    dm = hf.shape[0]
    assert da == dm and wo.shape[0] == da + dm
    const = lambda i: (0, 0)
    row = lambda w: pl.BlockSpec((tm, w), lambda i: (i, 0))
    col = pl.BlockSpec((dm, tm), lambda i: (0, i))
    return pl.pallas_call(
        functools.partial(_outproj_body, dh=dm // MLSTM_HEADS, rows=rows),
        grid=(n // tm,),
        in_specs=[row(da), col, col, row(dm), row(dm), row(d),
                  pl.BlockSpec(og.shape, const), pl.BlockSpec(gn.shape, const), pl.BlockSpec(sk.shape, const),
                  pl.BlockSpec((da, d), lambda i: (0, 0), pipeline_mode=pl.Buffered(1)),
                  pl.BlockSpec((dm, d), lambda i: (1, 0), pipeline_mode=pl.Buffered(1)),
                  pl.BlockSpec(g.shape, const), pl.BlockSpec(b.shape, const)],
        out_specs=row(d),
        out_shape=jax.ShapeDtypeStruct((n, d), F32),
        compiler_params=_params("parallel"),
        name="outproj",
    )(o, hf, hb, z, xc, h1, og, gn, sk, wo, wo, g, b)


def _pad_lanes(w):
    return jnp.pad(w, [(0, 0)] * (w.ndim - 1) + [(0, LANES - w.shape[-1])])


def _pick(n, pref):
    for t in pref:
        if n % t == 0:
            return t
    raise ValueError(f"no tile for {n}")


def kernel(x, meta_tokens, ffn1_w_gate, ffn1_w_up, ffn1_w_down, ln1_g, ln1_b, w_in, mla_q_norm_g, mla_w_uq, mla_kv_norm_g, mla_w_ukv, attn_out_g, mlstm_conv_w, mlstm_conv_b, mlstm_w_q, mlstm_w_k, mlstm_w_v, mlstm_w_gates, mlstm_b_gates, mlstm_gn_g, mlstm_skip, w_out, ln2_g, ln2_b, ffn2_w_gate, ffn2_w_up, ffn2_w_down, ln3_g, ln3_b):
    batch, seq, d = x.shape
    assert ffn1_w_gate.shape[0] == DEPTH and meta_tokens.shape[0] == N_META
    n = batch * seq
    ql = mla_q_norm_g.shape[-1]
    kvl = mla_kv_norm_g.shape[-1]
    da = attn_out_g.shape[-1]
    dm = mlstm_gn_g.shape[-1]
    dh = dm // MLSTM_HEADS
    dv = da // ATT_HEADS
    nh = MLSTM_HEADS
    half = QK_ROPE // 2
    assert dv == LANES and dh % LANES == 0 and ql % LANES == 0 and kvl % LANES == 0
    assert seq % MLSTM_CHUNK == 0

    swap = jnp.concatenate([jnp.arange(half, QK_ROPE), jnp.arange(half)])
    win = _win(jnp.swapaxes(w_in[0], 0, 1), ql + kvl)
    wuq = mla_w_uq[0].reshape(ql, ATT_HEADS, QK_NOPE + QK_ROPE)
    wqnt = wuq[:, :, :QK_NOPE].reshape(ql, -1).T.astype(BF16)
    w_qr = wuq[:, :, QK_NOPE:]
    wqrt = jnp.concatenate([w_qr, w_qr[:, :, swap]], axis=-1).reshape(ql, -1).T.astype(BF16)
    wukv = mla_w_ukv[0].reshape(kvl, ATT_HEADS, QK_NOPE + dv)
    wkn = wukv[:, :, :QK_NOPE].reshape(kvl, -1).astype(BF16)
    wvt = wukv[:, :, QK_NOPE:].reshape(kvl, -1).T.astype(BF16)

    wg = mlstm_w_gates[0].reshape(2 * dm, 4, nh)
    bgr = mlstm_b_gates[0].reshape(4, nh)

    def gate_tile(w4, kind_f, kind_b):
        zeros = jnp.zeros_like(w4[..., 0, :])
        cols = [w4[..., kind_f, :]] * 3 + [w4[..., kind_b, :]] * 3 + [zeros] * (GATE_GROUP - 6)
        tile = jnp.stack(cols, axis=-1).reshape(*w4.shape[:-2], nh * GATE_GROUP)
        return _pad_lanes(tile)

    wgate = jnp.concatenate([gate_tile(wg, 0, 2), gate_tile(wg, 1, 3)], axis=-1)
    wgc = wgate[:dm].astype(BF16)
    wgm = wgate[dm:].astype(BF16)
    bg = jnp.concatenate([gate_tile(bgr, 0, 2), gate_tile(bgr, 1, 3)], axis=-1)[None, :]
    cw = jnp.pad(mlstm_conv_w[0], ((0, SUBLANES - CONV_K), (0, 0)))
    cb = mlstm_conv_b[0][None, :]
    wk_m, wv_m = (w[0].astype(BF16) for w in (mlstm_w_k, mlstm_w_v))
    wqt_m, wvt_m = (jnp.swapaxes(w[0], 1, 2).astype(BF16) for w in (mlstm_w_q, mlstm_w_v))
    row = lambda p: p[0][None, :]

    pos = jnp.arange(N_META + seq, dtype=F32)
    inv = ROPE_BASE ** (-jnp.arange(0, QK_ROPE, 2, dtype=F32) / QK_ROPE)
    ang = pos[:, None] * inv[None, :]
    rot = jnp.concatenate([jnp.cos(ang), jnp.cos(ang), -jnp.sin(ang), jnp.sin(ang)], axis=-1)

    ffn1_f32 = [w[0] for w in (ffn1_w_gate, ffn1_w_up, ffn1_w_down)]
    ffn2_f32 = [w[0] for w in (ffn2_w_gate, ffn2_w_up, ffn2_w_down)]
    tf = _pick(ffn1_f32[0].shape[1], (512, 256, 128))
    tm_ffn = _pick(n, (1024, 512, 256))
    tm_proj = _pick(seq, (256,))

    xr = x.reshape(n, d)
    h1m, *ffn1 = _ffn_ln(meta_tokens.astype(x.dtype), *ffn1_f32, row(ln1_g), row(ln1_b),
                         tm=N_META, tf=tf, cast_weights=True)
    h1, = _ffn_ln(xr, *ffn1, row(ln1_g), row(ln1_b), tm=tm_ffn, tf=tf)
    proj_w = (win, row(mla_q_norm_g), row(mla_kv_norm_g), wqnt, wqrt, wkn, wvt)
    q_a, k_a, vt_a, xm, z, *ffn2, wo = _proj(
        h1, *proj_w, rot[N_META:], tm=tm_proj, dm=dm, side=ffn2_f32 + [w_out[0]])
    _, k_am, vt_am, xm_m, _ = _proj(h1m, *proj_w, rot[:N_META], tm=N_META, dm=dm)

    prep_w = (cw, cb, wk_m, wv_m, wqt_m, wvt_m, wgc, wgm, bg)
    tm_prep = _pick(seq, (512, MLSTM_CHUNK))
    k_m, qt_m, vt_m, xc, gc, gr = _prep(
        xm, xm_m, *prep_w, batch=batch, tm=tm_prep, chunk=MLSTM_CHUNK, meta=False)
    k_mm, v_mm, gc_m = _prep(xm, xm_m, *prep_w, batch=batch, tm=N_META, chunk=N_META, meta=True)
    hf, hb = _scan(k_m, qt_m, vt_m, gc, gr, k_mm, v_mm, gc_m, batch=batch, chunk=MLSTM_CHUNK)

    o = _attn(q_a, k_a, vt_a, k_am, vt_am, batch=batch, tq=_pick(seq, (2048, 1024, 512, 256)), tk=_pick(seq, (512, 256)))

    h2 = _outproj(o, hf, hb, z, xc, h1, row(attn_out_g), row(mlstm_gn_g), row(mlstm_skip),
                  wo, row(ln2_g), row(ln2_b), tm=_pick(seq, (512, 256)), rows=_pick(seq, (256,)))
    out, = _ffn_ln(h2, *ffn2, row(ln3_g), row(ln3_b), tm=tm_ffn, tf=tf)
    return out.reshape(batch, seq, d)
```

```python
import functools
import math

import jax
import jax.numpy as jnp
from jax import lax
from jax.experimental import pallas as pl
from jax.experimental.pallas import tpu as pltpu

F32 = jnp.float32
BF16 = jnp.bfloat16

N_META = 16
ATT_HEADS = 8
QK_NOPE = 128
QK_ROPE = 64
MLSTM_HEADS = 4
CONV_K = 5
ROPE_BASE = 10000.0
LN_EPS = 1e-5
RMS_EPS = 1e-6
DEPTH = 1
ALPHA = (2 * DEPTH) ** 0.25

LANES = 128
SUBLANES = 8
VMEM_LIMIT = 56 * 1024 * 1024

BF16_ROWS = 16
V_PAD = BF16_ROWS
MLSTM_CHUNK = 256
GATE_GROUP = 8
LOWEST = float(jnp.finfo(jnp.float32).min)
LARGEST = float(jnp.finfo(jnp.float32).max)
NEG = -1e30
LOG2E = 1.4426950408889634


def _dot(a, b):
    return jnp.dot(a, b, preferred_element_type=F32)


def _dot_nt(a, b):
    return lax.dot_general(a, b, (((1,), (1,)), ((), ())), preferred_element_type=F32)


def _dot_tn(a, b):
    return lax.dot_general(a, b, (((0,), (0,)), ((), ())), preferred_element_type=F32)


def _layer_norm(y, g, b):
    mu = jnp.mean(y, axis=-1, keepdims=True)
    yc = y - mu
    var = jnp.mean(yc * yc, axis=-1, keepdims=True)
    return yc * lax.rsqrt(var + LN_EPS) * g + b


def _rms_norm(y, g):
    return y * lax.rsqrt(jnp.mean(y * y, axis=-1, keepdims=True) + RMS_EPS) * g


def _params(*sem):
    return pltpu.CompilerParams(dimension_semantics=sem, vmem_limit_bytes=VMEM_LIMIT)


def _ffn_ln_body(x_ref, wg_ref, wu_ref, wd_ref, g_ref, b_ref, o_ref, *rest, cast_weights, n_sub):
    copies = rest[:-2]
    acc_ref, xb_ref = rest[-2:]
    i = pl.program_id(0)
    f = pl.program_id(1)
    nt = pl.num_programs(0) - 1
    cur = i % 2
    rows = o_ref.shape[0]

    @pl.when((i == 0) & (f == 0))
    def _():
        acc_ref[1] = jnp.zeros(acc_ref.shape[1:], F32)

    def norm_previous_tile():
        r0 = pl.multiple_of(jnp.minimum(f, n_sub - 1) * rows, rows)
        y = 0.5 * acc_ref[1 - cur, pl.ds(r0, rows), :]
        out = _layer_norm(y, g_ref[...], b_ref[...])
        o_ref[...] = out
        return jnp.max(out, axis=(0, 1), keepdims=True)

    @pl.when(i < nt)
    def _():
        @pl.when(f == 0)
        def _():
            x = x_ref[...]
            xb_ref[...] = x.astype(BF16)
            acc_ref[cur] = (2.0 * ALPHA) * x

        wg, wu, wd = wg_ref[...], wu_ref[...], wd_ref[...]
        if cast_weights:
            wg, wu, wd = wg.astype(BF16), wu.astype(BF16), wd.astype(BF16)
            for dst, w in zip(copies, (wg, wu, wd)):
                dst[...] = w

        norm_max = norm_previous_tile()
        xb = xb_ref[...]
        half = wg.shape[1] // 2
        for c in range(2):
            cs = slice(c * half, (c + 1) * half)
            gate = _dot(xb, wg[:, cs])
            up = _dot(xb, wu[:, cs])
            act = gate * jax.nn.sigmoid(gate) * up
            act = jnp.where(norm_max > LARGEST, 0.0, act).astype(BF16)
            acc_ref[cur] += _dot(act, wd[cs, :])

    @pl.when(i == nt)
    def _():
        norm_previous_tile()


def _ffn_ln(x, wg, wu, wd, g, b, *, tm, tf, cast_weights=False):
    n, d = x.shape
    dff = wg.shape[1]
    nf = dff // tf
    nt = n // tm
    assert n % tm == 0 and dff % tf == 0
    n_sub = 1
    while 2 * n_sub <= nf and tm % (2 * n_sub * SUBLANES) == 0:
        n_sub *= 2
    row_i = lambda i: jnp.minimum(i, nt - 1)
    col_f = lambda i, f: jnp.where(i < nt, f, nf - 1)
    in_specs = [
        pl.BlockSpec((tm, d), lambda i, f: (row_i(i), 0)),
        pl.BlockSpec((d, tf), lambda i, f: (0, col_f(i, f))),
        pl.BlockSpec((d, tf), lambda i, f: (0, col_f(i, f))),
        pl.BlockSpec((tf, d), lambda i, f: (col_f(i, f), 0)),
        pl.BlockSpec((1, d), lambda i, f: (0, 0)),
        pl.BlockSpec((1, d), lambda i, f: (0, 0)),
    ]
    out_specs = [pl.BlockSpec(
        (tm // n_sub, d),
        lambda i, f: (jnp.where(i == 0, 0, (i - 1) * n_sub + jnp.minimum(f, n_sub - 1)), 0))]
    out_shape = [jax.ShapeDtypeStruct((n, d), F32)]
    if cast_weights:
        out_specs += in_specs[1:4]
        out_shape += [jax.ShapeDtypeStruct(w.shape, BF16) for w in (wg, wu, wd)]
    return pl.pallas_call(
        functools.partial(_ffn_ln_body, cast_weights=cast_weights, n_sub=n_sub),
        grid=(nt + 1, nf),
        in_specs=in_specs,
        out_specs=out_specs,
        out_shape=out_shape,
        scratch_shapes=[pltpu.VMEM((2, tm, d), F32), pltpu.VMEM((tm, d), BF16)],
        compiler_params=_params("arbitrary", "arbitrary"),
        name="ffn_ln",
    )(x, wg, wu, wd, g, b)


def _proj_body(h_ref, win_ref, qg_ref, kvg_ref, wqnt_ref, wqrt_ref, wkn_ref, wvt_ref,
               rot_ref, rott_ref, *rest, ql, kvl, dm, scale, n_side):
    side_in = rest[:n_side]
    qt_ref, k_ref, vt_ref, xm_ref, z_ref = rest[n_side:n_side + 5]
    for src, dst in zip(side_in, rest[n_side + 5:]):
        dst[...] = src[...].astype(BF16)
    hb = h_ref[...].astype(BF16)
    u = _dot(hb, win_ref[...])
    o1 = ql
    o2 = o1 + kvl
    o3 = o2 + LANES
    o4 = o3 + dm
    xm_ref[...] = u[:, o3:o4]
    z_ref[...] = u[:, o4:]
    rott = rott_ref[...]
    row = lax.broadcasted_iota(jnp.int32, rott.shape, 0)
    q_rope_scale = jnp.where(row < QK_ROPE, scale, 0.0)
    qn = _rms_norm(u[:, :o1], qg_ref[...]).astype(BF16)
    qat = _dot_nt(wqnt_ref[...], qn)
    qrt = _dot_nt(wqrt_ref[...], qn)
    kvn = _rms_norm(u[:, o1:o2], kvg_ref[...]).astype(BF16)
    kn = _dot(kvn, wkn_ref[...])
    vt = _dot_nt(wvt_ref[...], kvn).astype(BF16)
    dv = vt.shape[0] // ATT_HEADS
    ones_rows = (lax.broadcasted_iota(jnp.int32, (V_PAD, vt.shape[1]), 0) == 0).astype(BF16)
    for h in range(ATT_HEADS):
        lo = h * (dv + V_PAD)
        vt_ref[lo:lo + dv, :] = vt[h * dv:(h + 1) * dv]
        vt_ref[lo + dv:lo + dv + V_PAD, :] = ones_rows
    t = u[:, o2:o3] * rot_ref[...]
    k_rope = (t + pltpu.roll(t, LANES // 2, 1)).astype(BF16)
    for h in range(ATT_HEADS):
        sl = slice(h * LANES, (h + 1) * LANES)
        lo = 2 * h * LANES
        qt_ref[lo:lo + LANES, :] = (qat[sl] * scale).astype(BF16)
        t = qrt[sl] * rott
        qt_ref[lo + LANES:lo + 2 * LANES, :] = ((t + pltpu.roll(t, LANES // 2, 0)) * q_rope_scale).astype(BF16)
        k_ref[:, lo:lo + LANES] = kn[:, sl].astype(BF16)
        k_ref[:, lo + LANES:lo + 2 * LANES] = k_rope


def _win_body(a_ref, b_ref, o_ref, *, swap_tile):
    b = b_ref[...]
    half = b.shape[0] // 2
    b = jnp.where(pl.program_id(0) == swap_tile, jnp.concatenate([b[half:], b[:half]], axis=0), b)
    o_ref[...] = jnp.concatenate([a_ref[...], b], axis=0).T.astype(BF16)


def _win(w_in_t, rope_row):
    cols, d = w_in_t.shape
    blk = QK_ROPE
    kr = rope_row // blk
    assert rope_row % blk == 0 and cols % blk == 0 and kr % 2 == 0 and 2 * blk == LANES
    n_tiles = (cols + blk) // LANES
    src = lambda j: jnp.where(j <= kr, j, j - 1)
    return pl.pallas_call(
        functools.partial(_win_body, swap_tile=kr // 2),
        grid=(n_tiles,),
        in_specs=[pl.BlockSpec((blk, d), lambda t: (src(2 * t), 0)),
                  pl.BlockSpec((blk, d), lambda t: (src(2 * t + 1), 0))],
        out_specs=pl.BlockSpec((d, LANES), lambda t: (0, t)),
        out_shape=jax.ShapeDtypeStruct((d, cols + blk), BF16),
        compiler_params=_params("parallel"),
        name="win_prep",
    )(w_in_t, w_in_t)


def _slice_spec(shape, steps):
    rows, cols = shape
    for csplit in (1, 2, 4, 8):
        rblocks = steps // csplit
        if (steps % csplit == 0 and rows % (rblocks * BF16_ROWS) == 0 and cols % (csplit * LANES) == 0):
            return pl.BlockSpec((rows // rblocks, cols // csplit), lambda i: (i // csplit, i % csplit))
    raise ValueError(f"cannot slice {shape} over {steps} steps")


def _proj(h, win, qg, kvg, wqnt, wqrt, wkn, wvt, rot, *, tm, dm, side=()):
    n, d = h.shape
    ql = qg.shape[1]
    kvl = kvg.shape[1]
    da = wvt.shape[0] + ATT_HEADS * V_PAD
    npos = rot.shape[0] // tm
    hq = ATT_HEADS * 2 * LANES
    const = lambda i: (0, 0)
    scale = float((QK_NOPE + QK_ROPE) ** -0.5 * LOG2E)
    side_specs = [_slice_spec(w.shape, n // tm) for w in side]
    return pl.pallas_call(
        functools.partial(_proj_body, ql=ql, kvl=kvl, dm=dm, scale=scale, n_side=len(side)),
        grid=(n // tm,),
        in_specs=[
            pl.BlockSpec((tm, d), lambda i: (i, 0)),
            pl.BlockSpec(win.shape, const),
            pl.BlockSpec(qg.shape, const),
            pl.BlockSpec(kvg.shape, const),
            pl.BlockSpec(wqnt.shape, const),
            pl.BlockSpec(wqrt.shape, const),
            pl.BlockSpec(wkn.shape, const),
            pl.BlockSpec(wvt.shape, const),
            pl.BlockSpec((tm, LANES), lambda i: (i % npos, 0)),
            pl.BlockSpec((LANES, tm), lambda i: (0, i % npos)),
        ] + side_specs,
        out_specs=[
            pl.BlockSpec((hq, tm), lambda i: (0, i)),
            pl.BlockSpec((tm, hq), lambda i: (i, 0)),
            pl.BlockSpec((da, tm), lambda i: (0, i)),
            pl.BlockSpec((tm, dm), lambda i: (i, 0)),
            pl.BlockSpec((tm, dm), lambda i: (i, 0)),
        ] + side_specs,
        out_shape=[
            jax.ShapeDtypeStruct((hq, n), BF16),
            jax.ShapeDtypeStruct((n, hq), BF16),
            jax.ShapeDtypeStruct((da, n), BF16),
            jax.ShapeDtypeStruct((n, dm), F32),
            jax.ShapeDtypeStruct((n, dm), F32),
        ] + [jax.ShapeDtypeStruct(w.shape, BF16) for w in side],
        compiler_params=_params("parallel"),
        name="proj",
    )(h, win, qg, kvg, wqnt, wqrt, wkn, wvt, rot, rot.T, *side)


def _log_sigmoid(x):
    return jnp.minimum(x, 0.0) - jnp.log1p(jnp.exp(-jnp.abs(x)))


def _gate_scans(gi, lf, chunk):
    row = lax.broadcasted_iota(jnp.int32, (chunk, LANES), 0)
    lane = lax.broadcasted_iota(jnp.int32, (chunk, LANES), 1)
    j = lane % GATE_GROUP
    fwd = j < 3

    def scan(val, combine, identity):
        pre = val
        suf = val
        k = 1
        while k < chunk:
            pre = combine(pre, jnp.where(row >= k, pltpu.roll(pre, k, 0), identity))
            suf = combine(suf, jnp.where(row < chunk - k, pltpu.roll(suf, chunk - k, 0), identity))
            k *= 2
        return jnp.where(fwd, pre, suf)

    b = scan(lf, jnp.add, 0.0)
    r = gi - b
    cm = scan(r, jnp.maximum, LOWEST)
    return jnp.where(j % 3 == 0, b, jnp.where(j % 3 == 1, cm, r))


def _prep_body(prev_ref, x_ref, next_ref, mtail_ref, cw_ref, cb_ref, wk_ref, wv_ref, wqt_ref,
               wvt_ref, wgc_ref, wgm_ref, bg_ref, *rest, tm, chunk, dh, meta):
    if meta:
        k_ref, v_ref, gc_ref, xs_ref = rest
    else:
        k_ref, qt_ref, vt_ref, xc_ref, gc_ref, gr_ref, xs_ref = rest
    i = pl.program_id(1)
    nt = pl.num_programs(1)
    x = x_ref[...]
    if meta:
        prev = jnp.zeros_like(prev_ref[...])
        nxt = next_ref[...]
    else:
        prev = jnp.where(i == 0, mtail_ref[...], prev_ref[...])
        nxt = jnp.where(i == nt - 1, 0.0, next_ref[...])
    xs_ref[0:SUBLANES, :] = prev
    xs_ref[SUBLANES:SUBLANES + tm, :] = x
    xs_ref[SUBLANES + tm:2 * SUBLANES + tm, :] = nxt
    acc = jnp.broadcast_to(cb_ref[...], x.shape)
    for t in range(CONV_K):
        off = SUBLANES - CONV_K // 2 + t
        acc = acc + cw_ref[t:t + 1, :] * xs_ref[off:off + tm, :]
    xc = acc * jax.nn.sigmoid(acc)
    xcb = xc.astype(BF16)
    xmb = x.astype(BF16)
    if not meta:
        xc_ref[...] = xcb
        ones_rows = (lax.broadcasted_iota(jnp.int32, (V_PAD, tm), 0) == 0).astype(BF16)
    for h in range(MLSTM_HEADS):
        sl = slice(h * dh, (h + 1) * dh)
        k_ref[:, sl] = _dot(xcb[:, sl], wk_ref[h]).astype(BF16)
        if meta:
            v_ref[:, sl] = _dot(xmb[:, sl], wv_ref[h]).astype(BF16)
        else:
            qt_ref[sl, :] = _dot_nt(wqt_ref[h], xcb[:, sl]).astype(BF16)
            lo = h * (dh + V_PAD)
            vt_ref[lo:lo + dh, :] = _dot_nt(wvt_ref[h], xmb[:, sl]).astype(BF16)
            vt_ref[lo + dh:lo + dh + V_PAD, :] = ones_rows
    g = _dot(xcb, wgc_ref[...]) + _dot(xmb, wgm_ref[...]) + bg_ref[...]
    gi = g[:, :LANES]
    lf = _log_sigmoid(g[:, LANES:])
    for ci in range(tm // chunk):
        rows = slice(ci * chunk, (ci + 1) * chunk)
        out = _gate_scans(gi[rows], lf[rows], chunk)
        gc_ref[rows, :] = out
        if not meta:
            gr_ref[ci] = out.T


def _prep(xm, xm_meta, cw, cb, wk, wv, wqt, wvt, wgc, wgm, bg, *, batch, tm, chunk, meta):
    dm = xm.shape[1]
    dh = dm // MLSTM_HEADS
    seq = xm.shape[0] // batch
    if meta:
        nt = 1
        n_out = batch * N_META
        x_arr = xm_meta
        x_spec = pl.BlockSpec((tm, dm), lambda b, i: (0, 0))
        prev_spec = pl.BlockSpec((SUBLANES, dm), lambda b, i: (0, 0))
        next_spec = pl.BlockSpec((SUBLANES, dm), lambda b, i: (b * (seq // SUBLANES), 0))
    else:
        nt = seq // tm
        n_out = batch * seq
        x_arr = xm
        last_blk = batch * seq // SUBLANES - 1
        x_spec = pl.BlockSpec((tm, dm), lambda b, i: (b * nt + i, 0))
        prev_spec = pl.BlockSpec(
            (SUBLANES, dm), lambda b, i: (jnp.maximum((b * nt + i) * (tm // SUBLANES) - 1, 0), 0))
        next_spec = pl.BlockSpec(
            (SUBLANES, dm), lambda b, i: (jnp.minimum((b * nt + i + 1) * (tm // SUBLANES), last_blk), 0))
    const2 = lambda b, i: (0, 0)
    const3 = lambda b, i: (0, 0, 0)
    row_spec = lambda w: pl.BlockSpec((tm, w), lambda b, i: (b * nt + i, 0))
    col_spec = lambda r: pl.BlockSpec((r, tm), lambda b, i: (0, b * nt + i))
    bf = lambda *shape: jax.ShapeDtypeStruct(shape, BF16)
    gc_shape = jax.ShapeDtypeStruct((n_out, LANES), F32)
    if meta:
        out_specs = [row_spec(dm), row_spec(dm), row_spec(LANES)]
        out_shape = [bf(n_out, dm), bf(n_out, dm), gc_shape]
    else:
        dva = dm + MLSTM_HEADS * V_PAD
        out_specs = [row_spec(dm), col_spec(dm), col_spec(dva), row_spec(dm), row_spec(LANES),
                     pl.BlockSpec((tm // chunk, LANES, chunk), lambda b, i: (b * nt + i, 0, 0))]
        out_shape = [bf(n_out, dm), bf(dm, n_out), bf(dva, n_out), bf(n_out, dm), gc_shape,
                     jax.ShapeDtypeStruct((n_out // chunk, LANES, chunk), F32)]
    return pl.pallas_call(
        functools.partial(_prep_body, tm=tm, chunk=chunk, dh=dh, meta=meta),
        grid=(batch, nt),
        in_specs=[
            prev_spec, x_spec, next_spec,
            pl.BlockSpec((SUBLANES, dm), lambda b, i: (1, 0)),
            pl.BlockSpec(cw.shape, const2),
            pl.BlockSpec(cb.shape, const2),
            pl.BlockSpec(wk.shape, const3),
            pl.BlockSpec(wv.shape, const3),
            pl.BlockSpec(wqt.shape, const3),
            pl.BlockSpec(wvt.shape, const3),
            pl.BlockSpec(wgc.shape, const2),
            pl.BlockSpec(wgm.shape, const2),
            pl.BlockSpec(bg.shape, const2),
        ],
        out_specs=out_specs,
        out_shape=out_shape,
        scratch_shapes=[pltpu.VMEM((tm + 2 * SUBLANES, dm), F32)],
        compiler_params=_params("parallel", "parallel"),
        name="mlstm_prep_meta" if meta else "mlstm_prep",
    )(xm, x_arr, xm, xm_meta, cw, cb, wk, wv, wqt, wvt, wgc, wgm, bg)


def _scan_chain(k, qt, vta, r_col, rows, c_ref, m_ref, mask, edge, dh):
    b_row, cm_row = rows
    m_old = m_ref[...]
    ct = c_ref[...]

    mt = jnp.maximum(m_old, cm_row)
    at = jnp.exp(jnp.where(mask, r_col - mt, NEG)) * _dot(k, qt)
    inter = _dot(ct.astype(BF16), qt)
    w = jnp.exp(m_old - mt)
    den = jnp.sum(at, axis=0, keepdims=True) + w * inter[dh:dh + 1, :]
    scale = 1.0 / jnp.maximum(jnp.abs(den), jnp.exp(-(b_row + mt)))
    ht = _dot(vta[:dh], (at * scale).astype(BF16)) + inter[:dh] * (w * scale)

    b_end = b_row[:, edge:edge + 1]
    m_new = jnp.maximum(b_end + m_old, b_end + cm_row[:, edge:edge + 1])
    decay = jnp.exp(b_end + m_old - m_new)
    kw = (k.astype(F32) * jnp.exp(b_end + r_col - m_new)).astype(BF16)
    c_ref[...] = decay * ct + _dot(vta, kw)
    m_ref[...] = m_new
    return ht


def _scan_body(kf_ref, qtf_ref, vtf_ref, gcf_ref, grf_ref,
               kb_ref, qtb_ref, vtb_ref, gcb_ref, grb_ref,
               km_ref, vm_ref, gcm_ref, hf_ref, hb_ref, c_ref, m_ref, *, chunk, dh):
    ci = pl.program_id(1)
    nh = MLSTM_HEADS
    dva = dh + V_PAD

    @pl.when(ci == 0)
    def _():
        gcm = gcm_ref[0]
        km = km_ref[0]
        vm = vm_ref[0]
        first = lax.broadcasted_iota(jnp.int32, (V_PAD, dh), 0) == 0
        for h in range(nh):
            sl = slice(h * dh, (h + 1) * dh)
            lane0 = h * GATE_GROUP
            b_end = gcm[N_META - 1:N_META, lane0:lane0 + 1]
            m_new = jnp.maximum(b_end, b_end + gcm[N_META - 1:N_META, lane0 + 1:lane0 + 2])
            kw = km[:, sl].astype(F32) * jnp.exp(b_end + gcm[:, lane0 + 2:lane0 + 3] - m_new)
            c_ref[h, 0:dh, :] = _dot_tn(vm[:, sl], kw.astype(BF16))
            c_ref[h, dh:dva, :] = jnp.where(first, jnp.sum(kw, axis=0, keepdims=True), 0.0)
            m_ref[h] = m_new
            c_ref[nh + h] = jnp.zeros((dva, dh), F32)
            m_ref[nh + h] = jnp.zeros((1, 1), F32)

    srow = lax.broadcasted_iota(jnp.int32, (chunk, chunk), 0)
    tcol = lax.broadcasted_iota(jnp.int32, (chunk, chunk), 1)
    for h in range(nh):
        sl = slice(h * dh, (h + 1) * dh)
        sla = slice(h * dva, (h + 1) * dva)
        lane0 = h * GATE_GROUP
        rows_f = tuple(grf_ref[0, lane0 + j:lane0 + j + 1, :] for j in (0, 1))
        rows_b = tuple(grb_ref[0, lane0 + j:lane0 + j + 1, :] for j in (3, 4))
        hf_ref[sl, :] = _scan_chain(
            kf_ref[:, sl], qtf_ref[sl, :], vtf_ref[sla, :], gcf_ref[:, lane0 + 2:lane0 + 3],
            rows_f, c_ref.at[h], m_ref.at[h], srow <= tcol, chunk - 1, dh).astype(hf_ref.dtype)
        hb_ref[sl, :] = _scan_chain(
            kb_ref[:, sl], qtb_ref[sl, :], vtb_ref[sla, :], gcb_ref[:, lane0 + 5:lane0 + 6],
            rows_b, c_ref.at[nh + h], m_ref.at[nh + h], srow >= tcol, 0, dh).astype(hb_ref.dtype)


def _scan(k, qt, vta, gc, gr, km, vm, gcm, *, batch, chunk):
    n, dm = k.shape
    dh = dm // MLSTM_HEADS
    dva = vta.shape[0]
    nc = n // batch // chunk
    fwd = lambda b, c: (b * nc + c, 0)
    bwd = lambda b, c: (b * nc + nc - 1 - c, 0)
    fwd_t = lambda b, c: (0, b * nc + c)
    bwd_t = lambda b, c: (0, b * nc + nc - 1 - c)
    fwd3 = lambda b, c: (b * nc + c, 0, 0)
    bwd3 = lambda b, c: (b * nc + nc - 1 - c, 0, 0)
    per_b = lambda b, c: (b, 0, 0)

    def specs(idx, idx_t, idx3):
        return [pl.BlockSpec((chunk, dm), idx), pl.BlockSpec((dm, chunk), idx_t),
                pl.BlockSpec((dva, chunk), idx_t), pl.BlockSpec((chunk, LANES), idx),
                pl.BlockSpec((1, LANES, chunk), idx3)]

    return pl.pallas_call(
        functools.partial(_scan_body, chunk=chunk, dh=dh),
        grid=(batch, nc),
        in_specs=specs(fwd, fwd_t, fwd3) + specs(bwd, bwd_t, bwd3) + [
            pl.BlockSpec((1, N_META, dm), per_b),
            pl.BlockSpec((1, N_META, dm), per_b),
            pl.BlockSpec((1, N_META, LANES), per_b),
        ],
        out_specs=[pl.BlockSpec((dm, chunk), fwd_t), pl.BlockSpec((dm, chunk), bwd_t)],
        out_shape=[jax.ShapeDtypeStruct((dm, n), BF16)] * 2,
        scratch_shapes=[
            pltpu.VMEM((2 * MLSTM_HEADS, dh + V_PAD, dh), F32),
            pltpu.VMEM((2 * MLSTM_HEADS, 1, 1), F32),
        ],
        compiler_params=_params("parallel", "arbitrary"),
        name="mlstm_scan",
    )(k, qt, vta, gc, gr, k, qt, vta, gc, gr,
      km.reshape(batch, N_META, dm), vm.reshape(batch, N_META, dm), gcm.reshape(batch, N_META, LANES))


def _attn_body(qt_ref, k_ref, vt_ref, km_ref, vmt_ref, o_ref, *, tk, nk):
    qt = qt_ref[...]
    dv = o_ref.shape[1]

    def scores(j):
        s = _dot(k_ref[j * tk:(j + 1) * tk, :], qt)
        return s, jnp.max(s, axis=0, keepdims=True)

    s_meta = _dot(km_ref[...], qt)
    ahead = scores(0)
    m = jnp.maximum(jnp.max(s_meta, axis=0, keepdims=True), ahead[1])
    acc = _dot(vmt_ref[...], jnp.exp2(s_meta - m).astype(BF16))
    for j in range(nk):
        s, s_max = ahead
        if j + 1 < nk:
            ahead = scores(j + 1)
        m_new = jnp.maximum(m, s_max)
        p = jnp.exp2(s - m_new).astype(BF16)
        acc = jnp.exp2(m - m_new) * acc + _dot(vt_ref[:, j * tk:(j + 1) * tk], p)
        m = m_new
    o_ref[...] = (acc[:dv] * (1.0 / acc[dv:dv + 1])).T


def _attn(qt, k, vt, km, vmt, *, batch, tq, tk):
    n = k.shape[0]
    seq = n // batch
    nq = seq // tq
    dva = vt.shape[0] // ATT_HEADS
    dv = dva - V_PAD
    dk = 2 * LANES
    return pl.pallas_call(
        functools.partial(_attn_body, tk=tk, nk=seq // tk),
        grid=(batch, ATT_HEADS, nq),
        in_specs=[
            pl.BlockSpec((dk, tq), lambda b, h, i: (h, b * nq + i)),
            pl.BlockSpec((seq, dk), lambda b, h, i: (b, h)),
            pl.BlockSpec((dva, seq), lambda b, h, i: (h, b)),
            pl.BlockSpec((N_META, dk), lambda b, h, i: (0, h)),
            pl.BlockSpec((dva, N_META), lambda b, h, i: (h, 0)),
        ],
        out_specs=pl.BlockSpec((tq, dv), lambda b, h, i: (b * nq + i, h)),
        out_shape=jax.ShapeDtypeStruct((n, ATT_HEADS * dv), F32),
        compiler_params=_params("parallel", "parallel", "arbitrary"),
        name="attn",
    )(qt, k, vt, km, vmt)


def _outproj_body(o_ref, hf_ref, hb_ref, z_ref, xc_ref, h1_ref, og_ref, gn_ref, sk_ref,
                  wa_ref, wm_ref, g_ref, b_ref, out_ref, *, dh, rows):
    for r in range(out_ref.shape[0] // rows):
        rs = slice(r * rows, (r + 1) * rows)
        y_att = _rms_norm(o_ref[rs, :], og_ref[...]).astype(BF16)
        h_sum = (hf_ref[:, rs].astype(F32) + hb_ref[:, rs].astype(F32)).T
        hs = h_sum * jax.nn.sigmoid(z_ref[rs, :])
        parts = []
        for h in range(MLSTM_HEADS):
            seg = hs[:, h * dh:(h + 1) * dh]
            mu = jnp.mean(seg, axis=-1, keepdims=True)
            sc = seg - mu
            var = jnp.mean(sc * sc, axis=-1, keepdims=True)
            parts.append(sc * lax.rsqrt(var + LN_EPS))
        hn = jnp.concatenate(parts, axis=-1)
        y_ml = (hn * gn_ref[...] + sk_ref[...] * xc_ref[rs, :].astype(F32)).astype(BF16)
        y = _dot(y_att, wa_ref[...]) + _dot(y_ml, wm_ref[...])
        out_ref[rs, :] = _layer_norm(ALPHA * h1_ref[rs, :] + y, g_ref[...], b_ref[...])


def _outproj(o, hf, hb, z, xc, h1, og, gn, sk, wo, g, b, *, tm, rows):
    n, d = h1.shape
    da = o.shape[1]
    dm = hf.shape[0]
    assert da == dm and wo.shape[0] == da + dm
    const = lambda i: (0, 0)
    row = lambda w: pl.BlockSpec((tm, w), lambda i: (i, 0))
    col = pl.BlockSpec((dm, tm), lambda i: (0, i))
    return pl.pallas_call(
        functools.partial(_outproj_body, dh=dm // MLSTM_HEADS, rows=rows),
        grid=(n // tm,),
        in_specs=[row(da), col, col, row(dm), row(dm), row(d),
                  pl.BlockSpec(og.shape, const), pl.BlockSpec(gn.shape, const), pl.BlockSpec(sk.shape, const),
                  pl.BlockSpec((da, d), lambda i: (0, 0), pipeline_mode=pl.Buffered(1)),
                  pl.BlockSpec((dm, d), lambda i: (1, 0), pipeline_mode=pl.Buffered(1)),
                  pl.BlockSpec(g.shape, const), pl.BlockSpec(b.shape, const)],
        out_specs=row(d),
        out_shape=jax.ShapeDtypeStruct((n, d), F32),
        compiler_params=_params("parallel"),
        name="outproj",
    )(o, hf, hb, z, xc, h1, og, gn, sk, wo, wo, g, b)


def _pad_lanes(w):
    return jnp.pad(w, [(0, 0)] * (w.ndim - 1) + [(0, LANES - w.shape[-1])])


def _pick(n, pref):
    for t in pref:
        if n % t == 0:
            return t
    raise ValueError(f"no tile for {n}")


def kernel(x, meta_tokens, ffn1_w_gate, ffn1_w_up, ffn1_w_down, ln1_g, ln1_b, w_in, mla_q_norm_g, mla_w_uq, mla_kv_norm_g, mla_w_ukv, attn_out_g, mlstm_conv_w, mlstm_conv_b, mlstm_w_q, mlstm_w_k, mlstm_w_v, mlstm_w_gates, mlstm_b_gates, mlstm_gn_g, mlstm_skip, w_out, ln2_g, ln2_b, ffn2_w_gate, ffn2_w_up, ffn2_w_down, ln3_g, ln3_b):
    batch, seq, d = x.shape
    assert ffn1_w_gate.shape[0] == DEPTH and meta_tokens.shape[0] == N_META
    n = batch * seq
    ql = mla_q_norm_g.shape[-1]
    kvl = mla_kv_norm_g.shape[-1]
    da = attn_out_g.shape[-1]
    dm = mlstm_gn_g.shape[-1]
    dh = dm // MLSTM_HEADS
    dv = da // ATT_HEADS
    nh = MLSTM_HEADS
    half = QK_ROPE // 2
    assert dv == LANES and dh % LANES == 0 and ql % LANES == 0 and kvl % LANES == 0
    assert seq % MLSTM_CHUNK == 0

    swap = jnp.concatenate([jnp.arange(half, QK_ROPE), jnp.arange(half)])
    win = _win(jnp.swapaxes(w_in[0], 0, 1), ql + kvl)
    wuq = mla_w_uq[0].reshape(ql, ATT_HEADS, QK_NOPE + QK_ROPE)
    wqnt = wuq[:, :, :QK_NOPE].reshape(ql, -1).T.astype(BF16)
    w_qr = wuq[:, :, QK_NOPE:]
    wqrt = jnp.concatenate([w_qr, w_qr[:, :, swap]], axis=-1).reshape(ql, -1).T.astype(BF16)
    wukv = mla_w_ukv[0].reshape(kvl, ATT_HEADS, QK_NOPE + dv)
    wkn = wukv[:, :, :QK_NOPE].reshape(kvl, -1).astype(BF16)
    wvt = wukv[:, :, QK_NOPE:].reshape(kvl, -1).T.astype(BF16)

    wg = mlstm_w_gates[0].reshape(2 * dm, 4, nh)
    bgr = mlstm_b_gates[0].reshape(4, nh)

    def gate_tile(w4, kind_f, kind_b):
        zeros = jnp.zeros_like(w4[..., 0, :])
        cols = [w4[..., kind_f, :]] * 3 + [w4[..., kind_b, :]] * 3 + [zeros] * (GATE_GROUP - 6)
        tile = jnp.stack(cols, axis=-1).reshape(*w4.shape[:-2], nh * GATE_GROUP)
        return _pad_lanes(tile)

    wgate = jnp.concatenate([gate_tile(wg, 0, 2), gate_tile(wg, 1, 3)], axis=-1)
    wgc = wgate[:dm].astype(BF16)
    wgm = wgate[dm:].astype(BF16)
    bg = jnp.concatenate([gate_tile(bgr, 0, 2), gate_tile(bgr, 1, 3)], axis=-1)[None, :]
    cw = jnp.pad(mlstm_conv_w[0], ((0, SUBLANES - CONV_K), (0, 0)))
    cb = mlstm_conv_b[0][None, :]
    k_scale = dh ** -0.5
    assert math.frexp(k_scale)[0] == 0.5
    wk_m = (mlstm_w_k[0] * k_scale).astype(BF16)
    wv_m = mlstm_w_v[0].astype(BF16)
    wqt_m, wvt_m = (jnp.swapaxes(w[0], 1, 2).astype(BF16) for w in (mlstm_w_q, mlstm_w_v))
    row = lambda p: p[0][None, :]

    pos = jnp.arange(N_META + seq, dtype=F32)
    inv = ROPE_BASE ** (-jnp.arange(0, QK_ROPE, 2, dtype=F32) / QK_ROPE)
    ang = pos[:, None] * inv[None, :]
    rot = jnp.concatenate([jnp.cos(ang), jnp.cos(ang), -jnp.sin(ang), jnp.sin(ang)], axis=-1)

    ffn1_f32 = [w[0] for w in (ffn1_w_gate, ffn1_w_up, ffn1_w_down)]
    ffn2_f32 = [w[0] for w in (ffn2_w_gate, ffn2_w_up, ffn2_w_down)]
    tf = _pick(ffn1_f32[0].shape[1], (512, 256, 128))
    tm_ffn = _pick(n, (1024, 512, 256))
    tm_proj = _pick(seq, (256,))

    xr = x.reshape(n, d)
    h1m, *ffn1 = _ffn_ln(meta_tokens.astype(x.dtype), *ffn1_f32, row(ln1_g), row(ln1_b),
                         tm=N_META, tf=tf, cast_weights=True)
    h1, = _ffn_ln(xr, *ffn1, row(ln1_g), row(ln1_b), tm=tm_ffn, tf=tf)
    proj_w = (win, row(mla_q_norm_g), row(mla_kv_norm_g), wqnt, wqrt, wkn, wvt)
    q_a, k_a, vt_a, xm, z, *ffn2, wo = _proj(
        h1, *proj_w, rot[N_META:], tm=tm_proj, dm=dm, side=ffn2_f32 + [w_out[0]])
    _, k_am, vt_am, xm_m, _ = _proj(h1m, *proj_w, rot[:N_META], tm=N_META, dm=dm)

    prep_w = (cw, cb, wk_m, wv_m, wqt_m, wvt_m, wgc, wgm, bg)
    tm_prep = _pick(seq, (512, MLSTM_CHUNK))
    k_m, qt_m, vt_m, xc, gc, gr = _prep(
        xm, xm_m, *prep_w, batch=batch, tm=tm_prep, chunk=MLSTM_CHUNK, meta=False)
    k_mm, v_mm, gc_m = _prep(xm, xm_m, *prep_w, batch=batch, tm=N_META, chunk=N_META, meta=True)
    hf, hb = _scan(k_m, qt_m, vt_m, gc, gr, k_mm, v_mm, gc_m, batch=batch, chunk=MLSTM_CHUNK)

    o = _attn(q_a, k_a, vt_a, k_am, vt_am, batch=batch, tq=_pick(seq, (4096, 2048, 1024, 512, 256)), tk=_pick(seq, (512, 256)))

    h2 = _outproj(o, hf, hb, z, xc, h1, row(attn_out_g), row(mlstm_gn_g), row(mlstm_skip),
                  wo, row(ln2_g), row(ln2_b), tm=_pick(seq, (512, 256)), rows=_pick(seq, (256,)))
    out, = _ffn_ln(h2, *ffn2, row(ln3_g), row(ln3_b), tm=tm_ffn, tf=tf)
    return out.reshape(batch, seq, d)
```
